```python
import math
import jax, jax.numpy as jnp
from jax import lax
import numpy as np

D_MODEL = 1024
BATCH = 8
SEQ = 8192
DEPTH = 1
DEC_BATCH = 16
DEC_SEQ = 4096
PAST_LEN = 128

HEAD_DIM = 64
N_GROUPS = 3
HEADS_PER_GROUP = 4
ATTN_WIDTH = N_GROUPS * HEADS_PER_GROUP * HEAD_DIM
ATTN_OUT_WIDTH = HEADS_PER_GROUP * HEAD_DIM
WINDOWS = (128, 512, 2048)
DILATIONS = (1, 4, 16)
ROT_DIM = HEAD_DIM // 4
ROPE_THETA = 500000.0
Q_BLOCK = 128
NEG_INF = -1e30
CONV_WIDTH = 512
CONV_KERNEL = 31
N_BRANCHES = 2
IN_COLS = 3 * ATTN_WIDTH + 2 * CONV_WIDTH + N_BRANCHES * D_MODEL
PEER_HEADS = 8
N_SUBKEYS = 128
N_EXPERTS = N_SUBKEYS * N_SUBKEYS
PEER_KEY_DIM = 256
PEER_TOPK = 16
TOKEN_BLOCK = 128
EPS = 1e-6

kernel_name = "hybrid_dilated_conv_peer_encoder"


def rms_norm(x, g):
    xf = x.astype(jnp.float32)
    y = xf * lax.rsqrt(jnp.mean(xf * xf, axis=-1, keepdims=True) + EPS)
    return (y * g.astype(jnp.float32)).astype(x.dtype)


def partial_rotary(x, pos):
    half = ROT_DIM // 2
    inv = ROPE_THETA ** (-jnp.arange(half, dtype=jnp.float32) * 2.0 / ROT_DIM)
    ang = pos.astype(jnp.float32)[:, None] * inv[None, :]
    cos = jnp.cos(ang)[None, :, None, None, :]
    sin = jnp.sin(ang)[None, :, None, None, :]
    xf = x.astype(jnp.float32)
    x1 = xf[..., :half]
    x2 = xf[..., half:ROT_DIM]
    out = jnp.concatenate([x1 * cos - x2 * sin, x1 * sin + x2 * cos, xf[..., ROT_DIM:]], axis=-1)
    return out.astype(x.dtype)


def dilation_offsets():
    return np.stack([d * np.arange(-(w // (2 * d)), w // (2 * d) + 1)
                     for w, d in zip(WINDOWS, DILATIONS)]).astype(np.int32)


def dilated_attention(q, k, v):
    B, S = q.shape[0], q.shape[1]
    offs = jnp.asarray(dilation_offsets())
    scale = HEAD_DIM ** -0.5
    gather = jax.vmap(lambda t, ix: jnp.take(t, ix, axis=1), in_axes=(2, 0), out_axes=2)

    def block(i):
        start = i * Q_BLOCK
        pos = start + jnp.arange(Q_BLOCK, dtype=jnp.int32)
        idx = pos[None, :, None] + offs[:, None, :]
        valid = (idx >= 0) & (idx < S)
        idxc = jnp.clip(idx, 0, S - 1)
        qb = lax.dynamic_slice_in_dim(q, start, Q_BLOCK, axis=1)
        kb = gather(k, idxc)
        vb = gather(v, idxc)
        s = jnp.einsum('bqghd,bqgjhd->bqghj', qb, kb).astype(jnp.float32) * scale
        mask = jnp.transpose(valid, (1, 0, 2))[None, :, :, None, :]
        s = jnp.where(mask, s, NEG_INF)
        m = jnp.max(s, axis=-1, keepdims=True)
        p = jnp.exp(s - m)
        den = jnp.sum(p, axis=-1)
        o = jnp.einsum('bqghj,bqgjhd->bqghd', p.astype(vb.dtype), vb).astype(jnp.float32) / den[..., None]
        lse = m[..., 0] + jnp.log(den)
        w = jax.nn.softmax(lse, axis=2)
        out = jnp.sum(w[..., None] * o, axis=2)
        return out.astype(q.dtype)

    out = lax.map(block, jnp.arange(S // Q_BLOCK))
    return jnp.transpose(out, (1, 0, 2, 3, 4)).reshape(B, S, ATTN_OUT_WIDTH)


def conv_module(a, b, dw_w, dw_b, ln_g, ln_b, pw_w, pw_b):
    u = a * jax.nn.sigmoid(b)
    u = lax.conv_general_dilated(
        u, dw_w[:, None, :].astype(u.dtype), window_strides=(1,),
        padding=[(CONV_KERNEL // 2, CONV_KERNEL // 2)],
        dimension_numbers=('NWC', 'WIO', 'NWC'),
        feature_group_count=CONV_WIDTH) + dw_b
    uf = u.astype(jnp.float32)
    mu = jnp.mean(uf, axis=-1, keepdims=True)
    var = jnp.mean(jnp.square(uf - mu), axis=-1, keepdims=True)
    un = ((uf - mu) * lax.rsqrt(var + EPS) * ln_g.astype(jnp.float32) + ln_b.astype(jnp.float32)).astype(u.dtype)
    return jax.nn.silu(un) @ pw_w + pw_b


def peer(x, wq, keys, u, v):
    shape = x.shape
    t = x.reshape(-1, D_MODEL)
    n_blocks = t.shape[0] // TOKEN_BLOCK

    def block(xb):
        q = (xb @ wq).reshape(TOKEN_BLOCK, PEER_HEADS, 2, PEER_KEY_DIM // 2)
        s = jnp.einsum('thcd,hckd->thck', q, keys).astype(jnp.float32)
        s_top, i_top = lax.top_k(s, PEER_TOPK)
        cand = (s_top[:, :, 0, :, None] + s_top[:, :, 1, None, :]).reshape(TOKEN_BLOCK, PEER_HEADS, PEER_TOPK * PEER_TOPK)
        cand_id = (i_top[:, :, 0, :, None] * N_SUBKEYS + i_top[:, :, 1, None, :]).reshape(TOKEN_BLOCK, PEER_HEADS, PEER_TOPK * PEER_TOPK)
        best, sel = lax.top_k(cand, PEER_TOPK)
        ids = jnp.take_along_axis(cand_id, sel, axis=-1)
        g = jax.nn.softmax(best, axis=-1)
        u_sel = u[ids]
        v_sel = v[ids]
        act = jax.nn.gelu(jnp.einsum('td,thkd->thk', xb, u_sel).astype(jnp.float32), approximate=False)
        return jnp.einsum('thk,thkd->td', (g * act).astype(xb.dtype), v_sel)

    out = lax.map(block, t.reshape(n_blocks, TOKEN_BLOCK, D_MODEL))
    return out.reshape(shape)


def encoder_layer(x, norm1_g, w_in, b_gate, w_attn_up, conv_dw_w, conv_dw_b, conv_ln_g, conv_ln_b,
                  conv_pw_w, conv_pw_b, w_out, norm2_g, peer_wq, peer_keys, peer_u, peer_v):
    B, S, _ = x.shape
    h = rms_norm(x, norm1_g)
    proj = h @ w_in
    cuts = np.cumsum([ATTN_WIDTH, ATTN_WIDTH, ATTN_WIDTH, CONV_WIDTH, CONV_WIDTH]).tolist()
    q, k, vv, glu_a, glu_b, gates = jnp.split(proj, cuts, axis=-1)
    hs = (B, S, N_GROUPS, HEADS_PER_GROUP, HEAD_DIM)
    pos = jnp.arange(S, dtype=jnp.int32)
    q = partial_rotary(q.reshape(hs), pos)
    k = partial_rotary(k.reshape(hs), pos)
    attn = dilated_attention(q, k, vv.reshape(hs)) @ w_attn_up
    conv = conv_module(glu_a, glu_b, conv_dw_w, conv_dw_b, conv_ln_g, conv_ln_b, conv_pw_w, conv_pw_b)
    g = jax.nn.sigmoid((gates + b_gate).astype(jnp.float32)).astype(x.dtype).reshape(B, S, N_BRANCHES, D_MODEL)
    mixed = g[:, :, 0, :] * attn + g[:, :, 1, :] * conv
    x = x + mixed @ w_out
    x = x + peer(rms_norm(x, norm2_g), peer_wq, peer_keys, peer_u, peer_v)
    return x


def setup_inputs(seed: int = 0) -> dict:
    key = jax.random.key(seed)
    ks = jax.random.split(key, 20)
    f32 = jnp.float32
    nrm = lambda k, shape, s: jax.random.normal(k, shape, f32) * s
    L = DEPTH
    return {
        "x_prompt": nrm(ks[0], (BATCH, SEQ, D_MODEL), 1.0),
        "x_sample": nrm(ks[1], (DEC_BATCH, DEC_SEQ, D_MODEL), 1.0),
        "norm1_g": 1.0 + nrm(ks[2], (L, D_MODEL), 0.02),
        "w_in": nrm(ks[3], (L, D_MODEL, IN_COLS), D_MODEL ** -0.5),
        "b_gate": nrm(ks[4], (L, N_BRANCHES * D_MODEL), 0.02),
        "w_attn_up": nrm(ks[5], (L, ATTN_OUT_WIDTH, D_MODEL), ATTN_OUT_WIDTH ** -0.5),
        "conv_dw_w": nrm(ks[6], (L, CONV_KERNEL, CONV_WIDTH), CONV_KERNEL ** -0.5),
        "conv_dw_b": nrm(ks[7], (L, CONV_WIDTH), 0.02),
        "conv_ln_g": 1.0 + nrm(ks[8], (L, CONV_WIDTH), 0.02),
        "conv_ln_b": nrm(ks[9], (L, CONV_WIDTH), 0.02),
        "conv_pw_w": nrm(ks[10], (L, CONV_WIDTH, D_MODEL), CONV_WIDTH ** -0.5),
        "conv_pw_b": nrm(ks[11], (L, D_MODEL), 0.02),
        "w_out": nrm(ks[12], (L, D_MODEL, D_MODEL), D_MODEL ** -0.5),
        "norm2_g": 1.0 + nrm(ks[13], (L, D_MODEL), 0.02),
        "peer_wq": nrm(ks[14], (L, D_MODEL, PEER_HEADS * PEER_KEY_DIM), D_MODEL ** -0.5),
        "peer_keys": nrm(ks[15], (L, PEER_HEADS, 2, N_SUBKEYS, PEER_KEY_DIM // 2), (PEER_KEY_DIM // 2) ** -0.5),
        "peer_u": nrm(ks[16], (L, N_EXPERTS, D_MODEL), D_MODEL ** -0.5),
        "peer_v": nrm(ks[17], (L, N_EXPERTS, D_MODEL), (PEER_HEADS * PEER_TOPK) ** -0.5),
        "final_g": 1.0 + nrm(ks[18], (D_MODEL,), 0.02),
    }


def reference(x_prompt, x_sample, norm1_g, w_in, b_gate, w_attn_up, conv_dw_w, conv_dw_b, conv_ln_g,
              conv_ln_b, conv_pw_w, conv_pw_b, w_out, norm2_g, peer_wq, peer_keys, peer_u, peer_v, final_g):
    hp = x_prompt
    hsmp = x_sample
    for l in range(DEPTH):
        args = (norm1_g[l], w_in[l], b_gate[l], w_attn_up[l], conv_dw_w[l], conv_dw_b[l], conv_ln_g[l],
                conv_ln_b[l], conv_pw_w[l], conv_pw_b[l], w_out[l], norm2_g[l], peer_wq[l], peer_keys[l],
                peer_u[l], peer_v[l])
        hp = encoder_layer(hp, *args)
        hsmp = encoder_layer(hsmp, *args)
    y_prompt = rms_norm(hp, final_g)
    y_sample = rms_norm(hsmp, final_g)
    return (y_prompt, y_sample)
```

```python
import functools
import math

import numpy as np
import jax
import jax.numpy as jnp
from jax import lax
from jax.experimental import pallas as pl
from jax.experimental.pallas import tpu as pltpu

f32 = jnp.float32
bf16 = jnp.bfloat16
i32 = jnp.int32

D_MODEL = 1024
HEAD_DIM = 64
N_GROUPS = 3
HEADS_PER_GROUP = 4
GROUP_W = HEADS_PER_GROUP * HEAD_DIM
ATTN_W = N_GROUPS * GROUP_W
WINDOWS = (128, 512, 2048)
DILATIONS = (1, 4, 16)
HALF_WIN = 64
ROT_DIM = HEAD_DIM // 4
ROPE_THETA = 500000.0
NEG_INF = -1e30
CONV_W = 512
CONV_K = 31
CONV_HALO = 16
PEER_HEADS = 8
N_SUBKEYS = 128
N_EXPERTS = N_SUBKEYS * N_SUBKEYS
PEER_TOPK = 16
PEER_SLOTS = PEER_HEADS * PEER_TOPK
EPS = 1e-6

LANES = 128
SUBLANES = 8
ROW_WORDS = D_MODEL // 2 // LANES
VMEM_LIMIT = 56 * 1024 * 1024

T_PROJ = 256
T_PEER = 64
Q_SUB = 128


def _cparams(sem):
    return pltpu.CompilerParams(dimension_semantics=sem, vmem_limit_bytes=VMEM_LIMIT)


def _const_spec(shape):
    nd = len(shape)
    return pl.BlockSpec(shape, lambda *_: (0,) * nd, pipeline_mode=pl.Buffered(1))


def _inproj_kernel(x_ref, g_ref, w_ref, bg_ref, cos_ref, sa_ref, sb_ref,
                   q_ref, k_ref, v_ref, u_ref, gate_ref):
    x = x_ref[...]
    ms = jnp.mean(x * x, axis=-1, keepdims=True)
    h = (x * lax.rsqrt(ms + EPS) * g_ref[...]).astype(bf16)

    def proj(lo, hi):
        return jnp.dot(h, w_ref[:, lo:hi], preferred_element_type=f32)

    cos = cos_ref[...]
    sa = sa_ref[...]
    sb = sb_ref[...]

    def rotary(t, scale):
        outs = []
        for c in range(ATTN_W // LANES):
            tc = t[:, c * LANES:(c + 1) * LANES]
            r = tc * cos + pltpu.roll(tc, LANES - ROT_DIM // 2, 1) * sa + pltpu.roll(tc, ROT_DIM // 2, 1) * sb
            outs.append((r * scale).astype(bf16))
        return jnp.concatenate(outs, axis=1)

    q_ref[...] = rotary(proj(0, ATTN_W), HEAD_DIM ** -0.5)
    k_ref[...] = rotary(proj(ATTN_W, 2 * ATTN_W), 1.0)
    v_ref[...] = proj(2 * ATTN_W, 3 * ATTN_W).astype(bf16)
    c0 = 3 * ATTN_W
    a = proj(c0, c0 + CONV_W)
    b = proj(c0 + CONV_W, c0 + 2 * CONV_W)
    u_ref[...] = a * jax.nn.sigmoid(b)
    gates = proj(c0 + 2 * CONV_W, c0 + 2 * CONV_W + 2 * D_MODEL) + bg_ref[...]
    gate_ref[...] = jax.nn.sigmoid(gates).astype(bf16)


def _inproj(x2, seq, norm1_g, w_in_bf, b_gate, cos_t, sa_t, sb_t):
    n = x2.shape[0]
    nsb = seq // T_PROJ
    in_cols = w_in_bf.shape[1]
    row = lambda w: pl.BlockSpec((T_PROJ, w), lambda i: (i, 0))
    pos = pl.BlockSpec((T_PROJ, LANES), lambda i: (i % nsb, 0))
    return pl.pallas_call(
        _inproj_kernel,
        grid=(n // T_PROJ,),
        in_specs=[row(D_MODEL), _const_spec((1, D_MODEL)), _const_spec((D_MODEL, in_cols)),
                  _const_spec((1, 2 * D_MODEL)), pos, pos, pos],
        out_specs=[row(ATTN_W), row(ATTN_W), row(ATTN_W), row(CONV_W), row(2 * D_MODEL)],
        out_shape=[jax.ShapeDtypeStruct((n, ATTN_W), bf16)] * 3
        + [jax.ShapeDtypeStruct((n, CONV_W), f32), jax.ShapeDtypeStruct((n, 2 * D_MODEL), bf16)],
        compiler_params=_cparams(("parallel",)),
        name="inproj",
    )(x2, norm1_g, w_in_bf, b_gate, cos_t, sa_t, sb_t)


def _attn_kernel(q_ref, kp_ref, kc_ref, kn_ref, vp_ref, vc_ref, vn_ref, o_ref, lse_ref,
                 kw_ref, vw_ref, *, tq, n_rows):
    i = pl.program_id(2)
    kw_ref[0:HALF_WIN] = kp_ref[0]
    kw_ref[HALF_WIN:HALF_WIN + tq] = kc_ref[0]
    kw_ref[HALF_WIN + tq:2 * HALF_WIN + tq] = kn_ref[0]
    vw_ref[0:HALF_WIN] = vp_ref[0]
    vw_ref[HALF_WIN:HALF_WIN + tq] = vc_ref[0]
    vw_ref[HALF_WIN + tq:2 * HALF_WIN + tq] = vn_ref[0]

    win = Q_SUB + 2 * HALF_WIN
    qi = lax.broadcasted_iota(i32, (Q_SUB, win), 0)
    kj = lax.broadcasted_iota(i32, (Q_SUB, win), 1)
    band = (kj - qi >= 0) & (kj - qi <= 2 * HALF_WIN)
    head_of_lane = lax.broadcasted_iota(i32, (1, GROUP_W), 1) // HEAD_DIM

    for s in range(tq // Q_SUB):
        qs = q_ref[0, s * Q_SUB:(s + 1) * Q_SUB, :]
        kwin = kw_ref[s * Q_SUB:s * Q_SUB + win, :]
        vwin = vw_ref[s * Q_SUB:s * Q_SUB + win, :]
        key_row = i * tq + (s * Q_SUB - HALF_WIN) + kj
        ok = band & (key_row >= 0) & (key_row < n_rows)
        o_acc = jnp.zeros((Q_SUB, GROUP_W), f32)
        l_acc = jnp.zeros((Q_SUB, GROUP_W), f32)
        for h in range(HEADS_PER_GROUP):
            hm = head_of_lane == h
            qh = jnp.where(hm, qs, jnp.zeros_like(qs))
            sc = lax.dot_general(qh, kwin, (((1,), (1,)), ((), ())), preferred_element_type=f32)
            sc = jnp.where(ok, sc, NEG_INF)
            m = jnp.max(sc, axis=-1, keepdims=True)
            p = jnp.exp(sc - m)
            den = jnp.sum(p, axis=-1, keepdims=True)
            pv = jnp.dot(p.astype(bf16), vwin, preferred_element_type=f32)
            o_acc = jnp.where(hm, pv / den, o_acc)
            l_acc = jnp.where(hm, m + jnp.log(den), l_acc)
        o_ref[0, s * Q_SUB:(s + 1) * Q_SUB, :] = o_acc
        lse_ref[0, s * Q_SUB:(s + 1) * Q_SUB, :] = l_acc


def _attention_group(q, k, v, batch, seq, g):
    d = DILATIONS[g]
    n_rows = seq // d
    tq = min(512, n_rows)
    nblk = n_rows // tq
    hb = tq // HALF_WIN
    n_halo_blocks = n_rows // HALF_WIN
    view = lambda t: t.reshape(batch, n_rows, d * ATTN_W)
    colblk = lambda r: r * N_GROUPS + g
    cur = pl.BlockSpec((1, tq, GROUP_W), lambda b, r, i: (b, i, colblk(r)))
    prev = pl.BlockSpec((1, HALF_WIN, GROUP_W), lambda b, r, i: (b, jnp.maximum(i * hb - 1, 0), colblk(r)))
    nxt = pl.BlockSpec((1, HALF_WIN, GROUP_W),
                       lambda b, r, i: (b, jnp.minimum((i + 1) * hb, n_halo_blocks - 1), colblk(r)))
    out = pl.BlockSpec((1, tq, GROUP_W), lambda b, r, i: (b, i, r))
    o, lse = pl.pallas_call(
        functools.partial(_attn_kernel, tq=tq, n_rows=n_rows),
        grid=(batch, d, nblk),
        in_specs=[cur, prev, cur, nxt, prev, cur, nxt],
        out_specs=[out, out],
        out_shape=[jax.ShapeDtypeStruct((batch, n_rows, d * GROUP_W), f32)] * 2,
        scratch_shapes=[pltpu.VMEM((tq + 2 * HALF_WIN, GROUP_W), bf16)] * 2,
        compiler_params=_cparams(("parallel", "parallel", "parallel")),
        name=f"attn_g{g}",
    )(view(q), view(k), view(k), view(k), view(v), view(v), view(v))
    return o.reshape(batch * seq, GROUP_W), lse.reshape(batch * seq, GROUP_W)


def _merge_kernel(x_ref, up_ref, uc_ref, un_ref, gate_ref,
                  o0_ref, o1_ref, o2_ref, l0_ref, l1_ref, l2_ref,
                  dww_ref, dwb_ref, lng_ref, lnb_ref, pww_ref, pwb_ref, wup_ref, wout_ref, n2g_ref, wq_ref,
                  x1_ref, h2_ref, qp_ref, ue_ref, cv_ref, *, nsb):
    i = pl.program_id(0)
    t = T_PROJ
    first = (i % nsb) == 0
    last = (i % nsb) == nsb - 1
    ue_ref[0:CONV_HALO] = jnp.where(first, 0.0, up_ref[...])
    ue_ref[CONV_HALO:CONV_HALO + t] = uc_ref[...]
    ue_ref[CONV_HALO + t:2 * CONV_HALO + t] = jnp.where(last, 0.0, un_ref[...])

    rc = 32
    off = CONV_HALO - CONV_K // 2
    for r0 in range(0, t, rc):
        acc = jnp.zeros((rc, CONV_W), f32)
        for j in range(CONV_K):
            acc = acc + ue_ref[r0 + off + j:r0 + off + j + rc, :] * dww_ref[j:j + 1, :]
        cv_ref[r0:r0 + rc, :] = acc
    c = cv_ref[...] + dwb_ref[...]
    mu = jnp.mean(c, axis=-1, keepdims=True)
    cc = c - mu
    var = jnp.mean(cc * cc, axis=-1, keepdims=True)
    un = cc * lax.rsqrt(var + EPS) * lng_ref[...] + lnb_ref[...]
    sw = un * jax.nn.sigmoid(un)
    conv = jnp.dot(sw.astype(bf16), pww_ref[...], preferred_element_type=f32) + pwb_ref[...]

    l0 = l0_ref[...]
    l1 = l1_ref[...]
    l2 = l2_ref[...]
    lm = jnp.maximum(jnp.maximum(l0, l1), l2)
    e0 = jnp.exp(l0 - lm)
    e1 = jnp.exp(l1 - lm)
    e2 = jnp.exp(l2 - lm)
    comb = (e0 * o0_ref[...] + e1 * o1_ref[...] + e2 * o2_ref[...]) / (e0 + e1 + e2)
    attn = jnp.dot(comb.astype(bf16), wup_ref[...], preferred_element_type=f32)

    g_attn = gate_ref[:, 0:D_MODEL].astype(f32)
    g_conv = gate_ref[:, D_MODEL:2 * D_MODEL].astype(f32)
    mixed = g_attn * attn + g_conv * conv
    x1 = x_ref[...] + jnp.dot(mixed.astype(bf16), wout_ref[...], preferred_element_type=f32)
    x1_ref[...] = x1
    ms = jnp.mean(x1 * x1, axis=-1, keepdims=True)
    h2 = x1 * lax.rsqrt(ms + EPS) * n2g_ref[...]
    h2_ref[...] = h2
    qp_ref[...] = jnp.dot(h2.astype(bf16), wq_ref[...], preferred_element_type=f32).astype(bf16)


def _merge(x2, seq, u, gates, os_, ls_, weights):
    n = x2.shape[0]
    t = T_PROJ
    nsb = seq // t
    hb = t // CONV_HALO
    nhalo = n // CONV_HALO
    row = lambda w: pl.BlockSpec((t, w), lambda i: (i, 0))
    prev = pl.BlockSpec((CONV_HALO, CONV_W), lambda i: (jnp.maximum(i * hb - 1, 0), 0))
    nxt = pl.BlockSpec((CONV_HALO, CONV_W), lambda i: (jnp.minimum((i + 1) * hb, nhalo - 1), 0))
    wspecs = [_const_spec(w.shape) for w in weights]
    qw = weights[-1].shape[1]
    return pl.pallas_call(
        functools.partial(_merge_kernel, nsb=nsb),
        grid=(n // t,),
        in_specs=[row(D_MODEL), prev, row(CONV_W), nxt, row(2 * D_MODEL)] + [row(GROUP_W)] * 6 + wspecs,
        out_specs=[row(D_MODEL), row(D_MODEL), row(qw)],
        out_shape=[jax.ShapeDtypeStruct((n, D_MODEL), f32), jax.ShapeDtypeStruct((n, D_MODEL), f32),
                   jax.ShapeDtypeStruct((n, qw), bf16)],
        scratch_shapes=[pltpu.VMEM((t + 2 * CONV_HALO, CONV_W), f32), pltpu.VMEM((t, CONV_W), f32)],
        compiler_params=_cparams(("parallel",)),
        name="merge",
    )(x2, u, u, u, gates, *os_, *ls_, *weights)


def _candidate_slabs():
    slabs = [("row", 0, 0, 16), ("row", 1, 0, 8)]
    for j in range(PEER_TOPK):
        hi = PEER_TOPK // (j + 1)
        if hi > 2:
            slabs.append(("col", j, 2, hi))
    return slabs


def _route_kernel(qp_ref, keys_ref, ids_ref, gate_ref, val_ref, idx_ref, best_ref, idt_ref, gt_ref):
    t = T_PROJ
    k_iota = lax.broadcasted_iota(i32, (N_SUBKEYS, t), 0)
    for hc in range(2 * PEER_HEADS):
        q = qp_ref[:, hc * N_SUBKEYS:(hc + 1) * N_SUBKEYS]
        s = lax.dot_general(keys_ref[hc], q, (((1,), (1,)), ((), ())), preferred_element_type=f32)
        for r in range(PEER_TOPK):
            m = jnp.max(s, axis=0, keepdims=True)
            am = jnp.min(jnp.where(s == m, k_iota, N_SUBKEYS), axis=0, keepdims=True)
            s = jnp.where(k_iota == am, -jnp.inf, s)
            val_ref[hc, pl.ds(r, 1), :] = m
            idx_ref[hc, pl.ds(r, 1), :] = am

    r_iota = lax.broadcasted_iota(i32, (PEER_TOPK, t), 0)
    slabs = _candidate_slabs()
    for h in range(PEER_HEADS):
        v0 = val_ref[2 * h]
        v1 = val_ref[2 * h + 1]
        i0 = idx_ref[2 * h]
        i1 = idx_ref[2 * h + 1]
        cands, flats, eids = [], [], []
        for kind, fixed, lo, hi in slabs:
            if kind == "row":
                c = v0[fixed:fixed + 1, :] + v1
                fl = fixed * PEER_TOPK + r_iota
                ei = i0[fixed:fixed + 1, :] * N_SUBKEYS + i1
            else:
                c = v0 + v1[fixed:fixed + 1, :]
                fl = r_iota * PEER_TOPK + fixed
                ei = i0 * N_SUBKEYS + i1[fixed:fixed + 1, :]
            valid = (r_iota >= lo) & (r_iota < hi)
            cands.append(jnp.where(valid, c, -jnp.inf))
            flats.append(fl)
            eids.append(ei)
        big = PEER_TOPK * PEER_TOPK
        for r in range(PEER_TOPK):
            m = functools.reduce(jnp.maximum, cands)
            m = jnp.max(m, axis=0, keepdims=True)
            fsel = functools.reduce(jnp.minimum, [jnp.where(c == m, fl, big) for c, fl in zip(cands, flats)])
            fsel = jnp.min(fsel, axis=0, keepdims=True)
            hit = [fl == fsel for fl in flats]
            eid = functools.reduce(jnp.maximum, [jnp.where(hh, ei, -1) for hh, ei in zip(hit, eids)])
            eid = jnp.max(eid, axis=0, keepdims=True)
            cands = [jnp.where(hh, -jnp.inf, c) for hh, c in zip(hit, cands)]
            best_ref[pl.ds(r, 1), :] = m
            idt_ref[pl.ds(h * PEER_TOPK + r, 1), :] = eid
        b = best_ref[...]
        e = jnp.exp(b - jnp.max(b, axis=0, keepdims=True))
        gt_ref[h * PEER_TOPK:(h + 1) * PEER_TOPK, :] = e / jnp.sum(e, axis=0, keepdims=True)
    ids_ref[...] = idt_ref[...].T
    gate_ref[...] = gt_ref[...].T


def _route(qp, keys_bf):
    n, qw = qp.shape
    t = T_PROJ
    return pl.pallas_call(
        _route_kernel,
        grid=(n // t,),
        in_specs=[pl.BlockSpec((t, qw), lambda i: (i, 0)), _const_spec(keys_bf.shape)],
        out_specs=[pl.BlockSpec((t, PEER_SLOTS), lambda i: (i, 0))] * 2,
        out_shape=[jax.ShapeDtypeStruct((n, PEER_SLOTS), i32), jax.ShapeDtypeStruct((n, PEER_SLOTS), f32)],
        scratch_shapes=[pltpu.VMEM((2 * PEER_HEADS, PEER_TOPK, t), f32), pltpu.VMEM((2 * PEER_HEADS, PEER_TOPK, t), i32),
                        pltpu.VMEM((PEER_TOPK, t), f32), pltpu.VMEM((PEER_SLOTS, t), i32),
                        pltpu.VMEM((PEER_SLOTS, t), f32)],
        compiler_params=_cparams(("parallel",)),
        name="route",
    )(qp, keys_bf)


def _pack_table(tbl):
    bits = lax.bitcast_convert_type(tbl.astype(bf16), jnp.uint16).astype(jnp.uint32)
    half = D_MODEL // 2
    words = bits[:, :half] | (bits[:, half:] << 16)
    return lax.bitcast_convert_type(words, i32).reshape(tbl.shape[0], ROW_WORDS, LANES)


def _unpack(words):
    lo = pltpu.bitcast(words << 16, f32)
    hi = pltpu.bitcast(words & jnp.int32(-65536), f32)
    return lo, hi


def _split3(x):
    p1 = x.astype(bf16)
    r1 = x - p1.astype(f32)
    p2 = r1.astype(bf16)
    p3 = (r1 - p2.astype(f32)).astype(bf16)
    return p1, p2, p3


def _peer_u_kernel(ids_ref, xr_ref, g_ref, tbl_ref, w_ref, p_ref):
    ones = jnp.ones((LANES, LANES), bf16)
    rows = PEER_SLOTS * ROW_WORDS
    pick = (lax.broadcasted_iota(i32, (rows, LANES), 0) // ROW_WORDS
            == lax.broadcasted_iota(i32, (rows, LANES), 1)).astype(f32)

    def token(t, carry):
        xt = xr_ref[pl.ds(pl.multiple_of(t * SUBLANES, SUBLANES), SUBLANES), :]
        xlo = xt[0:ROW_WORDS]
        xhi = xt[ROW_WORDS:2 * ROW_WORDS]
        for j in range(PEER_SLOTS):
            lo, hi = _unpack(tbl_ref[ids_ref[t, j]])
            p_ref[j * ROW_WORDS:(j + 1) * ROW_WORDS, :] = lo * xlo + hi * xhi
        p1, p2, p3 = _split3(p_ref[...])
        rs = (jnp.dot(p1, ones, preferred_element_type=f32) + jnp.dot(p2, ones, preferred_element_type=f32)
              + jnp.dot(p3, ones, preferred_element_type=f32))
        act = jnp.sum(rs * pick, axis=0, keepdims=True)
        gelu = 0.5 * act * (1.0 + lax.erf(act * (2.0 ** -0.5)))
        w_ref[t] = g_ref[t] * gelu
        return carry

    lax.fori_loop(0, T_PEER, token, 0)


def _peer_u(ids, h2r, gate3, tbl):
    n = ids.shape[0]
    t = T_PEER
    return pl.pallas_call(
        _peer_u_kernel,
        grid=(n // t,),
        in_specs=[pl.BlockSpec((t, PEER_SLOTS), lambda i: (i, 0), memory_space=pltpu.SMEM),
                  pl.BlockSpec((t * SUBLANES, LANES), lambda i: (i, 0)),
                  pl.BlockSpec((t, 1, PEER_SLOTS), lambda i: (i, 0, 0)),
                  _const_spec(tbl.shape)],
        out_specs=pl.BlockSpec((t, 1, PEER_SLOTS), lambda i: (i, 0, 0)),
        out_shape=jax.ShapeDtypeStruct((n, 1, PEER_SLOTS), f32),
        scratch_shapes=[pltpu.VMEM((PEER_SLOTS * ROW_WORDS, LANES), f32)],
        compiler_params=_cparams(("parallel",)),
        name="peer_u",
    )(ids, h2r, gate3, tbl)


def _peer_v_kernel(ids_ref, w_ref, tbl_ref, o_ref):
    nacc = 4

    def token(t, carry):
        acc_lo = [jnp.zeros((ROW_WORDS, LANES), f32) for _ in range(nacc)]
        acc_hi = [jnp.zeros((ROW_WORDS, LANES), f32) for _ in range(nacc)]
        for j in range(PEER_SLOTS):
            lo, hi = _unpack(tbl_ref[ids_ref[t, j]])
            wj = w_ref[t, j]
            acc_lo[j % nacc] = acc_lo[j % nacc] + wj * lo
            acc_hi[j % nacc] = acc_hi[j % nacc] + wj * hi
        lo = (acc_lo[0] + acc_lo[1]) + (acc_lo[2] + acc_lo[3])
        hi = (acc_hi[0] + acc_hi[1]) + (acc_hi[2] + acc_hi[3])
        o_ref[t] = jnp.concatenate([lo, hi], axis=0)
        return carry

    lax.fori_loop(0, T_PEER, token, 0)


def _peer_v(ids, w2, tbl):
    n = ids.shape[0]
    t = T_PEER
    smem = lambda: pl.BlockSpec((t, PEER_SLOTS), lambda i: (i, 0), memory_space=pltpu.SMEM)
    return pl.pallas_call(
        _peer_v_kernel,
        grid=(n // t,),
        in_specs=[smem(), smem(), _const_spec(tbl.shape)],
        out_specs=pl.BlockSpec((t, SUBLANES, LANES), lambda i: (i, 0, 0)),
        out_shape=jax.ShapeDtypeStruct((n, SUBLANES, LANES), f32),
        compiler_params=_cparams(("parallel",)),
        name="peer_v",
    )(ids, w2, tbl)


def _final_kernel(x1_ref, p_ref, g_ref, y_ref):
    x = x1_ref[...] + p_ref[...]
    ms = jnp.mean(x * x, axis=-1, keepdims=True)
    y_ref[...] = x * lax.rsqrt(ms + EPS) * g_ref[...]


def _final(x1, peer_out, final_g):
    n = x1.shape[0]
    t = 512
    row = pl.BlockSpec((t, D_MODEL), lambda i: (i, 0))
    return pl.pallas_call(
        _final_kernel,
        grid=(n // t,),
        in_specs=[row, row, _const_spec((1, D_MODEL))],
        out_specs=row,
        out_shape=jax.ShapeDtypeStruct((n, D_MODEL), f32),
        compiler_params=_cparams(("parallel",)),
        name="final_norm",
    )(x1, peer_out, final_g)


def _rotary_tables(seq):
    half = ROT_DIM // 2
    inv = ROPE_THETA ** (-jnp.arange(half, dtype=f32) * 2.0 / ROT_DIM)
    ang = jnp.arange(seq, dtype=jnp.int32).astype(f32)[:, None] * inv[None, :]
    cos = jnp.cos(ang)
    sin = jnp.sin(ang)
    pad = HEAD_DIM - ROT_DIM
    one = jnp.ones((seq, pad), f32)
    zero = jnp.zeros((seq, pad), f32)
    zh = jnp.zeros((seq, half), f32)
    cos_h = jnp.concatenate([cos, cos, one], axis=1)
    sa_h = jnp.concatenate([-sin, zh, zero], axis=1)
    sb_h = jnp.concatenate([zh, sin, zero], axis=1)
    rep = LANES // HEAD_DIM
    return tuple(jnp.tile(t, (1, rep)) for t in (cos_h, sa_h, sb_h))


def _layer(x, params, tables):
    (norm1_g, w_in_bf, b_gate, merge_w, keys_bf, u_tbl, v_tbl, final_g) = params
    batch, seq, _ = x.shape
    n = batch * seq
    x2 = x.reshape(n, D_MODEL)
    cos_t, sa_t, sb_t = tables
    q, k, v, u, gates = _inproj(x2, seq, norm1_g, w_in_bf, b_gate, cos_t, sa_t, sb_t)
    os_, ls_ = [], []
    for g in range(N_GROUPS):
        o, lse = _attention_group(q, k, v, batch, seq, g)
        os_.append(o)
        ls_.append(lse)
    x1, h2, qp = _merge(x2, seq, u, gates, os_, ls_, merge_w)
    ids, gate = _route(qp, keys_bf)
    w3 = _peer_u(ids, h2.reshape(n * SUBLANES, LANES), gate.reshape(n, 1, PEER_SLOTS), u_tbl)
    pv = _peer_v(ids, w3.reshape(n, PEER_SLOTS), v_tbl)
    y = _final(x1, pv.reshape(n, D_MODEL), final_g)
    return y.reshape(batch, seq, D_MODEL)


def kernel(x_prompt, x_sample, norm1_g, w_in, b_gate, w_attn_up, conv_dw_w, conv_dw_b, conv_ln_g, conv_ln_b,
           conv_pw_w, conv_pw_b, w_out, norm2_g, peer_wq, peer_keys, peer_u, peer_v, final_g):
    depth = w_in.shape[0]
    row = lambda a: a.reshape(1, -1)
    hp, hs = x_prompt, x_sample
    tables = _rotary_tables(max(x_prompt.shape[1], x_sample.shape[1]))
    for l in range(depth):
        merge_w = (conv_dw_w[l], row(conv_dw_b[l]), row(conv_ln_g[l]), row(conv_ln_b[l]),
                   conv_pw_w[l].astype(bf16), row(conv_pw_b[l]), w_attn_up[l].astype(bf16),
                   w_out[l].astype(bf16), row(norm2_g[l]), peer_wq[l].astype(bf16))
        keys_bf = peer_keys[l].astype(bf16).reshape(2 * PEER_HEADS, N_SUBKEYS, -1)
        last = l == depth - 1
        params = (row(norm1_g[l]), w_in[l].astype(bf16), row(b_gate[l]), merge_w, keys_bf,
                  _pack_table(peer_u[l]), _pack_table(peer_v[l]),
                  row(final_g) if last else None)
        hp = _layer(hp, params, tables)
        hs = _layer(hs, params, tables)
    return (hp, hs)
```

```python
import functools
import math

import numpy as np
import jax
import jax.numpy as jnp
from jax import lax
from jax.experimental import pallas as pl
from jax.experimental.pallas import tpu as pltpu

f32 = jnp.float32
bf16 = jnp.bfloat16
i32 = jnp.int32

D_MODEL = 1024
HEAD_DIM = 64
N_GROUPS = 3
HEADS_PER_GROUP = 4
GROUP_W = HEADS_PER_GROUP * HEAD_DIM
ATTN_W = N_GROUPS * GROUP_W
WINDOWS = (128, 512, 2048)
DILATIONS = (1, 4, 16)
HALF_WIN = 64
ROT_DIM = HEAD_DIM // 4
ROPE_THETA = 500000.0
NEG_INF = -1e30
CONV_W = 512
CONV_K = 31
CONV_HALO = 16
PEER_HEADS = 8
N_SUBKEYS = 128
N_EXPERTS = N_SUBKEYS * N_SUBKEYS
PEER_TOPK = 16
PEER_SLOTS = PEER_HEADS * PEER_TOPK
EPS = 1e-6

LANES = 128
SUBLANES = 8
ROW_WORDS = D_MODEL // 2 // LANES
VMEM_LIMIT = 56 * 1024 * 1024

T_PROJ = 256
T_PEER = 64
GATHER_CHUNK = 32
N_CHUNKS = (8 * 16) // GATHER_CHUNK
Q_SUB = 128


def _cparams(sem):
    return pltpu.CompilerParams(dimension_semantics=sem, vmem_limit_bytes=VMEM_LIMIT)


def _const_spec(shape):
    nd = len(shape)
    return pl.BlockSpec(shape, lambda *_: (0,) * nd, pipeline_mode=pl.Buffered(1))


def _inproj_kernel(x_ref, g_ref, w_ref, bg_ref, cos_ref, sa_ref, sb_ref,
                   q_ref, k_ref, v_ref, u_ref, gate_ref):
    x = x_ref[...]
    ms = jnp.mean(x * x, axis=-1, keepdims=True)
    h = (x * lax.rsqrt(ms + EPS) * g_ref[...]).astype(bf16)

    def proj(lo, hi):
        return jnp.dot(h, w_ref[:, lo:hi], preferred_element_type=f32)

    cos = cos_ref[...]
    sa = sa_ref[...]
    sb = sb_ref[...]

    def rotary(t, scale):
        outs = []
        for c in range(ATTN_W // LANES):
            tc = t[:, c * LANES:(c + 1) * LANES]
            r = tc * cos + pltpu.roll(tc, LANES - ROT_DIM // 2, 1) * sa + pltpu.roll(tc, ROT_DIM // 2, 1) * sb
            outs.append((r * scale).astype(bf16))
        return jnp.concatenate(outs, axis=1)

    q_ref[...] = rotary(proj(0, ATTN_W), HEAD_DIM ** -0.5)
    k_ref[...] = rotary(proj(ATTN_W, 2 * ATTN_W), 1.0)
    v_ref[...] = proj(2 * ATTN_W, 3 * ATTN_W).astype(bf16)
    c0 = 3 * ATTN_W
    a = proj(c0, c0 + CONV_W)
    b = proj(c0 + CONV_W, c0 + 2 * CONV_W)
    u_ref[...] = a * jax.nn.sigmoid(b)
    gates = proj(c0 + 2 * CONV_W, c0 + 2 * CONV_W + 2 * D_MODEL) + bg_ref[...]
    gate_ref[...] = jax.nn.sigmoid(gates).astype(bf16)


def _inproj(x2, seq, norm1_g, w_in_bf, b_gate, cos_t, sa_t, sb_t):
    n = x2.shape[0]
    nsb = seq // T_PROJ
    in_cols = w_in_bf.shape[1]
    row = lambda w: pl.BlockSpec((T_PROJ, w), lambda i: (i, 0))
    pos = pl.BlockSpec((T_PROJ, LANES), lambda i: (i % nsb, 0))
    return pl.pallas_call(
        _inproj_kernel,
        grid=(n // T_PROJ,),
        in_specs=[row(D_MODEL), _const_spec((1, D_MODEL)), _const_spec((D_MODEL, in_cols)),
                  _const_spec((1, 2 * D_MODEL)), pos, pos, pos],
        out_specs=[row(ATTN_W), row(ATTN_W), row(ATTN_W), row(CONV_W), row(2 * D_MODEL)],
        out_shape=[jax.ShapeDtypeStruct((n, ATTN_W), bf16)] * 3
        + [jax.ShapeDtypeStruct((n, CONV_W), f32), jax.ShapeDtypeStruct((n, 2 * D_MODEL), bf16)],
        compiler_params=_cparams(("parallel",)),
        name="inproj",
    )(x2, norm1_g, w_in_bf, b_gate, cos_t, sa_t, sb_t)


def _attn_kernel(q_ref, kp_ref, kc_ref, kn_ref, vp_ref, vc_ref, vn_ref, o_ref, lse_ref,
                 kw_ref, vw_ref, *, tq, n_rows):
    i = pl.program_id(2)
    kw_ref[0:HALF_WIN] = kp_ref[0]
    kw_ref[HALF_WIN:HALF_WIN + tq] = kc_ref[0]
    kw_ref[HALF_WIN + tq:2 * HALF_WIN + tq] = kn_ref[0]
    vw_ref[0:HALF_WIN] = vp_ref[0]
    vw_ref[HALF_WIN:HALF_WIN + tq] = vc_ref[0]
    vw_ref[HALF_WIN + tq:2 * HALF_WIN + tq] = vn_ref[0]

    win = Q_SUB + 2 * HALF_WIN
    qi = lax.broadcasted_iota(i32, (Q_SUB, win), 0)
    kj = lax.broadcasted_iota(i32, (Q_SUB, win), 1)
    band = (kj - qi >= 0) & (kj - qi <= 2 * HALF_WIN)
    head_of_lane = lax.broadcasted_iota(i32, (1, GROUP_W), 1) // HEAD_DIM

    for s in range(tq // Q_SUB):
        qs = q_ref[0, s * Q_SUB:(s + 1) * Q_SUB, :]
        kwin = kw_ref[s * Q_SUB:s * Q_SUB + win, :]
        vwin = vw_ref[s * Q_SUB:s * Q_SUB + win, :]
        key_row = i * tq + (s * Q_SUB - HALF_WIN) + kj
        ok = band & (key_row >= 0) & (key_row < n_rows)
        o_acc = jnp.zeros((Q_SUB, GROUP_W), f32)
        l_acc = jnp.zeros((Q_SUB, GROUP_W), f32)
        for h in range(HEADS_PER_GROUP):
            hm = head_of_lane == h
            qh = jnp.where(hm, qs, jnp.zeros_like(qs))
            sc = lax.dot_general(qh, kwin, (((1,), (1,)), ((), ())), preferred_element_type=f32)
            sc = jnp.where(ok, sc, NEG_INF)
            m = jnp.max(sc, axis=-1, keepdims=True)
            p = jnp.exp(sc - m)
            den = jnp.sum(p, axis=-1, keepdims=True)
            pv = jnp.dot(p.astype(bf16), vwin, preferred_element_type=f32)
            o_acc = jnp.where(hm, pv / den, o_acc)
            l_acc = jnp.where(hm, m + jnp.log(den), l_acc)
        o_ref[0, s * Q_SUB:(s + 1) * Q_SUB, :] = o_acc
        lse_ref[0, s * Q_SUB:(s + 1) * Q_SUB, :] = l_acc


def _attention_group(q, k, v, batch, seq, g):
    d = DILATIONS[g]
    n_rows = seq // d
    tq = min(512, n_rows)
    nblk = n_rows // tq
    hb = tq // HALF_WIN
    n_halo_blocks = n_rows // HALF_WIN
    view = lambda t: t.reshape(batch, n_rows, d * ATTN_W)
    colblk = lambda r: r * N_GROUPS + g
    cur = pl.BlockSpec((1, tq, GROUP_W), lambda b, r, i: (b, i, colblk(r)))
    prev = pl.BlockSpec((1, HALF_WIN, GROUP_W), lambda b, r, i: (b, jnp.maximum(i * hb - 1, 0), colblk(r)))
    nxt = pl.BlockSpec((1, HALF_WIN, GROUP_W),
                       lambda b, r, i: (b, jnp.minimum((i + 1) * hb, n_halo_blocks - 1), colblk(r)))
    out = pl.BlockSpec((1, tq, GROUP_W), lambda b, r, i: (b, i, r))
    o, lse = pl.pallas_call(
        functools.partial(_attn_kernel, tq=tq, n_rows=n_rows),
        grid=(batch, d, nblk),
        in_specs=[cur, prev, cur, nxt, prev, cur, nxt],
        out_specs=[out, out],
        out_shape=[jax.ShapeDtypeStruct((batch, n_rows, d * GROUP_W), f32)] * 2,
        scratch_shapes=[pltpu.VMEM((tq + 2 * HALF_WIN, GROUP_W), bf16)] * 2,
        compiler_params=_cparams(("parallel", "parallel", "parallel")),
        name=f"attn_g{g}",
    )(view(q), view(k), view(k), view(k), view(v), view(v), view(v))
    return o.reshape(batch * seq, GROUP_W), lse.reshape(batch * seq, GROUP_W)


def _merge_kernel(x_ref, up_ref, uc_ref, un_ref, gate_ref,
                  o0_ref, o1_ref, o2_ref, l0_ref, l1_ref, l2_ref,
                  dww_ref, dwb_ref, lng_ref, lnb_ref, pww_ref, pwb_ref, wup_ref, wout_ref, n2g_ref, wq_ref,
                  x1_ref, h2_ref, qp_ref, ue_ref, cv_ref, *, nsb):
    i = pl.program_id(0)
    t = T_PROJ
    first = (i % nsb) == 0
    last = (i % nsb) == nsb - 1
    ue_ref[0:CONV_HALO] = jnp.where(first, 0.0, up_ref[...])
    ue_ref[CONV_HALO:CONV_HALO + t] = uc_ref[...]
    ue_ref[CONV_HALO + t:2 * CONV_HALO + t] = jnp.where(last, 0.0, un_ref[...])

    rc = 32
    off = CONV_HALO - CONV_K // 2
    for r0 in range(0, t, rc):
        acc = jnp.zeros((rc, CONV_W), f32)
        for j in range(CONV_K):
            acc = acc + ue_ref[r0 + off + j:r0 + off + j + rc, :] * dww_ref[j:j + 1, :]
        cv_ref[r0:r0 + rc, :] = acc
    c = cv_ref[...] + dwb_ref[...]
    mu = jnp.mean(c, axis=-1, keepdims=True)
    cc = c - mu
    var = jnp.mean(cc * cc, axis=-1, keepdims=True)
    un = cc * lax.rsqrt(var + EPS) * lng_ref[...] + lnb_ref[...]
    sw = un * jax.nn.sigmoid(un)
    conv = jnp.dot(sw.astype(bf16), pww_ref[...], preferred_element_type=f32) + pwb_ref[...]

    l0 = l0_ref[...]
    l1 = l1_ref[...]
    l2 = l2_ref[...]
    lm = jnp.maximum(jnp.maximum(l0, l1), l2)
    e0 = jnp.exp(l0 - lm)
    e1 = jnp.exp(l1 - lm)
    e2 = jnp.exp(l2 - lm)
    comb = (e0 * o0_ref[...] + e1 * o1_ref[...] + e2 * o2_ref[...]) / (e0 + e1 + e2)
    attn = jnp.dot(comb.astype(bf16), wup_ref[...], preferred_element_type=f32)

    g_attn = gate_ref[:, 0:D_MODEL].astype(f32)
    g_conv = gate_ref[:, D_MODEL:2 * D_MODEL].astype(f32)
    mixed = g_attn * attn + g_conv * conv
    x1 = x_ref[...] + jnp.dot(mixed.astype(bf16), wout_ref[...], preferred_element_type=f32)
    x1_ref[...] = x1
    ms = jnp.mean(x1 * x1, axis=-1, keepdims=True)
    h2 = x1 * lax.rsqrt(ms + EPS) * n2g_ref[...]
    h2_ref[...] = h2
    qp_ref[...] = jnp.dot(h2.astype(bf16), wq_ref[...], preferred_element_type=f32).astype(bf16)


def _merge(x2, seq, u, gates, os_, ls_, weights):
    n = x2.shape[0]
    t = T_PROJ
    nsb = seq // t
    hb = t // CONV_HALO
    nhalo = n // CONV_HALO
    row = lambda w: pl.BlockSpec((t, w), lambda i: (i, 0))
    prev = pl.BlockSpec((CONV_HALO, CONV_W), lambda i: (jnp.maximum(i * hb - 1, 0), 0))
    nxt = pl.BlockSpec((CONV_HALO, CONV_W), lambda i: (jnp.minimum((i + 1) * hb, nhalo - 1), 0))
    wspecs = [_const_spec(w.shape) for w in weights]
    qw = weights[-1].shape[1]
    return pl.pallas_call(
        functools.partial(_merge_kernel, nsb=nsb),
        grid=(n // t,),
        in_specs=[row(D_MODEL), prev, row(CONV_W), nxt, row(2 * D_MODEL)] + [row(GROUP_W)] * 6 + wspecs,
        out_specs=[row(D_MODEL), row(D_MODEL), row(qw)],
        out_shape=[jax.ShapeDtypeStruct((n, D_MODEL), f32), jax.ShapeDtypeStruct((n, D_MODEL), f32),
                   jax.ShapeDtypeStruct((n, qw), bf16)],
        scratch_shapes=[pltpu.VMEM((t + 2 * CONV_HALO, CONV_W), f32), pltpu.VMEM((t, CONV_W), f32)],
        compiler_params=_cparams(("parallel",)),
        name="merge",
    )(x2, u, u, u, gates, *os_, *ls_, *weights)


def _candidate_slabs():
    slabs = [("row", 0, 0, 16), ("row", 1, 0, 8)]
    for j in range(PEER_TOPK):
        hi = PEER_TOPK // (j + 1)
        if hi > 2:
            slabs.append(("col", j, 2, hi))
    return slabs


def _route_kernel(qp_ref, keys_ref, ids_ref, gate_ref, val_ref, idx_ref, best_ref, idt_ref, gt_ref):
    t = T_PROJ
    k_iota = lax.broadcasted_iota(i32, (N_SUBKEYS, t), 0)
    for hc in range(2 * PEER_HEADS):
        q = qp_ref[:, hc * N_SUBKEYS:(hc + 1) * N_SUBKEYS]
        s = lax.dot_general(keys_ref[hc], q, (((1,), (1,)), ((), ())), preferred_element_type=f32)
        for r in range(PEER_TOPK):
            m = jnp.max(s, axis=0, keepdims=True)
            am = jnp.min(jnp.where(s == m, k_iota, N_SUBKEYS), axis=0, keepdims=True)
            s = jnp.where(k_iota == am, -jnp.inf, s)
            val_ref[hc, pl.ds(r, 1), :] = m
            idx_ref[hc, pl.ds(r, 1), :] = am

    r_iota = lax.broadcasted_iota(i32, (PEER_TOPK, t), 0)
    slabs = _candidate_slabs()
    for h in range(PEER_HEADS):
        v0 = val_ref[2 * h]
        v1 = val_ref[2 * h + 1]
        i0 = idx_ref[2 * h]
        i1 = idx_ref[2 * h + 1]
        cands, flats, eids = [], [], []
        for kind, fixed, lo, hi in slabs:
            if kind == "row":
                c = v0[fixed:fixed + 1, :] + v1
                fl = fixed * PEER_TOPK + r_iota
                ei = i0[fixed:fixed + 1, :] * N_SUBKEYS + i1
            else:
                c = v0 + v1[fixed:fixed + 1, :]
                fl = r_iota * PEER_TOPK + fixed
                ei = i0 * N_SUBKEYS + i1[fixed:fixed + 1, :]
            valid = (r_iota >= lo) & (r_iota < hi)
            cands.append(jnp.where(valid, c, -jnp.inf))
            flats.append(fl)
            eids.append(ei)
        big = PEER_TOPK * PEER_TOPK
        for r in range(PEER_TOPK):
            m = functools.reduce(jnp.maximum, cands)
            m = jnp.max(m, axis=0, keepdims=True)
            fsel = functools.reduce(jnp.minimum, [jnp.where(c == m, fl, big) for c, fl in zip(cands, flats)])
            fsel = jnp.min(fsel, axis=0, keepdims=True)
            hit = [fl == fsel for fl in flats]
            eid = functools.reduce(jnp.maximum, [jnp.where(hh, ei, -1) for hh, ei in zip(hit, eids)])
            eid = jnp.max(eid, axis=0, keepdims=True)
            cands = [jnp.where(hh, -jnp.inf, c) for hh, c in zip(hit, cands)]
            best_ref[pl.ds(r, 1), :] = m
            idt_ref[pl.ds(h * PEER_TOPK + r, 1), :] = eid
        b = best_ref[...]
        e = jnp.exp(b - jnp.max(b, axis=0, keepdims=True))
        gt_ref[h * PEER_TOPK:(h + 1) * PEER_TOPK, :] = e / jnp.sum(e, axis=0, keepdims=True)
    ids_ref[...] = idt_ref[...].T
    gate_ref[...] = gt_ref[...].T


def _route(qp, keys_bf):
    n, qw = qp.shape
    t = T_PROJ
    return pl.pallas_call(
        _route_kernel,
        grid=(n // t,),
        in_specs=[pl.BlockSpec((t, qw), lambda i: (i, 0)), _const_spec(keys_bf.shape)],
        out_specs=[pl.BlockSpec((t, PEER_SLOTS), lambda i: (i, 0))] * 2,
        out_shape=[jax.ShapeDtypeStruct((n, PEER_SLOTS), i32), jax.ShapeDtypeStruct((n, PEER_SLOTS), f32)],
        scratch_shapes=[pltpu.VMEM((2 * PEER_HEADS, PEER_TOPK, t), f32), pltpu.VMEM((2 * PEER_HEADS, PEER_TOPK, t), i32),
                        pltpu.VMEM((PEER_TOPK, t), f32), pltpu.VMEM((PEER_SLOTS, t), i32),
                        pltpu.VMEM((PEER_SLOTS, t), f32)],
        compiler_params=_cparams(("parallel",)),
        name="route",
    )(qp, keys_bf)


def _pack_table(tbl):
    bits = lax.bitcast_convert_type(tbl.astype(bf16), jnp.uint16).astype(jnp.uint32)
    half = D_MODEL // 2
    words = bits[:, :half] | (bits[:, half:] << 16)
    return lax.bitcast_convert_type(words, i32).reshape(tbl.shape[0], ROW_WORDS, LANES)


def _unpack(words):
    lo = pltpu.bitcast(words << 16, f32)
    hi = pltpu.bitcast(words & jnp.int32(-65536), f32)
    return lo, hi


def _split2(x):
    hi = x.astype(bf16)
    lo = (x - hi.astype(f32)).astype(bf16)
    return hi, lo


def _gather_chunk(ids_ref, tbl_ref, tile_ref, c, base):
    e = None
    for k in range(GATHER_CHUNK):
        e = ids_ref[base + k]
        tile_ref[c, k * ROW_WORDS:(k + 1) * ROW_WORDS, :] = tbl_ref[e]
    return e


def _two_token_pipeline(ids_ref, tbl_ref, tile_a, tile_b, consume, init):
    tile_b[...] = jnp.zeros(tile_b.shape, i32)
    last = T_PEER - 1

    def half_step(t_gather, gather_tile, t_consume, consume_tile, dep):
        base = jnp.minimum(t_gather, last) * PEER_SLOTS
        tc = jnp.clip(t_consume, 0, last)
        acc = init
        for c in range(N_CHUNKS):
            dep = _gather_chunk(ids_ref, tbl_ref, gather_tile, c, base + c * GATHER_CHUNK + (dep >> 31))
            acc = consume(consume_tile, c, tc, acc)
        return dep

    def pair_step(i, dep):
        dep = half_step(2 * i, tile_a, 2 * i - 1, tile_b, dep)
        dep = half_step(2 * i + 1, tile_b, 2 * i, tile_a, dep)
        return dep

    lax.fori_loop(0, T_PEER // 2 + 1, pair_step, jnp.int32(0))


def _tile_scratch():
    return pltpu.VMEM((N_CHUNKS, GATHER_CHUNK * ROW_WORDS, LANES), i32)


def _peer_u_kernel(ids_ref, xr_ref, g_ref, tbl_ref, w_ref, tile_a, tile_b):
    crow = GATHER_CHUNK * ROW_WORDS
    half = crow // 2
    kk = lax.broadcasted_iota(i32, (2 * LANES, LANES), 0)
    nn = lax.broadcasted_iota(i32, (2 * LANES, LANES), 1)
    summer = ((kk < LANES) == (nn % GATHER_CHUNK < GATHER_CHUNK // 2)).astype(bf16)
    rr = lax.broadcasted_iota(i32, (half, LANES), 0)
    ll = lax.broadcasted_iota(i32, (half, LANES), 1)
    own_row = ll % (GATHER_CHUNK // 2) == rr // ROW_WORDS
    lane_chunk = ll // GATHER_CHUNK

    def consume(tile, c, t, acc):
        xt = xr_ref[pl.ds(pl.multiple_of(t * SUBLANES, SUBLANES), SUBLANES), :]
        xlo = xt[0:ROW_WORDS]
        xhi = xt[ROW_WORDS:2 * ROW_WORDS]
        x2lo = pltpu.repeat(jnp.concatenate([xlo, xlo], axis=0), crow // SUBLANES, axis=0)
        x2hi = pltpu.repeat(jnp.concatenate([xhi, xhi], axis=0), crow // SUBLANES, axis=0)
        lo, hi = _unpack(tile[c])
        p = lo * x2lo + hi * x2hi
        p_hi, p_lo = _split2(jnp.concatenate([p[0:half], p[half:crow]], axis=1))
        rs = jnp.dot(p_hi, summer, preferred_element_type=f32) + jnp.dot(p_lo, summer, preferred_element_type=f32)
        acc = acc + jnp.sum(jnp.where(own_row & (lane_chunk == c), rs, 0.0), axis=0, keepdims=True)
        if c == N_CHUNKS - 1:
            gelu = 0.5 * acc * (1.0 + lax.erf(acc * (2.0 ** -0.5)))
            w_ref[t] = g_ref[t] * gelu
        return acc

    _two_token_pipeline(ids_ref, tbl_ref, tile_a, tile_b, consume, jnp.zeros((1, PEER_SLOTS), f32))


def _peer_u(ids_flat, h2r, gate3, tbl):
    n = gate3.shape[0]
    t = T_PEER
    return pl.pallas_call(
        _peer_u_kernel,
        grid=(n // t,),
        in_specs=[pl.BlockSpec((t * PEER_SLOTS,), lambda i: (i,), memory_space=pltpu.SMEM),
                  pl.BlockSpec((t * SUBLANES, LANES), lambda i: (i, 0)),
                  pl.BlockSpec((t, 1, PEER_SLOTS), lambda i: (i, 0, 0)),
                  _const_spec(tbl.shape)],
        out_specs=pl.BlockSpec((t, 1, PEER_SLOTS), lambda i: (i, 0, 0)),
        out_shape=jax.ShapeDtypeStruct((n, 1, PEER_SLOTS), f32),
        scratch_shapes=[_tile_scratch(), _tile_scratch()],
        compiler_params=_cparams(("parallel",)),
        name="peer_u",
    )(ids_flat, h2r, gate3, tbl)


def _chunk_of_tile_row(rho):
    return (rho >> 1) + ROW_WORDS * (rho & 1)


def _peer_v_kernel(ids_ref, w_ref, tbl_ref, o_ref, tile_a, tile_b, wx_ref):
    t_blk = T_PEER
    kc = GATHER_CHUNK * SUBLANES
    ej = lax.broadcasted_iota(i32, (PEER_SLOTS, kc), 0)
    ek = lax.broadcasted_iota(i32, (PEER_SLOTS, kc), 1)
    w_hi, w_lo = _split2(w_ref[...])
    for c in range(N_CHUNKS):
        expand = (ek // SUBLANES + c * GATHER_CHUNK == ej).astype(bf16)
        wx_ref[c, 0:t_blk] = jnp.dot(w_hi, expand, preferred_element_type=f32)
        wx_ref[c, t_blk:2 * t_blk] = jnp.dot(w_lo, expand, preferred_element_type=f32)
    mc = lax.broadcasted_iota(i32, (SUBLANES, kc), 0)
    mk = lax.broadcasted_iota(i32, (SUBLANES, kc), 1)
    mask = _chunk_of_tile_row(mk % SUBLANES) == mc

    def consume(tile, c, t, acc):
        a = pltpu.bitcast(tile[c], bf16)
        r_hi = jnp.broadcast_to(wx_ref[c, pl.ds(t, 1), :], (SUBLANES, kc))
        r_lo = jnp.broadcast_to(wx_ref[c, pl.ds(t_blk + t, 1), :], (SUBLANES, kc))
        lhs = jnp.concatenate([jnp.where(mask, r_hi, 0.0), jnp.where(mask, r_lo, 0.0)], axis=0).astype(bf16)
        acc = acc + jnp.dot(lhs, a, preferred_element_type=f32)
        if c == N_CHUNKS - 1:
            o_ref[t] = acc[0:SUBLANES] + acc[SUBLANES:2 * SUBLANES]
        return acc

    _two_token_pipeline(ids_ref, tbl_ref, tile_a, tile_b, consume, jnp.zeros((2 * SUBLANES, LANES), f32))


def _peer_v(ids_flat, w2, tbl):
    n = w2.shape[0]
    t = T_PEER
    return pl.pallas_call(
        _peer_v_kernel,
        grid=(n // t,),
        in_specs=[pl.BlockSpec((t * PEER_SLOTS,), lambda i: (i,), memory_space=pltpu.SMEM),
                  pl.BlockSpec((t, PEER_SLOTS), lambda i: (i, 0)),
                  _const_spec(tbl.shape)],
        out_specs=pl.BlockSpec((t, SUBLANES, LANES), lambda i: (i, 0, 0)),
        out_shape=jax.ShapeDtypeStruct((n, SUBLANES, LANES), f32),
        scratch_shapes=[_tile_scratch(), _tile_scratch(),
                        pltpu.VMEM((N_CHUNKS, 2 * t, GATHER_CHUNK * SUBLANES), f32)],
        compiler_params=_cparams(("parallel",)),
        name="peer_v",
    )(ids_flat, w2, tbl)


def _final_kernel(x1_ref, p_ref, g_ref, y_ref):
    x = x1_ref[...] + p_ref[...]
    ms = jnp.mean(x * x, axis=-1, keepdims=True)
    y_ref[...] = x * lax.rsqrt(ms + EPS) * g_ref[...]


def _final(x1, peer_out, final_g):
    n = x1.shape[0]
    t = 512
    row = pl.BlockSpec((t, D_MODEL), lambda i: (i, 0))
    return pl.pallas_call(
        _final_kernel,
        grid=(n // t,),
        in_specs=[row, row, _const_spec((1, D_MODEL))],
        out_specs=row,
        out_shape=jax.ShapeDtypeStruct((n, D_MODEL), f32),
        compiler_params=_cparams(("parallel",)),
        name="final_norm",
    )(x1, peer_out, final_g)


def _rotary_tables(seq):
    half = ROT_DIM // 2
    inv = ROPE_THETA ** (-jnp.arange(half, dtype=f32) * 2.0 / ROT_DIM)
    ang = jnp.arange(seq, dtype=jnp.int32).astype(f32)[:, None] * inv[None, :]
    cos = jnp.cos(ang)
    sin = jnp.sin(ang)
    pad = HEAD_DIM - ROT_DIM
    one = jnp.ones((seq, pad), f32)
    zero = jnp.zeros((seq, pad), f32)
    zh = jnp.zeros((seq, half), f32)
    cos_h = jnp.concatenate([cos, cos, one], axis=1)
    sa_h = jnp.concatenate([-sin, zh, zero], axis=1)
    sb_h = jnp.concatenate([zh, sin, zero], axis=1)
    rep = LANES // HEAD_DIM
    return tuple(jnp.tile(t, (1, rep)) for t in (cos_h, sa_h, sb_h))


def _layer(x, params, tables):
    (norm1_g, w_in_bf, b_gate, merge_w, keys_bf, u_tbl, v_tbl, final_g) = params
    batch, seq, _ = x.shape
    n = batch * seq
    x2 = x.reshape(n, D_MODEL)
    cos_t, sa_t, sb_t = tables
    q, k, v, u, gates = _inproj(x2, seq, norm1_g, w_in_bf, b_gate, cos_t, sa_t, sb_t)
    os_, ls_ = [], []
    for g in range(N_GROUPS):
        o, lse = _attention_group(q, k, v, batch, seq, g)
        os_.append(o)
        ls_.append(lse)
    x1, h2, qp = _merge(x2, seq, u, gates, os_, ls_, merge_w)
    ids, gate = _route(qp, keys_bf)
    ids_flat = ids.reshape(n * PEER_SLOTS)
    w3 = _peer_u(ids_flat, h2.reshape(n * SUBLANES, LANES), gate.reshape(n, 1, PEER_SLOTS), u_tbl)
    pv = _peer_v(ids_flat, w3.reshape(n, PEER_SLOTS), v_tbl)
    y = _final(x1, pv.reshape(n, D_MODEL), final_g)
    return y.reshape(batch, seq, D_MODEL)


def kernel(x_prompt, x_sample, norm1_g, w_in, b_gate, w_attn_up, conv_dw_w, conv_dw_b, conv_ln_g, conv_ln_b,
           conv_pw_w, conv_pw_b, w_out, norm2_g, peer_wq, peer_keys, peer_u, peer_v, final_g):
    depth = w_in.shape[0]
    row = lambda a: a.reshape(1, -1)
    hp, hs = x_prompt, x_sample
    tables = _rotary_tables(max(x_prompt.shape[1], x_sample.shape[1]))
    for l in range(depth):
        merge_w = (conv_dw_w[l], row(conv_dw_b[l]), row(conv_ln_g[l]), row(conv_ln_b[l]),
                   conv_pw_w[l].astype(bf16), row(conv_pw_b[l]), w_attn_up[l].astype(bf16),
                   w_out[l].astype(bf16), row(norm2_g[l]), peer_wq[l].astype(bf16))
        keys_bf = peer_keys[l].astype(bf16).reshape(2 * PEER_HEADS, N_SUBKEYS, -1)
        last = l == depth - 1
        params = (row(norm1_g[l]), w_in[l].astype(bf16), row(b_gate[l]), merge_w, keys_bf,
                  _pack_table(peer_u[l]), _pack_table(peer_v[l]),
                  row(final_g) if last else None)
        hp = _layer(hp, params, tables)
        hs = _layer(hs, params, tables)
    return (hp, hs)
```

```python
import dataclasses
import functools
import math

import numpy as np
import jax
import jax.numpy as jnp
from jax import lax
from jax.experimental import pallas as pl
from jax.experimental.pallas import tpu as pltpu
from jax.experimental.pallas import tpu_sc as plsc

f32 = jnp.float32
bf16 = jnp.bfloat16
i32 = jnp.int32

D_MODEL = 1024
HEAD_DIM = 64
N_GROUPS = 3
HEADS_PER_GROUP = 4
GROUP_W = HEADS_PER_GROUP * HEAD_DIM
ATTN_W = N_GROUPS * GROUP_W
WINDOWS = (128, 512, 2048)
DILATIONS = (1, 4, 16)
HALF_WIN = 64
ROT_DIM = HEAD_DIM // 4
ROPE_THETA = 500000.0
NEG_INF = -1e30
CONV_W = 512
CONV_K = 31
CONV_HALO = 16
PEER_HEADS = 8
N_SUBKEYS = 128
N_EXPERTS = N_SUBKEYS * N_SUBKEYS
PEER_TOPK = 16
PEER_SLOTS = PEER_HEADS * PEER_TOPK
EPS = 1e-6

LANES = 128
SUBLANES = 8
ROW_WORDS = D_MODEL // 2 // LANES
VMEM_LIMIT = 56 * 1024 * 1024

T_PROJ = 256
T_PEER = 64
GATHER_CHUNK = 32
N_CHUNKS = (8 * 16) // GATHER_CHUNK
PEER_TOKEN_CHUNKS = 4

SC_CORES = 2
SC_SUBCORES = 16
SC_LANES = 16
SC_WORKERS = SC_CORES * SC_SUBCORES
ROW_WORDS_FLAT = D_MODEL // 2
SC_HALF = PEER_SLOTS // 2
SC_DIM_BLOCK = 128
SC_ROW_GROUP = 8
Q_SUB = 128


def _cparams(sem):
    return pltpu.CompilerParams(dimension_semantics=sem, vmem_limit_bytes=VMEM_LIMIT)


def _const_spec(shape):
    nd = len(shape)
    return pl.BlockSpec(shape, lambda *_: (0,) * nd, pipeline_mode=pl.Buffered(1))


def _inproj_kernel(x_ref, g_ref, w_ref, bg_ref, cos_ref, sa_ref, sb_ref,
                   q_ref, k_ref, v_ref, u_ref, gate_ref):
    x = x_ref[...]
    ms = jnp.mean(x * x, axis=-1, keepdims=True)
    h = (x * lax.rsqrt(ms + EPS) * g_ref[...]).astype(bf16)

    def proj(lo, hi):
        return jnp.dot(h, w_ref[:, lo:hi], preferred_element_type=f32)

    cos = cos_ref[...]
    sa = sa_ref[...]
    sb = sb_ref[...]

    def rotary(t, scale):
        outs = []
        for c in range(ATTN_W // LANES):
            tc = t[:, c * LANES:(c + 1) * LANES]
            r = tc * cos + pltpu.roll(tc, LANES - ROT_DIM // 2, 1) * sa + pltpu.roll(tc, ROT_DIM // 2, 1) * sb
            outs.append((r * scale).astype(bf16))
        return jnp.concatenate(outs, axis=1)

    q_ref[...] = rotary(proj(0, ATTN_W), HEAD_DIM ** -0.5)
    k_ref[...] = rotary(proj(ATTN_W, 2 * ATTN_W), 1.0)
    v_ref[...] = proj(2 * ATTN_W, 3 * ATTN_W).astype(bf16)
    c0 = 3 * ATTN_W
    a = proj(c0, c0 + CONV_W)
    b = proj(c0 + CONV_W, c0 + 2 * CONV_W)
    u_ref[...] = a * jax.nn.sigmoid(b)
    gates = proj(c0 + 2 * CONV_W, c0 + 2 * CONV_W + 2 * D_MODEL) + bg_ref[...]
    gate_ref[...] = jax.nn.sigmoid(gates).astype(bf16)


def _inproj(x2, seq, norm1_g, w_in_bf, b_gate, cos_t, sa_t, sb_t):
    n = x2.shape[0]
    nsb = seq // T_PROJ
    in_cols = w_in_bf.shape[1]
    row = lambda w: pl.BlockSpec((T_PROJ, w), lambda i: (i, 0))
    pos = pl.BlockSpec((T_PROJ, LANES), lambda i: (i % nsb, 0))
    return pl.pallas_call(
        _inproj_kernel,
        grid=(n // T_PROJ,),
        in_specs=[row(D_MODEL), _const_spec((1, D_MODEL)), _const_spec((D_MODEL, in_cols)),
                  _const_spec((1, 2 * D_MODEL)), pos, pos, pos],
        out_specs=[row(ATTN_W), row(ATTN_W), row(ATTN_W), row(CONV_W), row(2 * D_MODEL)],
        out_shape=[jax.ShapeDtypeStruct((n, ATTN_W), bf16)] * 3
        + [jax.ShapeDtypeStruct((n, CONV_W), f32), jax.ShapeDtypeStruct((n, 2 * D_MODEL), bf16)],
        compiler_params=_cparams(("parallel",)),
        name="inproj",
    )(x2, norm1_g, w_in_bf, b_gate, cos_t, sa_t, sb_t)


def _attn_kernel(q_ref, kp_ref, kc_ref, kn_ref, vp_ref, vc_ref, vn_ref, o_ref, lse_ref,
                 kw_ref, vw_ref, *, tq, n_rows):
    i = pl.program_id(2)
    kw_ref[0:HALF_WIN] = kp_ref[0]
    kw_ref[HALF_WIN:HALF_WIN + tq] = kc_ref[0]
    kw_ref[HALF_WIN + tq:2 * HALF_WIN + tq] = kn_ref[0]
    vw_ref[0:HALF_WIN] = vp_ref[0]
    vw_ref[HALF_WIN:HALF_WIN + tq] = vc_ref[0]
    vw_ref[HALF_WIN + tq:2 * HALF_WIN + tq] = vn_ref[0]

    win = Q_SUB + 2 * HALF_WIN
    qi = lax.broadcasted_iota(i32, (Q_SUB, win), 0)
    kj = lax.broadcasted_iota(i32, (Q_SUB, win), 1)
    band = (kj - qi >= 0) & (kj - qi <= 2 * HALF_WIN)
    head_of_lane = lax.broadcasted_iota(i32, (1, GROUP_W), 1) // HEAD_DIM

    for s in range(tq // Q_SUB):
        qs = q_ref[0, s * Q_SUB:(s + 1) * Q_SUB, :]
        kwin = kw_ref[s * Q_SUB:s * Q_SUB + win, :]
        vwin = vw_ref[s * Q_SUB:s * Q_SUB + win, :]
        key_row = i * tq + (s * Q_SUB - HALF_WIN) + kj
        ok = band & (key_row >= 0) & (key_row < n_rows)
        o_acc = jnp.zeros((Q_SUB, GROUP_W), f32)
        l_acc = jnp.zeros((Q_SUB, GROUP_W), f32)
        for h in range(HEADS_PER_GROUP):
            hm = head_of_lane == h
            qh = jnp.where(hm, qs, jnp.zeros_like(qs))
            sc = lax.dot_general(qh, kwin, (((1,), (1,)), ((), ())), preferred_element_type=f32)
            sc = jnp.where(ok, sc, NEG_INF)
            m = jnp.max(sc, axis=-1, keepdims=True)
            p = jnp.exp(sc - m)
            den = jnp.sum(p, axis=-1, keepdims=True)
            pv = jnp.dot(p.astype(bf16), vwin, preferred_element_type=f32)
            o_acc = jnp.where(hm, pv / den, o_acc)
            l_acc = jnp.where(hm, m + jnp.log(den), l_acc)
        o_ref[0, s * Q_SUB:(s + 1) * Q_SUB, :] = o_acc
        lse_ref[0, s * Q_SUB:(s + 1) * Q_SUB, :] = l_acc


def _attention_group(q, k, v, batch, seq, g):
    d = DILATIONS[g]
    n_rows = seq // d
    tq = min(512, n_rows)
    nblk = n_rows // tq
    hb = tq // HALF_WIN
    n_halo_blocks = n_rows // HALF_WIN
    view = lambda t: t.reshape(batch, n_rows, d * ATTN_W)
    colblk = lambda r: r * N_GROUPS + g
    cur = pl.BlockSpec((1, tq, GROUP_W), lambda b, r, i: (b, i, colblk(r)))
    prev = pl.BlockSpec((1, HALF_WIN, GROUP_W), lambda b, r, i: (b, jnp.maximum(i * hb - 1, 0), colblk(r)))
    nxt = pl.BlockSpec((1, HALF_WIN, GROUP_W),
                       lambda b, r, i: (b, jnp.minimum((i + 1) * hb, n_halo_blocks - 1), colblk(r)))
    out = pl.BlockSpec((1, tq, GROUP_W), lambda b, r, i: (b, i, r))
    o, lse = pl.pallas_call(
        functools.partial(_attn_kernel, tq=tq, n_rows=n_rows),
        grid=(batch, d, nblk),
        in_specs=[cur, prev, cur, nxt, prev, cur, nxt],
        out_specs=[out, out],
        out_shape=[jax.ShapeDtypeStruct((batch, n_rows, d * GROUP_W), f32)] * 2,
        scratch_shapes=[pltpu.VMEM((tq + 2 * HALF_WIN, GROUP_W), bf16)] * 2,
        compiler_params=_cparams(("parallel", "parallel", "parallel")),
        name=f"attn_g{g}",
    )(view(q), view(k), view(k), view(k), view(v), view(v), view(v))
    return o.reshape(batch * seq, GROUP_W), lse.reshape(batch * seq, GROUP_W)


def _merge_kernel(x_ref, up_ref, uc_ref, un_ref, gate_ref,
                  o0_ref, o1_ref, o2_ref, l0_ref, l1_ref, l2_ref,
                  dww_ref, dwb_ref, lng_ref, lnb_ref, pww_ref, pwb_ref, wup_ref, wout_ref, n2g_ref, wq_ref,
                  x1_ref, h2_ref, qp_ref, ue_ref, cv_ref, *, nsb):
    i = pl.program_id(0)
    t = T_PROJ
    first = (i % nsb) == 0
    last = (i % nsb) == nsb - 1
    ue_ref[0:CONV_HALO] = jnp.where(first, 0.0, up_ref[...])
    ue_ref[CONV_HALO:CONV_HALO + t] = uc_ref[...]
    ue_ref[CONV_HALO + t:2 * CONV_HALO + t] = jnp.where(last, 0.0, un_ref[...])

    rc = 32
    off = CONV_HALO - CONV_K // 2
    for r0 in range(0, t, rc):
        acc = jnp.zeros((rc, CONV_W), f32)
        for j in range(CONV_K):
            acc = acc + ue_ref[r0 + off + j:r0 + off + j + rc, :] * dww_ref[j:j + 1, :]
        cv_ref[r0:r0 + rc, :] = acc
    c = cv_ref[...] + dwb_ref[...]
    mu = jnp.mean(c, axis=-1, keepdims=True)
    cc = c - mu
    var = jnp.mean(cc * cc, axis=-1, keepdims=True)
    un = cc * lax.rsqrt(var + EPS) * lng_ref[...] + lnb_ref[...]
    sw = un * jax.nn.sigmoid(un)
    conv = jnp.dot(sw.astype(bf16), pww_ref[...], preferred_element_type=f32) + pwb_ref[...]

    l0 = l0_ref[...]
    l1 = l1_ref[...]
    l2 = l2_ref[...]
    lm = jnp.maximum(jnp.maximum(l0, l1), l2)
    e0 = jnp.exp(l0 - lm)
    e1 = jnp.exp(l1 - lm)
    e2 = jnp.exp(l2 - lm)
    comb = (e0 * o0_ref[...] + e1 * o1_ref[...] + e2 * o2_ref[...]) / (e0 + e1 + e2)
    attn = jnp.dot(comb.astype(bf16), wup_ref[...], preferred_element_type=f32)

    g_attn = gate_ref[:, 0:D_MODEL].astype(f32)
    g_conv = gate_ref[:, D_MODEL:2 * D_MODEL].astype(f32)
    mixed = g_attn * attn + g_conv * conv
    x1 = x_ref[...] + jnp.dot(mixed.astype(bf16), wout_ref[...], preferred_element_type=f32)
    x1_ref[...] = x1
    ms = jnp.mean(x1 * x1, axis=-1, keepdims=True)
    h2 = x1 * lax.rsqrt(ms + EPS) * n2g_ref[...]
    h2_ref[...] = h2
    qp_ref[...] = jnp.dot(h2.astype(bf16), wq_ref[...], preferred_element_type=f32).astype(bf16)


def _merge(x2, seq, u, gates, os_, ls_, weights):
    n = x2.shape[0]
    t = T_PROJ
    nsb = seq // t
    hb = t // CONV_HALO
    nhalo = n // CONV_HALO
    row = lambda w: pl.BlockSpec((t, w), lambda i: (i, 0))
    prev = pl.BlockSpec((CONV_HALO, CONV_W), lambda i: (jnp.maximum(i * hb - 1, 0), 0))
    nxt = pl.BlockSpec((CONV_HALO, CONV_W), lambda i: (jnp.minimum((i + 1) * hb, nhalo - 1), 0))
    wspecs = [_const_spec(w.shape) for w in weights]
    qw = weights[-1].shape[1]
    return pl.pallas_call(
        functools.partial(_merge_kernel, nsb=nsb),
        grid=(n // t,),
        in_specs=[row(D_MODEL), prev, row(CONV_W), nxt, row(2 * D_MODEL)] + [row(GROUP_W)] * 6 + wspecs,
        out_specs=[row(D_MODEL), row(D_MODEL), row(qw)],
        out_shape=[jax.ShapeDtypeStruct((n, D_MODEL), f32), jax.ShapeDtypeStruct((n, D_MODEL), f32),
                   jax.ShapeDtypeStruct((n, qw), bf16)],
        scratch_shapes=[pltpu.VMEM((t + 2 * CONV_HALO, CONV_W), f32), pltpu.VMEM((t, CONV_W), f32)],
        compiler_params=_cparams(("parallel",)),
        name="merge",
    )(x2, u, u, u, gates, *os_, *ls_, *weights)


def _candidate_slabs():
    slabs = [("row", 0, 0, 16), ("row", 1, 0, 8)]
    for j in range(PEER_TOPK):
        hi = PEER_TOPK // (j + 1)
        if hi > 2:
            slabs.append(("col", j, 2, hi))
    return slabs


def _route_kernel(qp_ref, keys_ref, ids_ref, gate_ref, val_ref, idx_ref, best_ref, idt_ref, gt_ref):
    t = T_PROJ
    k_iota = lax.broadcasted_iota(i32, (N_SUBKEYS, t), 0)
    for hc in range(2 * PEER_HEADS):
        q = qp_ref[:, hc * N_SUBKEYS:(hc + 1) * N_SUBKEYS]
        s = lax.dot_general(keys_ref[hc], q, (((1,), (1,)), ((), ())), preferred_element_type=f32)
        for r in range(PEER_TOPK):
            m = jnp.max(s, axis=0, keepdims=True)
            am = jnp.min(jnp.where(s == m, k_iota, N_SUBKEYS), axis=0, keepdims=True)
            s = jnp.where(k_iota == am, -jnp.inf, s)
            val_ref[hc, pl.ds(r, 1), :] = m
            idx_ref[hc, pl.ds(r, 1), :] = am

    r_iota = lax.broadcasted_iota(i32, (PEER_TOPK, t), 0)
    slabs = _candidate_slabs()
    for h in range(PEER_HEADS):
        v0 = val_ref[2 * h]
        v1 = val_ref[2 * h + 1]
        i0 = idx_ref[2 * h]
        i1 = idx_ref[2 * h + 1]
        cands, flats, eids = [], [], []
        for kind, fixed, lo, hi in slabs:
            if kind == "row":
                c = v0[fixed:fixed + 1, :] + v1
                fl = fixed * PEER_TOPK + r_iota
                ei = i0[fixed:fixed + 1, :] * N_SUBKEYS + i1
            else:
                c = v0 + v1[fixed:fixed + 1, :]
                fl = r_iota * PEER_TOPK + fixed
                ei = i0 * N_SUBKEYS + i1[fixed:fixed + 1, :]
            valid = (r_iota >= lo) & (r_iota < hi)
            cands.append(jnp.where(valid, c, -jnp.inf))
            flats.append(fl)
            eids.append(ei)
        big = PEER_TOPK * PEER_TOPK
        for r in range(PEER_TOPK):
            m = functools.reduce(jnp.maximum, cands)
            m = jnp.max(m, axis=0, keepdims=True)
            fsel = functools.reduce(jnp.minimum, [jnp.where(c == m, fl, big) for c, fl in zip(cands, flats)])
            fsel = jnp.min(fsel, axis=0, keepdims=True)
            hit = [fl == fsel for fl in flats]
            eid = functools.reduce(jnp.maximum, [jnp.where(hh, ei, -1) for hh, ei in zip(hit, eids)])
            eid = jnp.max(eid, axis=0, keepdims=True)
            cands = [jnp.where(hh, -jnp.inf, c) for hh, c in zip(hit, cands)]
            best_ref[pl.ds(r, 1), :] = m
            idt_ref[pl.ds(h * PEER_TOPK + r, 1), :] = eid
        b = best_ref[...]
        e = jnp.exp(b - jnp.max(b, axis=0, keepdims=True))
        gt_ref[h * PEER_TOPK:(h + 1) * PEER_TOPK, :] = e / jnp.sum(e, axis=0, keepdims=True)
    ids_ref[...] = idt_ref[...].T
    gate_ref[...] = gt_ref[...].T


def _route(qp, keys_bf):
    n, qw = qp.shape
    t = T_PROJ
    return pl.pallas_call(
        _route_kernel,
        grid=(n // t,),
        in_specs=[pl.BlockSpec((t, qw), lambda i: (i, 0)), _const_spec(keys_bf.shape)],
        out_specs=[pl.BlockSpec((t, PEER_SLOTS), lambda i: (i, 0))] * 2,
        out_shape=[jax.ShapeDtypeStruct((n, PEER_SLOTS), i32), jax.ShapeDtypeStruct((n, PEER_SLOTS), f32)],
        scratch_shapes=[pltpu.VMEM((2 * PEER_HEADS, PEER_TOPK, t), f32), pltpu.VMEM((2 * PEER_HEADS, PEER_TOPK, t), i32),
                        pltpu.VMEM((PEER_TOPK, t), f32), pltpu.VMEM((PEER_SLOTS, t), i32),
                        pltpu.VMEM((PEER_SLOTS, t), f32)],
        compiler_params=_cparams(("parallel",)),
        name="route",
    )(qp, keys_bf)


def _pack_table(tbl):
    bits = lax.bitcast_convert_type(tbl.astype(bf16), jnp.uint16).astype(jnp.uint32)
    half = D_MODEL // 2
    words = bits[:, :half] | (bits[:, half:] << 16)
    return lax.bitcast_convert_type(words, i32).reshape(tbl.shape[0], ROW_WORDS, LANES)


def _unpack(words):
    lo = pltpu.bitcast(words << 16, f32)
    hi = pltpu.bitcast(words & jnp.int32(-65536), f32)
    return lo, hi


def _split2(x):
    hi = x.astype(bf16)
    lo = (x - hi.astype(f32)).astype(bf16)
    return hi, lo


def _gather_chunk(ids_ref, tbl_ref, tile_ref, c, base):
    e = None
    for k in range(GATHER_CHUNK):
        e = ids_ref[base + k]
        tile_ref[c, k * ROW_WORDS:(k + 1) * ROW_WORDS, :] = tbl_ref[e]
    return e


def _two_token_pipeline(ids_ref, tbl_ref, tile_a, tile_b, consume, init):
    tile_b[...] = jnp.zeros(tile_b.shape, i32)
    last = T_PEER - 1

    def half_step(t_gather, gather_tile, t_consume, consume_tile, dep):
        base = jnp.minimum(t_gather, last) * PEER_SLOTS
        tc = jnp.clip(t_consume, 0, last)
        acc = init
        for c in range(N_CHUNKS):
            dep = _gather_chunk(ids_ref, tbl_ref, gather_tile, c, base + c * GATHER_CHUNK + (dep >> 31))
            acc = consume(consume_tile, c, tc, acc)
        return dep

    def pair_step(i, dep):
        dep = half_step(2 * i, tile_a, 2 * i - 1, tile_b, dep)
        dep = half_step(2 * i + 1, tile_b, 2 * i, tile_a, dep)
        return dep

    lax.fori_loop(0, T_PEER // 2 + 1, pair_step, jnp.int32(0))


def _tile_scratch():
    return pltpu.VMEM((N_CHUNKS, GATHER_CHUNK * ROW_WORDS, LANES), i32)


def _peer_u_kernel(ids_ref, xr_ref, g_ref, tbl_ref, w_ref, tile_a, tile_b):
    crow = GATHER_CHUNK * ROW_WORDS
    half = crow // 2
    kk = lax.broadcasted_iota(i32, (2 * LANES, LANES), 0)
    nn = lax.broadcasted_iota(i32, (2 * LANES, LANES), 1)
    summer = ((kk < LANES) == (nn % GATHER_CHUNK < GATHER_CHUNK // 2)).astype(bf16)
    rr = lax.broadcasted_iota(i32, (half, LANES), 0)
    ll = lax.broadcasted_iota(i32, (half, LANES), 1)
    own_row = ll % (GATHER_CHUNK // 2) == rr // ROW_WORDS
    lane_chunk = ll // GATHER_CHUNK

    def consume(tile, c, t, acc):
        xt = xr_ref[pl.ds(pl.multiple_of(t * SUBLANES, SUBLANES), SUBLANES), :]
        xlo = xt[0:ROW_WORDS]
        xhi = xt[ROW_WORDS:2 * ROW_WORDS]
        x2lo = pltpu.repeat(jnp.concatenate([xlo, xlo], axis=0), crow // SUBLANES, axis=0)
        x2hi = pltpu.repeat(jnp.concatenate([xhi, xhi], axis=0), crow // SUBLANES, axis=0)
        lo, hi = _unpack(tile[c])
        p = lo * x2lo + hi * x2hi
        p_hi, p_lo = _split2(jnp.concatenate([p[0:half], p[half:crow]], axis=1))
        rs = jnp.dot(p_hi, summer, preferred_element_type=f32) + jnp.dot(p_lo, summer, preferred_element_type=f32)
        acc = acc + jnp.sum(jnp.where(own_row & (lane_chunk == c), rs, 0.0), axis=0, keepdims=True)
        if c == N_CHUNKS - 1:
            gelu = 0.5 * acc * (1.0 + lax.erf(acc * (2.0 ** -0.5)))
            w_ref[t] = g_ref[t] * gelu
        return acc

    _two_token_pipeline(ids_ref, tbl_ref, tile_a, tile_b, consume, jnp.zeros((1, PEER_SLOTS), f32))


def _peer_u(ids_flat, h2r, gate3, tbl, tok_off, n_tok):
    t = T_PEER
    b0 = tok_off // t
    return pl.pallas_call(
        _peer_u_kernel,
        grid=(n_tok // t,),
        in_specs=[pl.BlockSpec((t * PEER_SLOTS,), lambda i: (i + b0,), memory_space=pltpu.SMEM),
                  pl.BlockSpec((t * SUBLANES, LANES), lambda i: (i + b0, 0)),
                  pl.BlockSpec((t, 1, PEER_SLOTS), lambda i: (i + b0, 0, 0)),
                  _const_spec(tbl.shape)],
        out_specs=pl.BlockSpec((t, 1, PEER_SLOTS), lambda i: (i, 0, 0)),
        out_shape=jax.ShapeDtypeStruct((n_tok, 1, PEER_SLOTS), f32),
        scratch_shapes=[_tile_scratch(), _tile_scratch()],
        compiler_params=_cparams(("parallel",)),
        name="peer_u",
    )(ids_flat, h2r, gate3, tbl)


def _peer_v_sc(ids, w, tbl_words, tok_off):
    n_tok = w.shape[0]
    tpw = n_tok // SC_WORKERS
    lanes = SC_LANES
    mesh = plsc.VectorSubcoreMesh(core_axis_name="c", subcore_axis_name="s")

    @functools.partial(
        pl.kernel, mesh=mesh,
        out_type=jax.ShapeDtypeStruct((n_tok, D_MODEL), f32),
        scratch_types=[
            pltpu.VMEM((SC_HALF,), i32), pltpu.VMEM((SC_HALF,), i32),
            pltpu.VMEM((SC_HALF, ROW_WORDS_FLAT), i32), pltpu.VMEM((SC_HALF, ROW_WORDS_FLAT), i32),
            pltpu.VMEM((PEER_SLOTS,), f32),
            pltpu.VMEM((D_MODEL,), f32),
            pltpu.SemaphoreType.DMA, pltpu.SemaphoreType.DMA,
        ],
        compiler_params=dataclasses.replace(pltpu.CompilerParams(), needs_layout_passes=False),
        name="peer_v_sc",
    )
    def run(ids_hbm, w_hbm, tbl_hbm, out_hbm, idx_a, idx_b, rows_a, rows_b, w_v, out_v, sem_a, sem_b):
        wid = lax.axis_index("s") * SC_CORES + lax.axis_index("c")
        base = wid * tpw

        def accumulate(rows, w_off):
            for db in range(ROW_WORDS_FLAT // SC_DIM_BLOCK):
                def group_body(g, carry):
                    accs = []
                    for wc in range(SC_DIM_BLOCK // lanes):
                        accs.append(out_v[pl.ds(db * SC_DIM_BLOCK + wc * lanes, lanes)])
                        accs.append(out_v[pl.ds(ROW_WORDS_FLAT + db * SC_DIM_BLOCK + wc * lanes, lanes)])
                    j0 = g * SC_ROW_GROUP
                    wchunk = w_v[pl.ds(pl.multiple_of((w_off + j0) // lanes * lanes, lanes), lanes)]
                    sub = (w_off + j0) % lanes
                    for r in range(SC_ROW_GROUP):
                        wj = jnp.take(wchunk, jnp.full((lanes,), sub + r, i32))
                        for wc in range(SC_DIM_BLOCK // lanes):
                            word = rows[j0 + r, pl.ds(db * SC_DIM_BLOCK + wc * lanes, lanes)]
                            lo = lax.bitcast_convert_type(word << 16, f32)
                            hi = lax.bitcast_convert_type(word & jnp.int32(-65536), f32)
                            accs[2 * wc] = accs[2 * wc] + wj * lo
                            accs[2 * wc + 1] = accs[2 * wc + 1] + wj * hi
                    for wc in range(SC_DIM_BLOCK // lanes):
                        out_v[pl.ds(db * SC_DIM_BLOCK + wc * lanes, lanes)] = accs[2 * wc]
                        out_v[pl.ds(ROW_WORDS_FLAT + db * SC_DIM_BLOCK + wc * lanes, lanes)] = accs[2 * wc + 1]
                    return carry
                lax.fori_loop(0, SC_HALF // SC_ROW_GROUP, group_body, 0)

        def token(i, carry):
            t = base + i
            pltpu.sync_copy(ids_hbm.at[tok_off + t, pl.ds(0, SC_HALF)], idx_a)
            pltpu.sync_copy(ids_hbm.at[tok_off + t, pl.ds(SC_HALF, SC_HALF)], idx_b)
            copy_a = pltpu.async_copy(tbl_hbm.at[idx_a], rows_a, sem_a)
            copy_b = pltpu.async_copy(tbl_hbm.at[idx_b], rows_b, sem_b)
            pltpu.sync_copy(w_hbm.at[t], w_v)
            for q in range(D_MODEL // lanes):
                out_v[pl.ds(q * lanes, lanes)] = jnp.zeros((lanes,), f32)
            copy_a.wait()
            accumulate(rows_a, 0)
            copy_b.wait()
            accumulate(rows_b, SC_HALF)
            pltpu.sync_copy(out_v, out_hbm.at[t])
            return carry

        lax.fori_loop(0, tpw, token, 0)

    return run(ids, w, tbl_words)


def _final_kernel(x1_ref, p_ref, g_ref, y_ref):
    x = x1_ref[...] + p_ref[...]
    ms = jnp.mean(x * x, axis=-1, keepdims=True)
    y_ref[...] = x * lax.rsqrt(ms + EPS) * g_ref[...]


def _final(x1, peer_out, final_g):
    n = x1.shape[0]
    t = 512
    row = pl.BlockSpec((t, D_MODEL), lambda i: (i, 0))
    return pl.pallas_call(
        _final_kernel,
        grid=(n // t,),
        in_specs=[row, row, _const_spec((1, D_MODEL))],
        out_specs=row,
        out_shape=jax.ShapeDtypeStruct((n, D_MODEL), f32),
        compiler_params=_cparams(("parallel",)),
        name="final_norm",
    )(x1, peer_out, final_g)


def _rotary_tables(seq):
    half = ROT_DIM // 2
    inv = ROPE_THETA ** (-jnp.arange(half, dtype=f32) * 2.0 / ROT_DIM)
    ang = jnp.arange(seq, dtype=jnp.int32).astype(f32)[:, None] * inv[None, :]
    cos = jnp.cos(ang)
    sin = jnp.sin(ang)
    pad = HEAD_DIM - ROT_DIM
    one = jnp.ones((seq, pad), f32)
    zero = jnp.zeros((seq, pad), f32)
    zh = jnp.zeros((seq, half), f32)
    cos_h = jnp.concatenate([cos, cos, one], axis=1)
    sa_h = jnp.concatenate([-sin, zh, zero], axis=1)
    sb_h = jnp.concatenate([zh, sin, zero], axis=1)
    rep = LANES // HEAD_DIM
    return tuple(jnp.tile(t, (1, rep)) for t in (cos_h, sa_h, sb_h))


def _layer(x, params, tables):
    (norm1_g, w_in_bf, b_gate, merge_w, keys_bf, u_tbl, v_tbl, final_g) = params
    batch, seq, _ = x.shape
    n = batch * seq
    x2 = x.reshape(n, D_MODEL)
    cos_t, sa_t, sb_t = tables
    q, k, v, u, gates = _inproj(x2, seq, norm1_g, w_in_bf, b_gate, cos_t, sa_t, sb_t)
    os_, ls_ = [], []
    for g in range(N_GROUPS):
        o, lse = _attention_group(q, k, v, batch, seq, g)
        os_.append(o)
        ls_.append(lse)
    x1, h2, qp = _merge(x2, seq, u, gates, os_, ls_, merge_w)
    ids, gate = _route(qp, keys_bf)
    ids_flat = ids.reshape(n * PEER_SLOTS)
    h2r = h2.reshape(n * SUBLANES, LANES)
    gate3 = gate.reshape(n, 1, PEER_SLOTS)
    n_chunk = n // PEER_TOKEN_CHUNKS
    pv = []
    for ch in range(PEER_TOKEN_CHUNKS):
        w3 = _peer_u(ids_flat, h2r, gate3, u_tbl, ch * n_chunk, n_chunk)
        pv.append(_peer_v_sc(ids, w3.reshape(n_chunk, PEER_SLOTS), v_tbl, ch * n_chunk))
    y = _final(x1, jnp.concatenate(pv, axis=0), final_g)
    return y.reshape(batch, seq, D_MODEL)


def kernel(x_prompt, x_sample, norm1_g, w_in, b_gate, w_attn_up, conv_dw_w, conv_dw_b, conv_ln_g, conv_ln_b,
           conv_pw_w, conv_pw_b, w_out, norm2_g, peer_wq, peer_keys, peer_u, peer_v, final_g):
    depth = w_in.shape[0]
    row = lambda a: a.reshape(1, -1)
    hp, hs = x_prompt, x_sample
    tables = _rotary_tables(max(x_prompt.shape[1], x_sample.shape[1]))
    for l in range(depth):
        merge_w = (conv_dw_w[l], row(conv_dw_b[l]), row(conv_ln_g[l]), row(conv_ln_b[l]),
                   conv_pw_w[l].astype(bf16), row(conv_pw_b[l]), w_attn_up[l].astype(bf16),
                   w_out[l].astype(bf16), row(norm2_g[l]), peer_wq[l].astype(bf16))
        keys_bf = peer_keys[l].astype(bf16).reshape(2 * PEER_HEADS, N_SUBKEYS, -1)
        last = l == depth - 1
        params = (row(norm1_g[l]), w_in[l].astype(bf16), row(b_gate[l]), merge_w, keys_bf,
                  _pack_table(peer_u[l]), _pack_table(peer_v[l]).reshape(N_EXPERTS, ROW_WORDS_FLAT),
                  row(final_g) if last else None)
        hp = _layer(hp, params, tables)
        hs = _layer(hs, params, tables)
    return (hp, hs)
```

```python
import dataclasses
import functools
import math

import numpy as np
import jax
import jax.numpy as jnp
from jax import lax
from jax.experimental import pallas as pl
from jax.experimental.pallas import tpu as pltpu
from jax.experimental.pallas import tpu_sc as plsc

f32 = jnp.float32
bf16 = jnp.bfloat16
i32 = jnp.int32

D_MODEL = 1024
HEAD_DIM = 64
N_GROUPS = 3
HEADS_PER_GROUP = 4
GROUP_W = HEADS_PER_GROUP * HEAD_DIM
ATTN_W = N_GROUPS * GROUP_W
WINDOWS = (128, 512, 2048)
DILATIONS = (1, 4, 16)
HALF_WIN = 64
ROT_DIM = HEAD_DIM // 4
ROPE_THETA = 500000.0
NEG_INF = -1e30
CONV_W = 512
CONV_K = 31
CONV_HALO = 16
PEER_HEADS = 8
N_SUBKEYS = 128
N_EXPERTS = N_SUBKEYS * N_SUBKEYS
PEER_TOPK = 16
PEER_SLOTS = PEER_HEADS * PEER_TOPK
EPS = 1e-6

LANES = 128
SUBLANES = 8
ROW_WORDS = D_MODEL // 2 // LANES
VMEM_LIMIT = 56 * 1024 * 1024

T_PROJ = 256
T_PEER = 64
GATHER_CHUNK = 32
N_CHUNKS = (8 * 16) // GATHER_CHUNK
PEER_TOKEN_CHUNKS = 8

SC_CORES = 2
SC_SUBCORES = 16
SC_LANES = 16
SC_WORKERS = SC_CORES * SC_SUBCORES
ROW_WORDS_FLAT = D_MODEL // 2
SC_HALF = PEER_SLOTS // 2
SC_DIM_BLOCK = 128
SC_ROW_GROUP = 8
Q_SUB = 128


def _cparams(sem):
    return pltpu.CompilerParams(dimension_semantics=sem, vmem_limit_bytes=VMEM_LIMIT)


def _const_spec(shape):
    nd = len(shape)
    return pl.BlockSpec(shape, lambda *_: (0,) * nd, pipeline_mode=pl.Buffered(1))


def _inproj_kernel(x_ref, g_ref, w_ref, bg_ref, cos_ref, sa_ref, sb_ref, *refs):
    qkv_refs = (refs[0:N_GROUPS], refs[N_GROUPS:2 * N_GROUPS], refs[2 * N_GROUPS:3 * N_GROUPS])
    u_ref, gate_ref, stage_ref = refs[3 * N_GROUPS:]
    x = x_ref[...]
    ms = jnp.mean(x * x, axis=-1, keepdims=True)
    h = (x * lax.rsqrt(ms + EPS) * g_ref[...]).astype(bf16)

    def proj(lo, hi):
        return jnp.dot(h, w_ref[:, lo:hi], preferred_element_type=f32)

    cos = cos_ref[...]
    sa = sa_ref[...]
    sb = sb_ref[...]

    def rotary(tc, scale):
        r = tc * cos + pltpu.roll(tc, LANES - ROT_DIM // 2, 1) * sa + pltpu.roll(tc, ROT_DIM // 2, 1) * sb
        return r * scale

    def emit(t, out_refs, fn):
        for c in range(ATTN_W // LANES):
            g, half = divmod(c, GROUP_W // LANES)
            d = DILATIONS[g]
            chunk = fn(t[:, c * LANES:(c + 1) * LANES])
            if d == 1:
                out_refs[g][:, half * LANES:(half + 1) * LANES] = chunk.astype(bf16)
                continue
            stage_ref[...] = chunk
            for r in range(d):
                col = r * GROUP_W + half * LANES
                out_refs[g][:, col:col + LANES] = stage_ref[pl.ds(r, T_PROJ // d, stride=d), :].astype(bf16)

    emit(proj(0, ATTN_W), qkv_refs[0], lambda tc: rotary(tc, HEAD_DIM ** -0.5))
    emit(proj(ATTN_W, 2 * ATTN_W), qkv_refs[1], lambda tc: rotary(tc, 1.0))
    emit(proj(2 * ATTN_W, 3 * ATTN_W), qkv_refs[2], lambda tc: tc)
    c0 = 3 * ATTN_W
    a = proj(c0, c0 + CONV_W)
    b = proj(c0 + CONV_W, c0 + 2 * CONV_W)
    u_ref[...] = a * jax.nn.sigmoid(b)
    gates = proj(c0 + 2 * CONV_W, c0 + 2 * CONV_W + 2 * D_MODEL) + bg_ref[...]
    gate_ref[...] = jax.nn.sigmoid(gates).astype(bf16)


def _inproj(x2, seq, norm1_g, w_in_bf, b_gate, cos_t, sa_t, sb_t):
    n = x2.shape[0]
    nsb = seq // T_PROJ
    in_cols = w_in_bf.shape[1]
    row = lambda w: pl.BlockSpec((T_PROJ, w), lambda i: (i, 0))
    pos = pl.BlockSpec((T_PROJ, LANES), lambda i: (i % nsb, 0))
    grp_specs = [pl.BlockSpec((T_PROJ // d, d * GROUP_W), lambda i: (i, 0)) for d in DILATIONS]
    grp_shapes = [jax.ShapeDtypeStruct((n // d, d * GROUP_W), bf16) for d in DILATIONS]
    outs = pl.pallas_call(
        _inproj_kernel,
        grid=(n // T_PROJ,),
        in_specs=[row(D_MODEL), _const_spec((1, D_MODEL)), _const_spec((D_MODEL, in_cols)),
                  _const_spec((1, 2 * D_MODEL)), pos, pos, pos],
        out_specs=grp_specs * 3 + [row(CONV_W), row(2 * D_MODEL)],
        out_shape=grp_shapes * 3
        + [jax.ShapeDtypeStruct((n, CONV_W), f32), jax.ShapeDtypeStruct((n, 2 * D_MODEL), bf16)],
        scratch_shapes=[pltpu.VMEM((T_PROJ, LANES), f32)],
        compiler_params=_cparams(("parallel",)),
        name="inproj",
    )(x2, norm1_g, w_in_bf, b_gate, cos_t, sa_t, sb_t)
    q, k, v = outs[0:N_GROUPS], outs[N_GROUPS:2 * N_GROUPS], outs[2 * N_GROUPS:3 * N_GROUPS]
    return q, k, v, outs[3 * N_GROUPS], outs[3 * N_GROUPS + 1]


def _attn_kernel(q_ref, kp_ref, kc_ref, kn_ref, vp_ref, vc_ref, vn_ref, o_ref, lse_ref,
                 kw_ref, vw_ref, *, tq, n_rows):
    i = pl.program_id(2)
    kw_ref[0:HALF_WIN] = kp_ref[0]
    kw_ref[HALF_WIN:HALF_WIN + tq] = kc_ref[0]
    kw_ref[HALF_WIN + tq:2 * HALF_WIN + tq] = kn_ref[0]
    vw_ref[0:HALF_WIN] = vp_ref[0]
    vw_ref[HALF_WIN:HALF_WIN + tq] = vc_ref[0]
    vw_ref[HALF_WIN + tq:2 * HALF_WIN + tq] = vn_ref[0]

    win = Q_SUB + 2 * HALF_WIN
    qi = lax.broadcasted_iota(i32, (Q_SUB, win), 0)
    kj = lax.broadcasted_iota(i32, (Q_SUB, win), 1)
    band = (kj - qi >= 0) & (kj - qi <= 2 * HALF_WIN)
    head_of_lane = lax.broadcasted_iota(i32, (1, GROUP_W), 1) // HEAD_DIM

    for s in range(tq // Q_SUB):
        qs = q_ref[0, s * Q_SUB:(s + 1) * Q_SUB, :]
        kwin = kw_ref[s * Q_SUB:s * Q_SUB + win, :]
        vwin = vw_ref[s * Q_SUB:s * Q_SUB + win, :]
        key_row = i * tq + (s * Q_SUB - HALF_WIN) + kj
        ok = band & (key_row >= 0) & (key_row < n_rows)
        o_acc = jnp.zeros((Q_SUB, GROUP_W), f32)
        l_acc = jnp.zeros((Q_SUB, GROUP_W), f32)
        for h in range(HEADS_PER_GROUP):
            hm = head_of_lane == h
            qh = jnp.where(hm, qs, jnp.zeros_like(qs))
            sc = lax.dot_general(qh, kwin, (((1,), (1,)), ((), ())), preferred_element_type=f32)
            sc = jnp.where(ok, sc, NEG_INF)
            m = jnp.max(sc, axis=-1, keepdims=True)
            p = jnp.exp(sc - m)
            den = jnp.sum(p, axis=-1, keepdims=True)
            pv = jnp.dot(p.astype(bf16), vwin, preferred_element_type=f32)
            o_acc = jnp.where(hm, pv / den, o_acc)
            l_acc = jnp.where(hm, m + jnp.log(den), l_acc)
        o_ref[0, s * Q_SUB:(s + 1) * Q_SUB, :] = o_acc
        lse_ref[0, s * Q_SUB:(s + 1) * Q_SUB, :] = l_acc


def _attention_group(q, k, v, batch, seq, g):
    d = DILATIONS[g]
    n_rows = seq // d
    tq = min(512, n_rows)
    nblk = n_rows // tq
    hb = tq // HALF_WIN
    n_halo_blocks = n_rows // HALF_WIN
    view = lambda t: t.reshape(batch, n_rows, d * GROUP_W)
    cur = pl.BlockSpec((1, tq, GROUP_W), lambda b, r, i: (b, i, r))
    prev = pl.BlockSpec((1, HALF_WIN, GROUP_W), lambda b, r, i: (b, jnp.maximum(i * hb - 1, 0), r))
    nxt = pl.BlockSpec((1, HALF_WIN, GROUP_W),
                       lambda b, r, i: (b, jnp.minimum((i + 1) * hb, n_halo_blocks - 1), r))
    o, lse = pl.pallas_call(
        functools.partial(_attn_kernel, tq=tq, n_rows=n_rows),
        grid=(batch, d, nblk),
        in_specs=[cur, prev, cur, nxt, prev, cur, nxt],
        out_specs=[cur, cur],
        out_shape=[jax.ShapeDtypeStruct((batch, n_rows, d * GROUP_W), f32)] * 2,
        scratch_shapes=[pltpu.VMEM((tq + 2 * HALF_WIN, GROUP_W), bf16)] * 2,
        compiler_params=_cparams(("parallel", "parallel", "parallel")),
        name=f"attn_g{g}",
    )(view(q), view(k), view(k), view(k), view(v), view(v), view(v))
    return o.reshape(batch * seq, GROUP_W), lse.reshape(batch * seq, GROUP_W)


def _merge_kernel(x_ref, up_ref, uc_ref, un_ref, gate_ref,
                  o0_ref, o1_ref, o2_ref, l0_ref, l1_ref, l2_ref,
                  dww_ref, dwb_ref, lng_ref, lnb_ref, pww_ref, pwb_ref, wup_ref, wout_ref, n2g_ref, wq_ref,
                  x1_ref, h2_ref, qp_ref, ue_ref, cv_ref, *, nsb):
    i = pl.program_id(0)
    t = T_PROJ
    first = (i % nsb) == 0
    last = (i % nsb) == nsb - 1
    ue_ref[0:CONV_HALO] = jnp.where(first, 0.0, up_ref[...])
    ue_ref[CONV_HALO:CONV_HALO + t] = uc_ref[...]
    ue_ref[CONV_HALO + t:2 * CONV_HALO + t] = jnp.where(last, 0.0, un_ref[...])

    rc = 32
    off = CONV_HALO - CONV_K // 2
    for r0 in range(0, t, rc):
        acc = jnp.zeros((rc, CONV_W), f32)
        for j in range(CONV_K):
            acc = acc + ue_ref[r0 + off + j:r0 + off + j + rc, :] * dww_ref[j:j + 1, :]
        cv_ref[r0:r0 + rc, :] = acc
    c = cv_ref[...] + dwb_ref[...]
    mu = jnp.mean(c, axis=-1, keepdims=True)
    cc = c - mu
    var = jnp.mean(cc * cc, axis=-1, keepdims=True)
    un = cc * lax.rsqrt(var + EPS) * lng_ref[...] + lnb_ref[...]
    sw = un * jax.nn.sigmoid(un)
    conv = jnp.dot(sw.astype(bf16), pww_ref[...], preferred_element_type=f32) + pwb_ref[...]

    l0 = l0_ref[...]
    l1 = l1_ref[...]
    l2 = l2_ref[...]
    lm = jnp.maximum(jnp.maximum(l0, l1), l2)
    e0 = jnp.exp(l0 - lm)
    e1 = jnp.exp(l1 - lm)
    e2 = jnp.exp(l2 - lm)
    comb = (e0 * o0_ref[...] + e1 * o1_ref[...] + e2 * o2_ref[...]) / (e0 + e1 + e2)
    attn = jnp.dot(comb.astype(bf16), wup_ref[...], preferred_element_type=f32)

    g_attn = gate_ref[:, 0:D_MODEL].astype(f32)
    g_conv = gate_ref[:, D_MODEL:2 * D_MODEL].astype(f32)
    mixed = g_attn * attn + g_conv * conv
    x1 = x_ref[...] + jnp.dot(mixed.astype(bf16), wout_ref[...], preferred_element_type=f32)
    x1_ref[...] = x1
    ms = jnp.mean(x1 * x1, axis=-1, keepdims=True)
    h2 = x1 * lax.rsqrt(ms + EPS) * n2g_ref[...]
    for c in range(D_MODEL // LANES):
        h2_ref[pl.ds(c, t, stride=SUBLANES), :] = h2[:, c * LANES:(c + 1) * LANES]
    qp_ref[...] = jnp.dot(h2.astype(bf16), wq_ref[...], preferred_element_type=f32).astype(bf16)


def _merge(x2, seq, u, gates, os_, ls_, weights):
    n = x2.shape[0]
    t = T_PROJ
    nsb = seq // t
    hb = t // CONV_HALO
    nhalo = n // CONV_HALO
    row = lambda w: pl.BlockSpec((t, w), lambda i: (i, 0))
    prev = pl.BlockSpec((CONV_HALO, CONV_W), lambda i: (jnp.maximum(i * hb - 1, 0), 0))
    nxt = pl.BlockSpec((CONV_HALO, CONV_W), lambda i: (jnp.minimum((i + 1) * hb, nhalo - 1), 0))
    wspecs = [_const_spec(w.shape) for w in weights]
    qw = weights[-1].shape[1]
    return pl.pallas_call(
        functools.partial(_merge_kernel, nsb=nsb),
        grid=(n // t,),
        in_specs=[row(D_MODEL), prev, row(CONV_W), nxt, row(2 * D_MODEL)] + [row(GROUP_W)] * 6 + wspecs,
        out_specs=[row(D_MODEL), pl.BlockSpec((t * SUBLANES, LANES), lambda i: (i, 0)), row(qw)],
        out_shape=[jax.ShapeDtypeStruct((n, D_MODEL), f32), jax.ShapeDtypeStruct((n * SUBLANES, LANES), f32),
                   jax.ShapeDtypeStruct((n, qw), bf16)],
        scratch_shapes=[pltpu.VMEM((t + 2 * CONV_HALO, CONV_W), f32), pltpu.VMEM((t, CONV_W), f32)],
        compiler_params=_cparams(("parallel",)),
        name="merge",
    )(x2, u, u, u, gates, *os_, *ls_, *weights)


def _candidate_slabs():
    slabs = [("row", 0, 0, 16), ("row", 1, 0, 8)]
    for j in range(PEER_TOPK):
        hi = PEER_TOPK // (j + 1)
        if hi > 2:
            slabs.append(("col", j, 2, hi))
    return slabs


def _route_kernel(qp_ref, keys_ref, ids_ref, gate_ref, val_ref, idx_ref, best_ref, idt_ref, gt_ref):
    t = T_PROJ
    k_iota = lax.broadcasted_iota(i32, (N_SUBKEYS, t), 0)
    for hc in range(2 * PEER_HEADS):
        q = qp_ref[:, hc * N_SUBKEYS:(hc + 1) * N_SUBKEYS]
        s = lax.dot_general(keys_ref[hc], q, (((1,), (1,)), ((), ())), preferred_element_type=f32)
        for r in range(PEER_TOPK):
            m = jnp.max(s, axis=0, keepdims=True)
            am = jnp.min(jnp.where(s == m, k_iota, N_SUBKEYS), axis=0, keepdims=True)
            s = jnp.where(k_iota == am, -jnp.inf, s)
            val_ref[hc, pl.ds(r, 1), :] = m
            idx_ref[hc, pl.ds(r, 1), :] = am

    r_iota = lax.broadcasted_iota(i32, (PEER_TOPK, t), 0)
    slabs = _candidate_slabs()
    for h in range(PEER_HEADS):
        v0 = val_ref[2 * h]
        v1 = val_ref[2 * h + 1]
        i0 = idx_ref[2 * h]
        i1 = idx_ref[2 * h + 1]
        cands, flats, eids = [], [], []
        for kind, fixed, lo, hi in slabs:
            if kind == "row":
                c = v0[fixed:fixed + 1, :] + v1
                fl = fixed * PEER_TOPK + r_iota
                ei = i0[fixed:fixed + 1, :] * N_SUBKEYS + i1
            else:
                c = v0 + v1[fixed:fixed + 1, :]
                fl = r_iota * PEER_TOPK + fixed
                ei = i0 * N_SUBKEYS + i1[fixed:fixed + 1, :]
            valid = (r_iota >= lo) & (r_iota < hi)
            cands.append(jnp.where(valid, c, -jnp.inf))
            flats.append(fl)
            eids.append(ei)
        big = PEER_TOPK * PEER_TOPK
        for r in range(PEER_TOPK):
            m = functools.reduce(jnp.maximum, cands)
            m = jnp.max(m, axis=0, keepdims=True)
            fsel = functools.reduce(jnp.minimum, [jnp.where(c == m, fl, big) for c, fl in zip(cands, flats)])
            fsel = jnp.min(fsel, axis=0, keepdims=True)
            hit = [fl == fsel for fl in flats]
            eid = functools.reduce(jnp.maximum, [jnp.where(hh, ei, -1) for hh, ei in zip(hit, eids)])
            eid = jnp.max(eid, axis=0, keepdims=True)
            cands = [jnp.where(hh, -jnp.inf, c) for hh, c in zip(hit, cands)]
            best_ref[pl.ds(r, 1), :] = m
            idt_ref[pl.ds(h * PEER_TOPK + r, 1), :] = eid
        b = best_ref[...]
        e = jnp.exp(b - jnp.max(b, axis=0, keepdims=True))
        gt_ref[h * PEER_TOPK:(h + 1) * PEER_TOPK, :] = e / jnp.sum(e, axis=0, keepdims=True)
    ids_ref[...] = idt_ref[...].T
    gate_ref[...] = gt_ref[...].T


def _route(qp, keys_bf):
    n, qw = qp.shape
    t = T_PROJ
    return pl.pallas_call(
        _route_kernel,
        grid=(n // t,),
        in_specs=[pl.BlockSpec((t, qw), lambda i: (i, 0)), _const_spec(keys_bf.shape)],
        out_specs=[pl.BlockSpec((t, PEER_SLOTS), lambda i: (i, 0))] * 2,
        out_shape=[jax.ShapeDtypeStruct((n, PEER_SLOTS), i32), jax.ShapeDtypeStruct((n, PEER_SLOTS), f32)],
        scratch_shapes=[pltpu.VMEM((2 * PEER_HEADS, PEER_TOPK, t), f32), pltpu.VMEM((2 * PEER_HEADS, PEER_TOPK, t), i32),
                        pltpu.VMEM((PEER_TOPK, t), f32), pltpu.VMEM((PEER_SLOTS, t), i32),
                        pltpu.VMEM((PEER_SLOTS, t), f32)],
        compiler_params=_cparams(("parallel",)),
        name="route",
    )(qp, keys_bf)


def _pack_table(tbl):
    bits = lax.bitcast_convert_type(tbl.astype(bf16), jnp.uint16).astype(jnp.uint32)
    half = D_MODEL // 2
    words = bits[:, :half] | (bits[:, half:] << 16)
    return lax.bitcast_convert_type(words, i32).reshape(tbl.shape[0], ROW_WORDS, LANES)


def _unpack(words):
    lo = pltpu.bitcast(words << 16, f32)
    hi = pltpu.bitcast(words & jnp.int32(-65536), f32)
    return lo, hi


def _split2(x):
    hi = x.astype(bf16)
    lo = (x - hi.astype(f32)).astype(bf16)
    return hi, lo


def _gather_chunk(ids_ref, tbl_ref, tile_ref, c, base):
    e = None
    for k in range(GATHER_CHUNK):
        e = ids_ref[base + k]
        tile_ref[c, k * ROW_WORDS:(k + 1) * ROW_WORDS, :] = tbl_ref[e]
    return e


def _two_token_pipeline(ids_ref, tbl_ref, tile_a, tile_b, consume, init):
    tile_b[...] = jnp.zeros(tile_b.shape, i32)
    last = T_PEER - 1

    def half_step(t_gather, gather_tile, t_consume, consume_tile, dep):
        base = jnp.minimum(t_gather, last) * PEER_SLOTS
        tc = jnp.clip(t_consume, 0, last)
        acc = init
        for c in range(N_CHUNKS):
            dep = _gather_chunk(ids_ref, tbl_ref, gather_tile, c, base + c * GATHER_CHUNK + (dep >> 31))
            acc = consume(consume_tile, c, tc, acc)
        return dep

    def pair_step(i, dep):
        dep = half_step(2 * i, tile_a, 2 * i - 1, tile_b, dep)
        dep = half_step(2 * i + 1, tile_b, 2 * i, tile_a, dep)
        return dep

    lax.fori_loop(0, T_PEER // 2 + 1, pair_step, jnp.int32(0))


def _tile_scratch():
    return pltpu.VMEM((N_CHUNKS, GATHER_CHUNK * ROW_WORDS, LANES), i32)


def _peer_u_kernel(ids_ref, xr_ref, g_ref, tbl_ref, w_ref, tile_a, tile_b):
    crow = GATHER_CHUNK * ROW_WORDS
    half = crow // 2
    kk = lax.broadcasted_iota(i32, (2 * LANES, LANES), 0)
    nn = lax.broadcasted_iota(i32, (2 * LANES, LANES), 1)
    summer = ((kk < LANES) == (nn % GATHER_CHUNK < GATHER_CHUNK // 2)).astype(bf16)
    rr = lax.broadcasted_iota(i32, (half, LANES), 0)
    ll = lax.broadcasted_iota(i32, (half, LANES), 1)
    own_row = ll % (GATHER_CHUNK // 2) == rr // ROW_WORDS
    lane_chunk = ll // GATHER_CHUNK

    def consume(tile, c, t, acc):
        xt = xr_ref[pl.ds(pl.multiple_of(t * SUBLANES, SUBLANES), SUBLANES), :]
        xlo = xt[0:ROW_WORDS]
        xhi = xt[ROW_WORDS:2 * ROW_WORDS]
        x2lo = pltpu.repeat(jnp.concatenate([xlo, xlo], axis=0), crow // SUBLANES, axis=0)
        x2hi = pltpu.repeat(jnp.concatenate([xhi, xhi], axis=0), crow // SUBLANES, axis=0)
        lo, hi = _unpack(tile[c])
        p = lo * x2lo + hi * x2hi
        p_hi, p_lo = _split2(jnp.concatenate([p[0:half], p[half:crow]], axis=1))
        rs = jnp.dot(p_hi, summer, preferred_element_type=f32) + jnp.dot(p_lo, summer, preferred_element_type=f32)
        acc = acc + jnp.sum(jnp.where(own_row & (lane_chunk == c), rs, 0.0), axis=0, keepdims=True)
        if c == N_CHUNKS - 1:
            gelu = 0.5 * acc * (1.0 + lax.erf(acc * (2.0 ** -0.5)))
            w_ref[t] = g_ref[t] * gelu
        return acc

    _two_token_pipeline(ids_ref, tbl_ref, tile_a, tile_b, consume, jnp.zeros((1, PEER_SLOTS), f32))


def _peer_u(ids_flat, h2r, gate3, tbl, tok_off, n_tok):
    t = T_PEER
    b0 = tok_off // t
    return pl.pallas_call(
        _peer_u_kernel,
        grid=(n_tok // t,),
        in_specs=[pl.BlockSpec((t * PEER_SLOTS,), lambda i: (i + b0,), memory_space=pltpu.SMEM),
                  pl.BlockSpec((t * SUBLANES, LANES), lambda i: (i + b0, 0)),
                  pl.BlockSpec((t, 1, PEER_SLOTS), lambda i: (i + b0, 0, 0)),
                  _const_spec(tbl.shape)],
        out_specs=pl.BlockSpec((t, 1, PEER_SLOTS), lambda i: (i, 0, 0)),
        out_shape=jax.ShapeDtypeStruct((n_tok, 1, PEER_SLOTS), f32),
        scratch_shapes=[_tile_scratch(), _tile_scratch()],
        compiler_params=_cparams(("parallel",)),
        name="peer_u",
    )(ids_flat, h2r, gate3, tbl)


def _peer_v_sc(ids, w, tbl_words, tok_off):
    n_tok = w.shape[0]
    tpw = n_tok // SC_WORKERS
    lanes = SC_LANES
    mesh = plsc.VectorSubcoreMesh(core_axis_name="c", subcore_axis_name="s")

    @functools.partial(
        pl.kernel, mesh=mesh,
        out_type=jax.ShapeDtypeStruct((n_tok, D_MODEL), f32),
        scratch_types=[
            pltpu.VMEM((SC_HALF,), i32), pltpu.VMEM((SC_HALF,), i32),
            pltpu.VMEM((SC_HALF, ROW_WORDS_FLAT), i32), pltpu.VMEM((SC_HALF, ROW_WORDS_FLAT), i32),
            pltpu.VMEM((PEER_SLOTS,), f32),
            pltpu.VMEM((D_MODEL,), f32),
            pltpu.SemaphoreType.DMA, pltpu.SemaphoreType.DMA,
        ],
        compiler_params=dataclasses.replace(pltpu.CompilerParams(), needs_layout_passes=False),
        name="peer_v_sc",
    )
    def run(ids_hbm, w_hbm, tbl_hbm, out_hbm, idx_a, idx_b, rows_a, rows_b, w_v, out_v, sem_a, sem_b):
        wid = lax.axis_index("s") * SC_CORES + lax.axis_index("c")
        base = wid * tpw

        def accumulate(rows, w_off):
            for db in range(ROW_WORDS_FLAT // SC_DIM_BLOCK):
                def group_body(g, carry):
                    accs = []
                    for wc in range(SC_DIM_BLOCK // lanes):
                        accs.append(out_v[pl.ds(db * SC_DIM_BLOCK + wc * lanes, lanes)])
                        accs.append(out_v[pl.ds(ROW_WORDS_FLAT + db * SC_DIM_BLOCK + wc * lanes, lanes)])
                    j0 = g * SC_ROW_GROUP
                    wchunk = w_v[pl.ds(pl.multiple_of((w_off + j0) // lanes * lanes, lanes), lanes)]
                    sub = (w_off + j0) % lanes
                    for r in range(SC_ROW_GROUP):
                        wj = jnp.take(wchunk, jnp.full((lanes,), sub + r, i32))
                        for wc in range(SC_DIM_BLOCK // lanes):
                            word = rows[j0 + r, pl.ds(db * SC_DIM_BLOCK + wc * lanes, lanes)]
                            lo = lax.bitcast_convert_type(word << 16, f32)
                            hi = lax.bitcast_convert_type(word & jnp.int32(-65536), f32)
                            accs[2 * wc] = accs[2 * wc] + wj * lo
                            accs[2 * wc + 1] = accs[2 * wc + 1] + wj * hi
                    for wc in range(SC_DIM_BLOCK // lanes):
                        out_v[pl.ds(db * SC_DIM_BLOCK + wc * lanes, lanes)] = accs[2 * wc]
                        out_v[pl.ds(ROW_WORDS_FLAT + db * SC_DIM_BLOCK + wc * lanes, lanes)] = accs[2 * wc + 1]
                    return carry
                lax.fori_loop(0, SC_HALF // SC_ROW_GROUP, group_body, 0)

        def token(i, carry):
            t = base + i
            pltpu.sync_copy(ids_hbm.at[tok_off + t, pl.ds(0, SC_HALF)], idx_a)
            pltpu.sync_copy(ids_hbm.at[tok_off + t, pl.ds(SC_HALF, SC_HALF)], idx_b)
            copy_a = pltpu.async_copy(tbl_hbm.at[idx_a], rows_a, sem_a)
            copy_b = pltpu.async_copy(tbl_hbm.at[idx_b], rows_b, sem_b)
            pltpu.sync_copy(w_hbm.at[t], w_v)
            for q in range(D_MODEL // lanes):
                out_v[pl.ds(q * lanes, lanes)] = jnp.zeros((lanes,), f32)
            copy_a.wait()
            accumulate(rows_a, 0)
            copy_b.wait()
            accumulate(rows_b, SC_HALF)
            pltpu.sync_copy(out_v, out_hbm.at[t])
            return carry

        lax.fori_loop(0, tpw, token, 0)

    return run(ids, w, tbl_words)


def _final_kernel(x1_ref, p_ref, g_ref, y_ref):
    x = x1_ref[...] + p_ref[...]
    ms = jnp.mean(x * x, axis=-1, keepdims=True)
    y_ref[...] = x * lax.rsqrt(ms + EPS) * g_ref[...]


def _final(x1, peer_out, final_g):
    n = x1.shape[0]
    t = 512
    row = pl.BlockSpec((t, D_MODEL), lambda i: (i, 0))
    return pl.pallas_call(
        _final_kernel,
        grid=(n // t,),
        in_specs=[row, row, _const_spec((1, D_MODEL))],
        out_specs=row,
        out_shape=jax.ShapeDtypeStruct((n, D_MODEL), f32),
        compiler_params=_cparams(("parallel",)),
        name="final_norm",
    )(x1, peer_out, final_g)


def _rotary_tables(seq):
    half = ROT_DIM // 2
    inv = ROPE_THETA ** (-jnp.arange(half, dtype=f32) * 2.0 / ROT_DIM)
    ang = jnp.arange(seq, dtype=jnp.int32).astype(f32)[:, None] * inv[None, :]
    cos = jnp.cos(ang)
    sin = jnp.sin(ang)
    pad = HEAD_DIM - ROT_DIM
    one = jnp.ones((seq, pad), f32)
    zero = jnp.zeros((seq, pad), f32)
    zh = jnp.zeros((seq, half), f32)
    cos_h = jnp.concatenate([cos, cos, one], axis=1)
    sa_h = jnp.concatenate([-sin, zh, zero], axis=1)
    sb_h = jnp.concatenate([zh, sin, zero], axis=1)
    rep = LANES // HEAD_DIM
    return tuple(jnp.tile(t, (1, rep)) for t in (cos_h, sa_h, sb_h))


def _layer(x, params, tables):
    (norm1_g, w_in_bf, b_gate, merge_w, keys_bf, u_tbl, v_tbl, final_g) = params
    batch, seq, _ = x.shape
    n = batch * seq
    x2 = x.reshape(n, D_MODEL)
    cos_t, sa_t, sb_t = tables
    q, k, v, u, gates = _inproj(x2, seq, norm1_g, w_in_bf, b_gate, cos_t, sa_t, sb_t)
    os_, ls_ = [], []
    for g in range(N_GROUPS):
        o, lse = _attention_group(q[g], k[g], v[g], batch, seq, g)
        os_.append(o)
        ls_.append(lse)
    x1, h2r, qp = _merge(x2, seq, u, gates, os_, ls_, merge_w)
    ids, gate = _route(qp, keys_bf)
    ids_flat = ids.reshape(n * PEER_SLOTS)
    gate3 = gate.reshape(n, 1, PEER_SLOTS)
    n_chunk = n // PEER_TOKEN_CHUNKS
    pv = []
    for ch in range(PEER_TOKEN_CHUNKS):
        w3 = _peer_u(ids_flat, h2r, gate3, u_tbl, ch * n_chunk, n_chunk)
        pv.append(_peer_v_sc(ids, w3.reshape(n_chunk, PEER_SLOTS), v_tbl, ch * n_chunk))
    y = _final(x1, jnp.concatenate(pv, axis=0), final_g)
    return y.reshape(batch, seq, D_MODEL)


def kernel(x_prompt, x_sample, norm1_g, w_in, b_gate, w_attn_up, conv_dw_w, conv_dw_b, conv_ln_g, conv_ln_b,
           conv_pw_w, conv_pw_b, w_out, norm2_g, peer_wq, peer_keys, peer_u, peer_v, final_g):
    depth = w_in.shape[0]
    row = lambda a: a.reshape(1, -1)
    hp, hs = x_prompt, x_sample
    tables = _rotary_tables(max(x_prompt.shape[1], x_sample.shape[1]))
    for l in range(depth):
        merge_w = (conv_dw_w[l], row(conv_dw_b[l]), row(conv_ln_g[l]), row(conv_ln_b[l]),
                   conv_pw_w[l].astype(bf16), row(conv_pw_b[l]), w_attn_up[l].astype(bf16),
                   w_out[l].astype(bf16), row(norm2_g[l]), peer_wq[l].astype(bf16))
        keys_bf = peer_keys[l].astype(bf16).reshape(2 * PEER_HEADS, N_SUBKEYS, -1)
        last = l == depth - 1
        params = (row(norm1_g[l]), w_in[l].astype(bf16), row(b_gate[l]), merge_w, keys_bf,
                  _pack_table(peer_u[l]), _pack_table(peer_v[l]).reshape(N_EXPERTS, ROW_WORDS_FLAT),
                  row(final_g) if last else None)
        hp = _layer(hp, params, tables)
        hs = _layer(hs, params, tables)
    return (hp, hs)
```

```python
import dataclasses
import functools
import math

import numpy as np
import jax
import jax.numpy as jnp
from jax import lax
from jax.experimental import pallas as pl
from jax.experimental.pallas import tpu as pltpu
from jax.experimental.pallas import tpu_sc as plsc

f32 = jnp.float32
bf16 = jnp.bfloat16
i32 = jnp.int32

D_MODEL = 1024
HEAD_DIM = 64
N_GROUPS = 3
HEADS_PER_GROUP = 4
GROUP_W = HEADS_PER_GROUP * HEAD_DIM
ATTN_W = N_GROUPS * GROUP_W
WINDOWS = (128, 512, 2048)
DILATIONS = (1, 4, 16)
HALF_WIN = 64
ROT_DIM = HEAD_DIM // 4
ROPE_THETA = 500000.0
NEG_INF = -1e30
CONV_W = 512
CONV_K = 31
CONV_HALO = 16
PEER_HEADS = 8
N_SUBKEYS = 128
N_EXPERTS = N_SUBKEYS * N_SUBKEYS
PEER_TOPK = 16
PEER_SLOTS = PEER_HEADS * PEER_TOPK
EPS = 1e-6

LANES = 128
SUBLANES = 8
ROW_WORDS = D_MODEL // 2 // LANES
VMEM_LIMIT = 56 * 1024 * 1024

T_PROJ = 256
T_PEER = 256
PAIRS_PER_STEP = 8
GATHER_CHUNK = 32
N_CHUNKS = (8 * 16) // GATHER_CHUNK
PEER_TOKEN_CHUNKS = 8

SC_CORES = 2
SC_SUBCORES = 16
SC_LANES = 16
SC_WORKERS = SC_CORES * SC_SUBCORES
ROW_WORDS_FLAT = D_MODEL // 2
SC_HALF = PEER_SLOTS // 2
SC_DIM_BLOCK = 128
SC_ROW_GROUP = 8
Q_SUB = 128


def _cparams(sem):
    return pltpu.CompilerParams(dimension_semantics=sem, vmem_limit_bytes=VMEM_LIMIT)


def _const_spec(shape):
    nd = len(shape)
    return pl.BlockSpec(shape, lambda *_: (0,) * nd, pipeline_mode=pl.Buffered(1))


def _inproj_kernel(x_ref, g_ref, w_ref, bg_ref, cos_ref, sa_ref, sb_ref, *refs):
    qkv_refs = (refs[0:N_GROUPS], refs[N_GROUPS:2 * N_GROUPS], refs[2 * N_GROUPS:3 * N_GROUPS])
    u_ref, gate_ref, stage_ref = refs[3 * N_GROUPS:]
    x = x_ref[...]
    ms = jnp.mean(x * x, axis=-1, keepdims=True)
    h = (x * lax.rsqrt(ms + EPS) * g_ref[...]).astype(bf16)

    def proj(lo, hi):
        return jnp.dot(h, w_ref[:, lo:hi], preferred_element_type=f32)

    cos = cos_ref[...]
    sa = sa_ref[...]
    sb = sb_ref[...]

    def rotary(tc, scale):
        r = tc * cos + pltpu.roll(tc, LANES - ROT_DIM // 2, 1) * sa + pltpu.roll(tc, ROT_DIM // 2, 1) * sb
        return r * scale

    def emit(t, out_refs, fn):
        for c in range(ATTN_W // LANES):
            g, half = divmod(c, GROUP_W // LANES)
            d = DILATIONS[g]
            chunk = fn(t[:, c * LANES:(c + 1) * LANES])
            if d == 1:
                out_refs[g][:, half * LANES:(half + 1) * LANES] = chunk.astype(bf16)
                continue
            stage_ref[...] = chunk
            for r in range(d):
                col = r * GROUP_W + half * LANES
                out_refs[g][:, col:col + LANES] = stage_ref[pl.ds(r, T_PROJ // d, stride=d), :].astype(bf16)

    emit(proj(0, ATTN_W), qkv_refs[0], lambda tc: rotary(tc, HEAD_DIM ** -0.5))
    emit(proj(ATTN_W, 2 * ATTN_W), qkv_refs[1], lambda tc: rotary(tc, 1.0))
    emit(proj(2 * ATTN_W, 3 * ATTN_W), qkv_refs[2], lambda tc: tc)
    c0 = 3 * ATTN_W
    a = proj(c0, c0 + CONV_W)
    b = proj(c0 + CONV_W, c0 + 2 * CONV_W)
    u_ref[...] = a * jax.nn.sigmoid(b)
    gates = proj(c0 + 2 * CONV_W, c0 + 2 * CONV_W + 2 * D_MODEL) + bg_ref[...]
    gate_ref[...] = jax.nn.sigmoid(gates).astype(bf16)


def _inproj(x2, seq, norm1_g, w_in_bf, b_gate, cos_t, sa_t, sb_t):
    n = x2.shape[0]
    nsb = seq // T_PROJ
    in_cols = w_in_bf.shape[1]
    row = lambda w: pl.BlockSpec((T_PROJ, w), lambda i: (i, 0))
    pos = pl.BlockSpec((T_PROJ, LANES), lambda i: (i % nsb, 0))
    grp_specs = [pl.BlockSpec((T_PROJ // d, d * GROUP_W), lambda i: (i, 0)) for d in DILATIONS]
    grp_shapes = [jax.ShapeDtypeStruct((n // d, d * GROUP_W), bf16) for d in DILATIONS]
    outs = pl.pallas_call(
        _inproj_kernel,
        grid=(n // T_PROJ,),
        in_specs=[row(D_MODEL), _const_spec((1, D_MODEL)), _const_spec((D_MODEL, in_cols)),
                  _const_spec((1, 2 * D_MODEL)), pos, pos, pos],
        out_specs=grp_specs * 3 + [row(CONV_W), row(2 * D_MODEL)],
        out_shape=grp_shapes * 3
        + [jax.ShapeDtypeStruct((n, CONV_W), f32), jax.ShapeDtypeStruct((n, 2 * D_MODEL), bf16)],
        scratch_shapes=[pltpu.VMEM((T_PROJ, LANES), f32)],
        compiler_params=_cparams(("parallel",)),
        name="inproj",
    )(x2, norm1_g, w_in_bf, b_gate, cos_t, sa_t, sb_t)
    q, k, v = outs[0:N_GROUPS], outs[N_GROUPS:2 * N_GROUPS], outs[2 * N_GROUPS:3 * N_GROUPS]
    return q, k, v, outs[3 * N_GROUPS], outs[3 * N_GROUPS + 1]


def _attn_kernel(q_ref, kp_ref, kc_ref, kn_ref, vp_ref, vc_ref, vn_ref, o_ref, lse_ref,
                 kw_ref, vw_ref, *, tq, n_rows):
    i = pl.program_id(2)
    kw_ref[0:HALF_WIN] = kp_ref[0]
    kw_ref[HALF_WIN:HALF_WIN + tq] = kc_ref[0]
    kw_ref[HALF_WIN + tq:2 * HALF_WIN + tq] = kn_ref[0]
    vw_ref[0:HALF_WIN] = vp_ref[0]
    vw_ref[HALF_WIN:HALF_WIN + tq] = vc_ref[0]
    vw_ref[HALF_WIN + tq:2 * HALF_WIN + tq] = vn_ref[0]

    win = Q_SUB + 2 * HALF_WIN
    qi = lax.broadcasted_iota(i32, (Q_SUB, win), 0)
    kj = lax.broadcasted_iota(i32, (Q_SUB, win), 1)
    band = (kj - qi >= 0) & (kj - qi <= 2 * HALF_WIN)
    head_of_lane = lax.broadcasted_iota(i32, (1, GROUP_W), 1) // HEAD_DIM

    for s in range(tq // Q_SUB):
        qs = q_ref[0, s * Q_SUB:(s + 1) * Q_SUB, :]
        kwin = kw_ref[s * Q_SUB:s * Q_SUB + win, :]
        vwin = vw_ref[s * Q_SUB:s * Q_SUB + win, :]
        key_row = i * tq + (s * Q_SUB - HALF_WIN) + kj
        ok = band & (key_row >= 0) & (key_row < n_rows)
        o_acc = jnp.zeros((Q_SUB, GROUP_W), f32)
        l_acc = jnp.zeros((Q_SUB, GROUP_W), f32)
        for h in range(HEADS_PER_GROUP):
            hm = head_of_lane == h
            qh = jnp.where(hm, qs, jnp.zeros_like(qs))
            sc = lax.dot_general(qh, kwin, (((1,), (1,)), ((), ())), preferred_element_type=f32)
            sc = jnp.where(ok, sc, NEG_INF)
            m = jnp.max(sc, axis=-1, keepdims=True)
            p = jnp.exp(sc - m)
            den = jnp.sum(p, axis=-1, keepdims=True)
            pv = jnp.dot(p.astype(bf16), vwin, preferred_element_type=f32)
            o_acc = jnp.where(hm, pv / den, o_acc)
            l_acc = jnp.where(hm, m + jnp.log(den), l_acc)
        o_ref[0, s * Q_SUB:(s + 1) * Q_SUB, :] = o_acc
        lse_ref[0, s * Q_SUB:(s + 1) * Q_SUB, :] = l_acc


def _attention_group(q, k, v, batch, seq, g):
    d = DILATIONS[g]
    n_rows = seq // d
    tq = min(512, n_rows)
    nblk = n_rows // tq
    hb = tq // HALF_WIN
    n_halo_blocks = n_rows // HALF_WIN
    view = lambda t: t.reshape(batch, n_rows, d * GROUP_W)
    cur = pl.BlockSpec((1, tq, GROUP_W), lambda b, r, i: (b, i, r))
    prev = pl.BlockSpec((1, HALF_WIN, GROUP_W), lambda b, r, i: (b, jnp.maximum(i * hb - 1, 0), r))
    nxt = pl.BlockSpec((1, HALF_WIN, GROUP_W),
                       lambda b, r, i: (b, jnp.minimum((i + 1) * hb, n_halo_blocks - 1), r))
    o, lse = pl.pallas_call(
        functools.partial(_attn_kernel, tq=tq, n_rows=n_rows),
        grid=(batch, d, nblk),
        in_specs=[cur, prev, cur, nxt, prev, cur, nxt],
        out_specs=[cur, cur],
        out_shape=[jax.ShapeDtypeStruct((batch, n_rows, d * GROUP_W), f32)] * 2,
        scratch_shapes=[pltpu.VMEM((tq + 2 * HALF_WIN, GROUP_W), bf16)] * 2,
        compiler_params=_cparams(("parallel", "parallel", "parallel")),
        name=f"attn_g{g}",
    )(view(q), view(k), view(k), view(k), view(v), view(v), view(v))
    return o.reshape(batch * seq, GROUP_W), lse.reshape(batch * seq, GROUP_W)


def _merge_kernel(x_ref, up_ref, uc_ref, un_ref, gate_ref,
                  o0_ref, o1_ref, o2_ref, l0_ref, l1_ref, l2_ref,
                  dww_ref, dwb_ref, lng_ref, lnb_ref, pww_ref, pwb_ref, wup_ref, wout_ref, n2g_ref, wq_ref,
                  x1_ref, h2_ref, qp_ref, ue_ref, cv_ref, *, nsb):
    i = pl.program_id(0)
    t = T_PROJ
    first = (i % nsb) == 0
    last = (i % nsb) == nsb - 1
    ue_ref[0:CONV_HALO] = jnp.where(first, 0.0, up_ref[...])
    ue_ref[CONV_HALO:CONV_HALO + t] = uc_ref[...]
    ue_ref[CONV_HALO + t:2 * CONV_HALO + t] = jnp.where(last, 0.0, un_ref[...])

    rc = 32
    off = CONV_HALO - CONV_K // 2
    for r0 in range(0, t, rc):
        acc = jnp.zeros((rc, CONV_W), f32)
        for j in range(CONV_K):
            acc = acc + ue_ref[r0 + off + j:r0 + off + j + rc, :] * dww_ref[j:j + 1, :]
        cv_ref[r0:r0 + rc, :] = acc
    c = cv_ref[...] + dwb_ref[...]
    mu = jnp.mean(c, axis=-1, keepdims=True)
    cc = c - mu
    var = jnp.mean(cc * cc, axis=-1, keepdims=True)
    un = cc * lax.rsqrt(var + EPS) * lng_ref[...] + lnb_ref[...]
    sw = un * jax.nn.sigmoid(un)
    conv = jnp.dot(sw.astype(bf16), pww_ref[...], preferred_element_type=f32) + pwb_ref[...]

    l0 = l0_ref[...]
    l1 = l1_ref[...]
    l2 = l2_ref[...]
    lm = jnp.maximum(jnp.maximum(l0, l1), l2)
    e0 = jnp.exp(l0 - lm)
    e1 = jnp.exp(l1 - lm)
    e2 = jnp.exp(l2 - lm)
    comb = (e0 * o0_ref[...] + e1 * o1_ref[...] + e2 * o2_ref[...]) / (e0 + e1 + e2)
    attn = jnp.dot(comb.astype(bf16), wup_ref[...], preferred_element_type=f32)

    g_attn = gate_ref[:, 0:D_MODEL].astype(f32)
    g_conv = gate_ref[:, D_MODEL:2 * D_MODEL].astype(f32)
    mixed = g_attn * attn + g_conv * conv
    x1 = x_ref[...] + jnp.dot(mixed.astype(bf16), wout_ref[...], preferred_element_type=f32)
    x1_ref[...] = x1
    ms = jnp.mean(x1 * x1, axis=-1, keepdims=True)
    h2 = x1 * lax.rsqrt(ms + EPS) * n2g_ref[...]
    for c in range(D_MODEL // LANES):
        h2_ref[pl.ds(c, t, stride=SUBLANES), :] = h2[:, c * LANES:(c + 1) * LANES]
    qp_ref[...] = jnp.dot(h2.astype(bf16), wq_ref[...], preferred_element_type=f32).astype(bf16)


def _merge(x2, seq, u, gates, os_, ls_, weights):
    n = x2.shape[0]
    t = T_PROJ
    nsb = seq // t
    hb = t // CONV_HALO
    nhalo = n // CONV_HALO
    row = lambda w: pl.BlockSpec((t, w), lambda i: (i, 0))
    prev = pl.BlockSpec((CONV_HALO, CONV_W), lambda i: (jnp.maximum(i * hb - 1, 0), 0))
    nxt = pl.BlockSpec((CONV_HALO, CONV_W), lambda i: (jnp.minimum((i + 1) * hb, nhalo - 1), 0))
    wspecs = [_const_spec(w.shape) for w in weights]
    qw = weights[-1].shape[1]
    return pl.pallas_call(
        functools.partial(_merge_kernel, nsb=nsb),
        grid=(n // t,),
        in_specs=[row(D_MODEL), prev, row(CONV_W), nxt, row(2 * D_MODEL)] + [row(GROUP_W)] * 6 + wspecs,
        out_specs=[row(D_MODEL), pl.BlockSpec((t * SUBLANES, LANES), lambda i: (i, 0)), row(qw)],
        out_shape=[jax.ShapeDtypeStruct((n, D_MODEL), f32), jax.ShapeDtypeStruct((n * SUBLANES, LANES), f32),
                   jax.ShapeDtypeStruct((n, qw), bf16)],
        scratch_shapes=[pltpu.VMEM((t + 2 * CONV_HALO, CONV_W), f32), pltpu.VMEM((t, CONV_W), f32)],
        compiler_params=_cparams(("parallel",)),
        name="merge",
    )(x2, u, u, u, gates, *os_, *ls_, *weights)


def _candidate_slabs():
    slabs = [("row", 0, 0, 16), ("row", 1, 0, 8)]
    for j in range(PEER_TOPK):
        hi = PEER_TOPK // (j + 1)
        if hi > 2:
            slabs.append(("col", j, 2, hi))
    return slabs


def _route_kernel(qp_ref, keys_ref, ids_ref, gate_ref, val_ref, idx_ref, best_ref, idt_ref, gt_ref):
    t = T_PROJ
    k_iota = lax.broadcasted_iota(i32, (N_SUBKEYS, t), 0)
    for hc in range(2 * PEER_HEADS):
        q = qp_ref[:, hc * N_SUBKEYS:(hc + 1) * N_SUBKEYS]
        s = lax.dot_general(keys_ref[hc], q, (((1,), (1,)), ((), ())), preferred_element_type=f32)
        for r in range(PEER_TOPK):
            m = jnp.max(s, axis=0, keepdims=True)
            am = jnp.min(jnp.where(s == m, k_iota, N_SUBKEYS), axis=0, keepdims=True)
            s = jnp.where(k_iota == am, -jnp.inf, s)
            val_ref[hc, pl.ds(r, 1), :] = m
            idx_ref[hc, pl.ds(r, 1), :] = am

    r_iota = lax.broadcasted_iota(i32, (PEER_TOPK, t), 0)
    slabs = _candidate_slabs()
    for h in range(PEER_HEADS):
        v0 = val_ref[2 * h]
        v1 = val_ref[2 * h + 1]
        i0 = idx_ref[2 * h]
        i1 = idx_ref[2 * h + 1]
        cands, flats, eids = [], [], []
        for kind, fixed, lo, hi in slabs:
            if kind == "row":
                c = v0[fixed:fixed + 1, :] + v1
                fl = fixed * PEER_TOPK + r_iota
                ei = i0[fixed:fixed + 1, :] * N_SUBKEYS + i1
            else:
                c = v0 + v1[fixed:fixed + 1, :]
                fl = r_iota * PEER_TOPK + fixed
                ei = i0 * N_SUBKEYS + i1[fixed:fixed + 1, :]
            valid = (r_iota >= lo) & (r_iota < hi)
            cands.append(jnp.where(valid, c, -jnp.inf))
            flats.append(fl)
            eids.append(ei)
        big = PEER_TOPK * PEER_TOPK
        for r in range(PEER_TOPK):
            m = functools.reduce(jnp.maximum, cands)
            m = jnp.max(m, axis=0, keepdims=True)
            fsel = functools.reduce(jnp.minimum, [jnp.where(c == m, fl, big) for c, fl in zip(cands, flats)])
            fsel = jnp.min(fsel, axis=0, keepdims=True)
            hit = [fl == fsel for fl in flats]
            eid = functools.reduce(jnp.maximum, [jnp.where(hh, ei, -1) for hh, ei in zip(hit, eids)])
            eid = jnp.max(eid, axis=0, keepdims=True)
            cands = [jnp.where(hh, -jnp.inf, c) for hh, c in zip(hit, cands)]
            best_ref[pl.ds(r, 1), :] = m
            idt_ref[pl.ds(h * PEER_TOPK + r, 1), :] = eid
        b = best_ref[...]
        e = jnp.exp(b - jnp.max(b, axis=0, keepdims=True))
        gt_ref[h * PEER_TOPK:(h + 1) * PEER_TOPK, :] = e / jnp.sum(e, axis=0, keepdims=True)
    ids_ref[...] = idt_ref[...].T
    gate_ref[...] = gt_ref[...].T


def _route(qp, keys_bf):
    n, qw = qp.shape
    t = T_PROJ
    return pl.pallas_call(
        _route_kernel,
        grid=(n // t,),
        in_specs=[pl.BlockSpec((t, qw), lambda i: (i, 0)), _const_spec(keys_bf.shape)],
        out_specs=[pl.BlockSpec((t, PEER_SLOTS), lambda i: (i, 0))] * 2,
        out_shape=[jax.ShapeDtypeStruct((n, PEER_SLOTS), i32), jax.ShapeDtypeStruct((n, PEER_SLOTS), f32)],
        scratch_shapes=[pltpu.VMEM((2 * PEER_HEADS, PEER_TOPK, t), f32), pltpu.VMEM((2 * PEER_HEADS, PEER_TOPK, t), i32),
                        pltpu.VMEM((PEER_TOPK, t), f32), pltpu.VMEM((PEER_SLOTS, t), i32),
                        pltpu.VMEM((PEER_SLOTS, t), f32)],
        compiler_params=_cparams(("parallel",)),
        name="route",
    )(qp, keys_bf)


def _pack_table(tbl):
    bits = lax.bitcast_convert_type(tbl.astype(bf16), jnp.uint16).astype(jnp.uint32)
    half = D_MODEL // 2
    words = bits[:, :half] | (bits[:, half:] << 16)
    return lax.bitcast_convert_type(words, i32).reshape(tbl.shape[0], ROW_WORDS, LANES)


def _unpack(words):
    lo = pltpu.bitcast(words << 16, f32)
    hi = pltpu.bitcast(words & jnp.int32(-65536), f32)
    return lo, hi


def _split2(x):
    hi = x.astype(bf16)
    lo = (x - hi.astype(f32)).astype(bf16)
    return hi, lo


def _gather_chunk(ids_ref, tbl_ref, tile_ref, c, base):
    e = None
    for k in range(GATHER_CHUNK):
        e = ids_ref[base + k]
        tile_ref[c, k * ROW_WORDS:(k + 1) * ROW_WORDS, :] = tbl_ref[e]
    return e


def _two_token_pipeline(ids_ref, tbl_ref, tile_a, tile_b, consume, init):
    tile_b[...] = jnp.zeros(tile_b.shape, i32)
    last = T_PEER - 1

    def half_step(t_gather, gather_tile, t_consume, consume_tile, dep):
        base = jnp.minimum(t_gather, last) * PEER_SLOTS
        tc = jnp.clip(t_consume, 0, last)
        acc = init
        for c in range(N_CHUNKS):
            dep = _gather_chunk(ids_ref, tbl_ref, gather_tile, c, base + c * GATHER_CHUNK + (dep >> 31))
            acc = consume(consume_tile, c, tc, acc)
        return dep

    def pairs_step(i, dep):
        for p in range(PAIRS_PER_STEP):
            even = 2 * (PAIRS_PER_STEP * i + p)
            dep = half_step(even, tile_a, even - 1, tile_b, dep)
            dep = half_step(even + 1, tile_b, even, tile_a, dep)
        return dep

    lax.fori_loop(0, T_PEER // (2 * PAIRS_PER_STEP) + 1, pairs_step, jnp.int32(0))


def _tile_scratch():
    return pltpu.VMEM((N_CHUNKS, GATHER_CHUNK * ROW_WORDS, LANES), i32)


def _peer_u_kernel(ids_ref, xr_ref, g_ref, tbl_ref, w_ref, tile_a, tile_b):
    crow = GATHER_CHUNK * ROW_WORDS
    half = crow // 2
    kk = lax.broadcasted_iota(i32, (2 * LANES, LANES), 0)
    nn = lax.broadcasted_iota(i32, (2 * LANES, LANES), 1)
    summer = ((kk < LANES) == (nn % GATHER_CHUNK < GATHER_CHUNK // 2)).astype(bf16)
    rr = lax.broadcasted_iota(i32, (half, LANES), 0)
    ll = lax.broadcasted_iota(i32, (half, LANES), 1)
    own_row = ll % (GATHER_CHUNK // 2) == rr // ROW_WORDS
    lane_chunk = ll // GATHER_CHUNK

    def consume(tile, c, t, acc):
        xt = xr_ref[pl.ds(pl.multiple_of(t * SUBLANES, SUBLANES), SUBLANES), :]
        xlo = xt[0:ROW_WORDS]
        xhi = xt[ROW_WORDS:2 * ROW_WORDS]
        x2lo = pltpu.repeat(jnp.concatenate([xlo, xlo], axis=0), crow // SUBLANES, axis=0)
        x2hi = pltpu.repeat(jnp.concatenate([xhi, xhi], axis=0), crow // SUBLANES, axis=0)
        lo, hi = _unpack(tile[c])
        p = lo * x2lo + hi * x2hi
        p_hi, p_lo = _split2(jnp.concatenate([p[0:half], p[half:crow]], axis=1))
        rs = jnp.dot(p_hi, summer, preferred_element_type=f32) + jnp.dot(p_lo, summer, preferred_element_type=f32)
        acc = acc + jnp.sum(jnp.where(own_row & (lane_chunk == c), rs, 0.0), axis=0, keepdims=True)
        if c == N_CHUNKS - 1:
            gelu = 0.5 * acc * (1.0 + lax.erf(acc * (2.0 ** -0.5)))
            w_ref[t] = g_ref[t] * gelu
        return acc

    _two_token_pipeline(ids_ref, tbl_ref, tile_a, tile_b, consume, jnp.zeros((1, PEER_SLOTS), f32))


def _peer_u(ids_flat, h2r, gate3, tbl, tok_off, n_tok):
    t = T_PEER
    b0 = tok_off // t
    return pl.pallas_call(
        _peer_u_kernel,
        grid=(n_tok // t,),
        in_specs=[pl.BlockSpec((t * PEER_SLOTS,), lambda i: (i + b0,), memory_space=pltpu.SMEM),
                  pl.BlockSpec((t * SUBLANES, LANES), lambda i: (i + b0, 0)),
                  pl.BlockSpec((t, 1, PEER_SLOTS), lambda i: (i + b0, 0, 0)),
                  _const_spec(tbl.shape)],
        out_specs=pl.BlockSpec((t, 1, PEER_SLOTS), lambda i: (i, 0, 0)),
        out_shape=jax.ShapeDtypeStruct((n_tok, 1, PEER_SLOTS), f32),
        scratch_shapes=[_tile_scratch(), _tile_scratch()],
        compiler_params=_cparams(("parallel",)),
        name="peer_u",
    )(ids_flat, h2r, gate3, tbl)


def _peer_v_sc(ids, w, tbl_words, tok_off):
    n_tok = w.shape[0]
    tpw = n_tok // SC_WORKERS
    lanes = SC_LANES
    mesh = plsc.VectorSubcoreMesh(core_axis_name="c", subcore_axis_name="s")

    @functools.partial(
        pl.kernel, mesh=mesh,
        out_type=jax.ShapeDtypeStruct((n_tok, D_MODEL), f32),
        scratch_types=[
            pltpu.VMEM((SC_HALF,), i32), pltpu.VMEM((SC_HALF,), i32),
            pltpu.VMEM((SC_HALF, ROW_WORDS_FLAT), i32), pltpu.VMEM((SC_HALF, ROW_WORDS_FLAT), i32),
            pltpu.VMEM((PEER_SLOTS,), f32),
            pltpu.VMEM((D_MODEL,), f32),
            pltpu.SemaphoreType.DMA, pltpu.SemaphoreType.DMA,
        ],
        compiler_params=dataclasses.replace(pltpu.CompilerParams(), needs_layout_passes=False),
        name="peer_v_sc",
    )
    def run(ids_hbm, w_hbm, tbl_hbm, out_hbm, idx_a, idx_b, rows_a, rows_b, w_v, out_v, sem_a, sem_b):
        wid = lax.axis_index("s") * SC_CORES + lax.axis_index("c")
        base = wid * tpw

        def accumulate(rows, w_off):
            for db in range(ROW_WORDS_FLAT // SC_DIM_BLOCK):
                def group_body(g, carry):
                    accs = []
                    for wc in range(SC_DIM_BLOCK // lanes):
                        accs.append(out_v[pl.ds(db * SC_DIM_BLOCK + wc * lanes, lanes)])
                        accs.append(out_v[pl.ds(ROW_WORDS_FLAT + db * SC_DIM_BLOCK + wc * lanes, lanes)])
                    j0 = g * SC_ROW_GROUP
                    wchunk = w_v[pl.ds(pl.multiple_of((w_off + j0) // lanes * lanes, lanes), lanes)]
                    sub = (w_off + j0) % lanes
                    for r in range(SC_ROW_GROUP):
                        wj = jnp.take(wchunk, jnp.full((lanes,), sub + r, i32))
                        for wc in range(SC_DIM_BLOCK // lanes):
                            word = rows[j0 + r, pl.ds(db * SC_DIM_BLOCK + wc * lanes, lanes)]
                            lo = lax.bitcast_convert_type(word << 16, f32)
                            hi = lax.bitcast_convert_type(word & jnp.int32(-65536), f32)
                            accs[2 * wc] = accs[2 * wc] + wj * lo
                            accs[2 * wc + 1] = accs[2 * wc + 1] + wj * hi
                    for wc in range(SC_DIM_BLOCK // lanes):
                        out_v[pl.ds(db * SC_DIM_BLOCK + wc * lanes, lanes)] = accs[2 * wc]
                        out_v[pl.ds(ROW_WORDS_FLAT + db * SC_DIM_BLOCK + wc * lanes, lanes)] = accs[2 * wc + 1]
                    return carry
                lax.fori_loop(0, SC_HALF // SC_ROW_GROUP, group_body, 0)

        def token(i, carry):
            t = base + i
            pltpu.sync_copy(ids_hbm.at[tok_off + t, pl.ds(0, SC_HALF)], idx_a)
            pltpu.sync_copy(ids_hbm.at[tok_off + t, pl.ds(SC_HALF, SC_HALF)], idx_b)
            copy_a = pltpu.async_copy(tbl_hbm.at[idx_a], rows_a, sem_a)
            copy_b = pltpu.async_copy(tbl_hbm.at[idx_b], rows_b, sem_b)
            pltpu.sync_copy(w_hbm.at[t], w_v)
            for q in range(D_MODEL // lanes):
                out_v[pl.ds(q * lanes, lanes)] = jnp.zeros((lanes,), f32)
            copy_a.wait()
            accumulate(rows_a, 0)
            copy_b.wait()
            accumulate(rows_b, SC_HALF)
            pltpu.sync_copy(out_v, out_hbm.at[t])
            return carry

        lax.fori_loop(0, tpw, token, 0)

    return run(ids, w, tbl_words)


def _final_kernel(x1_ref, p_ref, g_ref, y_ref):
    x = x1_ref[...] + p_ref[...]
    ms = jnp.mean(x * x, axis=-1, keepdims=True)
    y_ref[...] = x * lax.rsqrt(ms + EPS) * g_ref[...]


def _final(x1, peer_out, final_g):
    n = x1.shape[0]
    t = 512
    row = pl.BlockSpec((t, D_MODEL), lambda i: (i, 0))
    return pl.pallas_call(
        _final_kernel,
        grid=(n // t,),
        in_specs=[row, row, _const_spec((1, D_MODEL))],
        out_specs=row,
        out_shape=jax.ShapeDtypeStruct((n, D_MODEL), f32),
        compiler_params=_cparams(("parallel",)),
        name="final_norm",
    )(x1, peer_out, final_g)


def _rotary_tables(seq):
    half = ROT_DIM // 2
    inv = ROPE_THETA ** (-jnp.arange(half, dtype=f32) * 2.0 / ROT_DIM)
    ang = jnp.arange(seq, dtype=jnp.int32).astype(f32)[:, None] * inv[None, :]
    cos = jnp.cos(ang)
    sin = jnp.sin(ang)
    pad = HEAD_DIM - ROT_DIM
    one = jnp.ones((seq, pad), f32)
    zero = jnp.zeros((seq, pad), f32)
    zh = jnp.zeros((seq, half), f32)
    cos_h = jnp.concatenate([cos, cos, one], axis=1)
    sa_h = jnp.concatenate([-sin, zh, zero], axis=1)
    sb_h = jnp.concatenate([zh, sin, zero], axis=1)
    rep = LANES // HEAD_DIM
    return tuple(jnp.tile(t, (1, rep)) for t in (cos_h, sa_h, sb_h))


def _layer(x, params, tables):
    (norm1_g, w_in_bf, b_gate, merge_w, keys_bf, u_tbl, v_tbl, final_g) = params
    batch, seq, _ = x.shape
    n = batch * seq
    x2 = x.reshape(n, D_MODEL)
    cos_t, sa_t, sb_t = tables
    q, k, v, u, gates = _inproj(x2, seq, norm1_g, w_in_bf, b_gate, cos_t, sa_t, sb_t)
    os_, ls_ = [], []
    for g in range(N_GROUPS):
        o, lse = _attention_group(q[g], k[g], v[g], batch, seq, g)
        os_.append(o)
        ls_.append(lse)
    x1, h2r, qp = _merge(x2, seq, u, gates, os_, ls_, merge_w)
    ids, gate = _route(qp, keys_bf)
    ids_flat = ids.reshape(n * PEER_SLOTS)
    gate3 = gate.reshape(n, 1, PEER_SLOTS)
    n_chunk = n // PEER_TOKEN_CHUNKS
    pv = []
    for ch in range(PEER_TOKEN_CHUNKS):
        w3 = _peer_u(ids_flat, h2r, gate3, u_tbl, ch * n_chunk, n_chunk)
        pv.append(_peer_v_sc(ids, w3.reshape(n_chunk, PEER_SLOTS), v_tbl, ch * n_chunk))
    y = _final(x1, jnp.concatenate(pv, axis=0), final_g)
    return y.reshape(batch, seq, D_MODEL)


def kernel(x_prompt, x_sample, norm1_g, w_in, b_gate, w_attn_up, conv_dw_w, conv_dw_b, conv_ln_g, conv_ln_b,
           conv_pw_w, conv_pw_b, w_out, norm2_g, peer_wq, peer_keys, peer_u, peer_v, final_g):
    depth = w_in.shape[0]
    row = lambda a: a.reshape(1, -1)
    hp, hs = x_prompt, x_sample
    tables = _rotary_tables(max(x_prompt.shape[1], x_sample.shape[1]))
    for l in range(depth):
        merge_w = (conv_dw_w[l], row(conv_dw_b[l]), row(conv_ln_g[l]), row(conv_ln_b[l]),
                   conv_pw_w[l].astype(bf16), row(conv_pw_b[l]), w_attn_up[l].astype(bf16),
                   w_out[l].astype(bf16), row(norm2_g[l]), peer_wq[l].astype(bf16))
        keys_bf = peer_keys[l].astype(bf16).reshape(2 * PEER_HEADS, N_SUBKEYS, -1)
        last = l == depth - 1
        params = (row(norm1_g[l]), w_in[l].astype(bf16), row(b_gate[l]), merge_w, keys_bf,
                  _pack_table(peer_u[l]), _pack_table(peer_v[l]).reshape(N_EXPERTS, ROW_WORDS_FLAT),
                  row(final_g) if last else None)
        hp = _layer(hp, params, tables)
        hs = _layer(hs, params, tables)
    return (hp, hs)
```

```python
import dataclasses
import functools
import math

import numpy as np
import jax
import jax.numpy as jnp
from jax import lax
from jax.experimental import pallas as pl
from jax.experimental.pallas import tpu as pltpu
from jax.experimental.pallas import tpu_sc as plsc

f32 = jnp.float32
bf16 = jnp.bfloat16
i32 = jnp.int32

D_MODEL = 1024
HEAD_DIM = 64
N_GROUPS = 3
HEADS_PER_GROUP = 4
GROUP_W = HEADS_PER_GROUP * HEAD_DIM
ATTN_W = N_GROUPS * GROUP_W
WINDOWS = (128, 512, 2048)
DILATIONS = (1, 4, 16)
HALF_WIN = 64
ROT_DIM = HEAD_DIM // 4
ROPE_THETA = 500000.0
NEG_INF = -1e30
CONV_W = 512
CONV_K = 31
CONV_HALO = 16
PEER_HEADS = 8
N_SUBKEYS = 128
N_EXPERTS = N_SUBKEYS * N_SUBKEYS
PEER_TOPK = 16
PEER_SLOTS = PEER_HEADS * PEER_TOPK
EPS = 1e-6

LANES = 128
SUBLANES = 8
ROW_WORDS = D_MODEL // 2 // LANES
VMEM_LIMIT = 56 * 1024 * 1024

T_PROJ = 256
T_PEER = 256
PAIRS_PER_STEP = 8
GATHER_CHUNK = 32
N_CHUNKS = (8 * 16) // GATHER_CHUNK
BATCH_SUBGROUPS = 2
PEER_TOKEN_CHUNKS = 4

SC_CORES = 2
SC_SUBCORES = 16
SC_LANES = 16
SC_WORKERS = SC_CORES * SC_SUBCORES
ROW_WORDS_FLAT = D_MODEL // 2
SC_HALF = PEER_SLOTS // 2
SC_DIM_BLOCK = 128
SC_ROW_GROUP = 8
Q_SUB = 128


def _cparams(sem):
    return pltpu.CompilerParams(dimension_semantics=sem, vmem_limit_bytes=VMEM_LIMIT)


def _const_spec(shape):
    nd = len(shape)
    return pl.BlockSpec(shape, lambda *_: (0,) * nd, pipeline_mode=pl.Buffered(1))


def _inproj_kernel(x_ref, g_ref, w_ref, bg_ref, cos_ref, sa_ref, sb_ref, *refs):
    qkv_refs = (refs[0:N_GROUPS], refs[N_GROUPS:2 * N_GROUPS], refs[2 * N_GROUPS:3 * N_GROUPS])
    u_ref, gate_ref, stage_ref = refs[3 * N_GROUPS:]
    x = x_ref[...]
    ms = jnp.mean(x * x, axis=-1, keepdims=True)
    h = (x * lax.rsqrt(ms + EPS) * g_ref[...]).astype(bf16)

    def proj(lo, hi):
        return jnp.dot(h, w_ref[:, lo:hi], preferred_element_type=f32)

    cos = cos_ref[...]
    sa = sa_ref[...]
    sb = sb_ref[...]

    def rotary(tc, scale):
        r = tc * cos + pltpu.roll(tc, LANES - ROT_DIM // 2, 1) * sa + pltpu.roll(tc, ROT_DIM // 2, 1) * sb
        return r * scale

    def emit(t, out_refs, fn):
        for c in range(ATTN_W // LANES):
            g, half = divmod(c, GROUP_W // LANES)
            d = DILATIONS[g]
            chunk = fn(t[:, c * LANES:(c + 1) * LANES])
            if d == 1:
                out_refs[g][:, half * LANES:(half + 1) * LANES] = chunk.astype(bf16)
                continue
            stage_ref[...] = chunk
            for r in range(d):
                col = r * GROUP_W + half * LANES
                out_refs[g][:, col:col + LANES] = stage_ref[pl.ds(r, T_PROJ // d, stride=d), :].astype(bf16)

    emit(proj(0, ATTN_W), qkv_refs[0], lambda tc: rotary(tc, HEAD_DIM ** -0.5))
    emit(proj(ATTN_W, 2 * ATTN_W), qkv_refs[1], lambda tc: rotary(tc, 1.0))
    emit(proj(2 * ATTN_W, 3 * ATTN_W), qkv_refs[2], lambda tc: tc)
    c0 = 3 * ATTN_W
    a = proj(c0, c0 + CONV_W)
    b = proj(c0 + CONV_W, c0 + 2 * CONV_W)
    u_ref[...] = a * jax.nn.sigmoid(b)
    gates = proj(c0 + 2 * CONV_W, c0 + 2 * CONV_W + 2 * D_MODEL) + bg_ref[...]
    gate_ref[...] = jax.nn.sigmoid(gates).astype(bf16)


def _inproj(x2, seq, norm1_g, w_in_bf, b_gate, cos_t, sa_t, sb_t):
    n = x2.shape[0]
    nsb = seq // T_PROJ
    in_cols = w_in_bf.shape[1]
    row = lambda w: pl.BlockSpec((T_PROJ, w), lambda i: (i, 0))
    pos = pl.BlockSpec((T_PROJ, LANES), lambda i: (i % nsb, 0))
    grp_specs = [pl.BlockSpec((T_PROJ // d, d * GROUP_W), lambda i: (i, 0)) for d in DILATIONS]
    grp_shapes = [jax.ShapeDtypeStruct((n // d, d * GROUP_W), bf16) for d in DILATIONS]
    outs = pl.pallas_call(
        _inproj_kernel,
        grid=(n // T_PROJ,),
        in_specs=[row(D_MODEL), _const_spec((1, D_MODEL)), _const_spec((D_MODEL, in_cols)),
                  _const_spec((1, 2 * D_MODEL)), pos, pos, pos],
        out_specs=grp_specs * 3 + [row(CONV_W), row(2 * D_MODEL)],
        out_shape=grp_shapes * 3
        + [jax.ShapeDtypeStruct((n, CONV_W), f32), jax.ShapeDtypeStruct((n, 2 * D_MODEL), bf16)],
        scratch_shapes=[pltpu.VMEM((T_PROJ, LANES), f32)],
        compiler_params=_cparams(("parallel",)),
        name="inproj",
    )(x2, norm1_g, w_in_bf, b_gate, cos_t, sa_t, sb_t)
    q, k, v = outs[0:N_GROUPS], outs[N_GROUPS:2 * N_GROUPS], outs[2 * N_GROUPS:3 * N_GROUPS]
    return q, k, v, outs[3 * N_GROUPS], outs[3 * N_GROUPS + 1]


def _attn_kernel(q_ref, kp_ref, kc_ref, kn_ref, vp_ref, vc_ref, vn_ref, o_ref, lse_ref,
                 kw_ref, vw_ref, *, tq, n_rows):
    i = pl.program_id(2)
    kw_ref[0:HALF_WIN] = kp_ref[0]
    kw_ref[HALF_WIN:HALF_WIN + tq] = kc_ref[0]
    kw_ref[HALF_WIN + tq:2 * HALF_WIN + tq] = kn_ref[0]
    vw_ref[0:HALF_WIN] = vp_ref[0]
    vw_ref[HALF_WIN:HALF_WIN + tq] = vc_ref[0]
    vw_ref[HALF_WIN + tq:2 * HALF_WIN + tq] = vn_ref[0]

    win = Q_SUB + 2 * HALF_WIN
    qi = lax.broadcasted_iota(i32, (Q_SUB, win), 0)
    kj = lax.broadcasted_iota(i32, (Q_SUB, win), 1)
    band = (kj - qi >= 0) & (kj - qi <= 2 * HALF_WIN)
    head_of_lane = lax.broadcasted_iota(i32, (1, GROUP_W), 1) // HEAD_DIM

    for s in range(tq // Q_SUB):
        qs = q_ref[0, s * Q_SUB:(s + 1) * Q_SUB, :]
        kwin = kw_ref[s * Q_SUB:s * Q_SUB + win, :]
        vwin = vw_ref[s * Q_SUB:s * Q_SUB + win, :]
        key_row = i * tq + (s * Q_SUB - HALF_WIN) + kj
        ok = band & (key_row >= 0) & (key_row < n_rows)
        o_acc = jnp.zeros((Q_SUB, GROUP_W), f32)
        l_acc = jnp.zeros((Q_SUB, GROUP_W), f32)
        for h in range(HEADS_PER_GROUP):
            hm = head_of_lane == h
            qh = jnp.where(hm, qs, jnp.zeros_like(qs))
            sc = lax.dot_general(qh, kwin, (((1,), (1,)), ((), ())), preferred_element_type=f32)
            sc = jnp.where(ok, sc, NEG_INF)
            m = jnp.max(sc, axis=-1, keepdims=True)
            p = jnp.exp(sc - m)
            den = jnp.sum(p, axis=-1, keepdims=True)
            pv = jnp.dot(p.astype(bf16), vwin, preferred_element_type=f32)
            o_acc = jnp.where(hm, pv / den, o_acc)
            l_acc = jnp.where(hm, m + jnp.log(den), l_acc)
        o_ref[0, s * Q_SUB:(s + 1) * Q_SUB, :] = o_acc
        lse_ref[0, s * Q_SUB:(s + 1) * Q_SUB, :] = l_acc


def _attention_group(q, k, v, batch, seq, g):
    d = DILATIONS[g]
    n_rows = seq // d
    tq = min(512, n_rows)
    nblk = n_rows // tq
    hb = tq // HALF_WIN
    n_halo_blocks = n_rows // HALF_WIN
    view = lambda t: t.reshape(batch, n_rows, d * GROUP_W)
    cur = pl.BlockSpec((1, tq, GROUP_W), lambda b, r, i: (b, i, r))
    prev = pl.BlockSpec((1, HALF_WIN, GROUP_W), lambda b, r, i: (b, jnp.maximum(i * hb - 1, 0), r))
    nxt = pl.BlockSpec((1, HALF_WIN, GROUP_W),
                       lambda b, r, i: (b, jnp.minimum((i + 1) * hb, n_halo_blocks - 1), r))
    o, lse = pl.pallas_call(
        functools.partial(_attn_kernel, tq=tq, n_rows=n_rows),
        grid=(batch, d, nblk),
        in_specs=[cur, prev, cur, nxt, prev, cur, nxt],
        out_specs=[cur, cur],
        out_shape=[jax.ShapeDtypeStruct((batch, n_rows, d * GROUP_W), f32)] * 2,
        scratch_shapes=[pltpu.VMEM((tq + 2 * HALF_WIN, GROUP_W), bf16)] * 2,
        compiler_params=_cparams(("parallel", "parallel", "parallel")),
        name=f"attn_g{g}",
    )(view(q), view(k), view(k), view(k), view(v), view(v), view(v))
    return o.reshape(batch * seq, GROUP_W), lse.reshape(batch * seq, GROUP_W)


def _merge_kernel(x_ref, up_ref, uc_ref, un_ref, gate_ref,
                  o0_ref, o1_ref, o2_ref, l0_ref, l1_ref, l2_ref,
                  dww_ref, dwb_ref, lng_ref, lnb_ref, pww_ref, pwb_ref, wup_ref, wout_ref, n2g_ref, wq_ref,
                  x1_ref, h2_ref, qp_ref, ue_ref, cv_ref, *, nsb):
    i = pl.program_id(0)
    t = T_PROJ
    first = (i % nsb) == 0
    last = (i % nsb) == nsb - 1
    ue_ref[0:CONV_HALO] = jnp.where(first, 0.0, up_ref[...])
    ue_ref[CONV_HALO:CONV_HALO + t] = uc_ref[...]
    ue_ref[CONV_HALO + t:2 * CONV_HALO + t] = jnp.where(last, 0.0, un_ref[...])

    rc = 32
    off = CONV_HALO - CONV_K // 2
    for r0 in range(0, t, rc):
        acc = jnp.zeros((rc, CONV_W), f32)
        for j in range(CONV_K):
            acc = acc + ue_ref[r0 + off + j:r0 + off + j + rc, :] * dww_ref[j:j + 1, :]
        cv_ref[r0:r0 + rc, :] = acc
    c = cv_ref[...] + dwb_ref[...]
    mu = jnp.mean(c, axis=-1, keepdims=True)
    cc = c - mu
    var = jnp.mean(cc * cc, axis=-1, keepdims=True)
    un = cc * lax.rsqrt(var + EPS) * lng_ref[...] + lnb_ref[...]
    sw = un * jax.nn.sigmoid(un)
    conv = jnp.dot(sw.astype(bf16), pww_ref[...], preferred_element_type=f32) + pwb_ref[...]

    l0 = l0_ref[...]
    l1 = l1_ref[...]
    l2 = l2_ref[...]
    lm = jnp.maximum(jnp.maximum(l0, l1), l2)
    e0 = jnp.exp(l0 - lm)
    e1 = jnp.exp(l1 - lm)
    e2 = jnp.exp(l2 - lm)
    comb = (e0 * o0_ref[...] + e1 * o1_ref[...] + e2 * o2_ref[...]) / (e0 + e1 + e2)
    attn = jnp.dot(comb.astype(bf16), wup_ref[...], preferred_element_type=f32)

    g_attn = gate_ref[:, 0:D_MODEL].astype(f32)
    g_conv = gate_ref[:, D_MODEL:2 * D_MODEL].astype(f32)
    mixed = g_attn * attn + g_conv * conv
    x1 = x_ref[...] + jnp.dot(mixed.astype(bf16), wout_ref[...], preferred_element_type=f32)
    x1_ref[...] = x1
    ms = jnp.mean(x1 * x1, axis=-1, keepdims=True)
    h2 = x1 * lax.rsqrt(ms + EPS) * n2g_ref[...]
    for c in range(D_MODEL // LANES):
        h2_ref[pl.ds(c, t, stride=SUBLANES), :] = h2[:, c * LANES:(c + 1) * LANES]
    qp_ref[...] = jnp.dot(h2.astype(bf16), wq_ref[...], preferred_element_type=f32).astype(bf16)


def _merge(x2, seq, u, gates, os_, ls_, weights):
    n = x2.shape[0]
    t = T_PROJ
    nsb = seq // t
    hb = t // CONV_HALO
    nhalo = n // CONV_HALO
    row = lambda w: pl.BlockSpec((t, w), lambda i: (i, 0))
    prev = pl.BlockSpec((CONV_HALO, CONV_W), lambda i: (jnp.maximum(i * hb - 1, 0), 0))
    nxt = pl.BlockSpec((CONV_HALO, CONV_W), lambda i: (jnp.minimum((i + 1) * hb, nhalo - 1), 0))
    wspecs = [_const_spec(w.shape) for w in weights]
    qw = weights[-1].shape[1]
    return pl.pallas_call(
        functools.partial(_merge_kernel, nsb=nsb),
        grid=(n // t,),
        in_specs=[row(D_MODEL), prev, row(CONV_W), nxt, row(2 * D_MODEL)] + [row(GROUP_W)] * 6 + wspecs,
        out_specs=[row(D_MODEL), pl.BlockSpec((t * SUBLANES, LANES), lambda i: (i, 0)), row(qw)],
        out_shape=[jax.ShapeDtypeStruct((n, D_MODEL), f32), jax.ShapeDtypeStruct((n * SUBLANES, LANES), f32),
                   jax.ShapeDtypeStruct((n, qw), bf16)],
        scratch_shapes=[pltpu.VMEM((t + 2 * CONV_HALO, CONV_W), f32), pltpu.VMEM((t, CONV_W), f32)],
        compiler_params=_cparams(("parallel",)),
        name="merge",
    )(x2, u, u, u, gates, *os_, *ls_, *weights)


def _candidate_slabs():
    slabs = [("row", 0, 0, 16), ("row", 1, 0, 8)]
    for j in range(PEER_TOPK):
        hi = PEER_TOPK // (j + 1)
        if hi > 2:
            slabs.append(("col", j, 2, hi))
    return slabs


def _route_kernel(qp_ref, keys_ref, ids_ref, gate_ref, val_ref, idx_ref, best_ref, idt_ref, gt_ref):
    t = T_PROJ
    k_iota = lax.broadcasted_iota(i32, (N_SUBKEYS, t), 0)
    for hc in range(2 * PEER_HEADS):
        q = qp_ref[:, hc * N_SUBKEYS:(hc + 1) * N_SUBKEYS]
        s = lax.dot_general(keys_ref[hc], q, (((1,), (1,)), ((), ())), preferred_element_type=f32)
        for r in range(PEER_TOPK):
            m = jnp.max(s, axis=0, keepdims=True)
            am = jnp.min(jnp.where(s == m, k_iota, N_SUBKEYS), axis=0, keepdims=True)
            s = jnp.where(k_iota == am, -jnp.inf, s)
            val_ref[hc, pl.ds(r, 1), :] = m
            idx_ref[hc, pl.ds(r, 1), :] = am

    r_iota = lax.broadcasted_iota(i32, (PEER_TOPK, t), 0)
    slabs = _candidate_slabs()
    for h in range(PEER_HEADS):
        v0 = val_ref[2 * h]
        v1 = val_ref[2 * h + 1]
        i0 = idx_ref[2 * h]
        i1 = idx_ref[2 * h + 1]
        cands, flats, eids = [], [], []
        for kind, fixed, lo, hi in slabs:
            if kind == "row":
                c = v0[fixed:fixed + 1, :] + v1
                fl = fixed * PEER_TOPK + r_iota
                ei = i0[fixed:fixed + 1, :] * N_SUBKEYS + i1
            else:
                c = v0 + v1[fixed:fixed + 1, :]
                fl = r_iota * PEER_TOPK + fixed
                ei = i0 * N_SUBKEYS + i1[fixed:fixed + 1, :]
            valid = (r_iota >= lo) & (r_iota < hi)
            cands.append(jnp.where(valid, c, -jnp.inf))
            flats.append(fl)
            eids.append(ei)
        big = PEER_TOPK * PEER_TOPK
        for r in range(PEER_TOPK):
            m = functools.reduce(jnp.maximum, cands)
            m = jnp.max(m, axis=0, keepdims=True)
            fsel = functools.reduce(jnp.minimum, [jnp.where(c == m, fl, big) for c, fl in zip(cands, flats)])
            fsel = jnp.min(fsel, axis=0, keepdims=True)
            hit = [fl == fsel for fl in flats]
            eid = functools.reduce(jnp.maximum, [jnp.where(hh, ei, -1) for hh, ei in zip(hit, eids)])
            eid = jnp.max(eid, axis=0, keepdims=True)
            cands = [jnp.where(hh, -jnp.inf, c) for hh, c in zip(hit, cands)]
            best_ref[pl.ds(r, 1), :] = m
            idt_ref[pl.ds(h * PEER_TOPK + r, 1), :] = eid
        b = best_ref[...]
        e = jnp.exp(b - jnp.max(b, axis=0, keepdims=True))
        gt_ref[h * PEER_TOPK:(h + 1) * PEER_TOPK, :] = e / jnp.sum(e, axis=0, keepdims=True)
    ids_ref[...] = idt_ref[...].T
    gate_ref[...] = gt_ref[...].T


def _route(qp, keys_bf):
    n, qw = qp.shape
    t = T_PROJ
    return pl.pallas_call(
        _route_kernel,
        grid=(n // t,),
        in_specs=[pl.BlockSpec((t, qw), lambda i: (i, 0)), _const_spec(keys_bf.shape)],
        out_specs=[pl.BlockSpec((t, PEER_SLOTS), lambda i: (i, 0))] * 2,
        out_shape=[jax.ShapeDtypeStruct((n, PEER_SLOTS), i32), jax.ShapeDtypeStruct((n, PEER_SLOTS), f32)],
        scratch_shapes=[pltpu.VMEM((2 * PEER_HEADS, PEER_TOPK, t), f32), pltpu.VMEM((2 * PEER_HEADS, PEER_TOPK, t), i32),
                        pltpu.VMEM((PEER_TOPK, t), f32), pltpu.VMEM((PEER_SLOTS, t), i32),
                        pltpu.VMEM((PEER_SLOTS, t), f32)],
        compiler_params=_cparams(("parallel",)),
        name="route",
    )(qp, keys_bf)


def _pack_table(tbl):
    bits = lax.bitcast_convert_type(tbl.astype(bf16), jnp.uint16).astype(jnp.uint32)
    half = D_MODEL // 2
    words = bits[:, :half] | (bits[:, half:] << 16)
    return lax.bitcast_convert_type(words, i32).reshape(tbl.shape[0], ROW_WORDS, LANES)


def _unpack(words):
    lo = pltpu.bitcast(words << 16, f32)
    hi = pltpu.bitcast(words & jnp.int32(-65536), f32)
    return lo, hi


def _split2(x):
    hi = x.astype(bf16)
    lo = (x - hi.astype(f32)).astype(bf16)
    return hi, lo


def _gather_chunk(ids_ref, tbl_ref, tile_ref, c, base):
    e = None
    for k in range(GATHER_CHUNK):
        e = ids_ref[base + k]
        tile_ref[c, k * ROW_WORDS:(k + 1) * ROW_WORDS, :] = tbl_ref[e]
    return e


def _two_token_pipeline(ids_ref, tbl_ref, tile_a, tile_b, consume, init):
    tile_b[...] = jnp.zeros(tile_b.shape, i32)
    last = T_PEER - 1

    def half_step(t_gather, gather_tile, t_consume, consume_tile, dep):
        base = jnp.minimum(t_gather, last) * PEER_SLOTS
        tc = jnp.clip(t_consume, 0, last)
        acc = init
        for c in range(N_CHUNKS):
            dep = _gather_chunk(ids_ref, tbl_ref, gather_tile, c, base + c * GATHER_CHUNK + (dep >> 31))
            acc = consume(consume_tile, c, tc, acc)
        return dep

    def pairs_step(i, dep):
        for p in range(PAIRS_PER_STEP):
            even = 2 * (PAIRS_PER_STEP * i + p)
            dep = half_step(even, tile_a, even - 1, tile_b, dep)
            dep = half_step(even + 1, tile_b, even, tile_a, dep)
        return dep

    lax.fori_loop(0, T_PEER // (2 * PAIRS_PER_STEP) + 1, pairs_step, jnp.int32(0))


def _tile_scratch():
    return pltpu.VMEM((N_CHUNKS, GATHER_CHUNK * ROW_WORDS, LANES), i32)


def _peer_u_kernel(ids_ref, xr_ref, g_ref, tbl_ref, w_ref, tile_a, tile_b):
    crow = GATHER_CHUNK * ROW_WORDS
    half = crow // 2
    kk = lax.broadcasted_iota(i32, (2 * LANES, LANES), 0)
    nn = lax.broadcasted_iota(i32, (2 * LANES, LANES), 1)
    summer = ((kk < LANES) == (nn % GATHER_CHUNK < GATHER_CHUNK // 2)).astype(bf16)
    rr = lax.broadcasted_iota(i32, (half, LANES), 0)
    ll = lax.broadcasted_iota(i32, (half, LANES), 1)
    own_row = ll % (GATHER_CHUNK // 2) == rr // ROW_WORDS
    lane_chunk = ll // GATHER_CHUNK

    def consume(tile, c, t, acc):
        xt = xr_ref[pl.ds(pl.multiple_of(t * SUBLANES, SUBLANES), SUBLANES), :]
        xlo = xt[0:ROW_WORDS]
        xhi = xt[ROW_WORDS:2 * ROW_WORDS]
        x2lo = pltpu.repeat(jnp.concatenate([xlo, xlo], axis=0), crow // SUBLANES, axis=0)
        x2hi = pltpu.repeat(jnp.concatenate([xhi, xhi], axis=0), crow // SUBLANES, axis=0)
        lo, hi = _unpack(tile[c])
        p = lo * x2lo + hi * x2hi
        p_hi, p_lo = _split2(jnp.concatenate([p[0:half], p[half:crow]], axis=1))
        rs = jnp.dot(p_hi, summer, preferred_element_type=f32) + jnp.dot(p_lo, summer, preferred_element_type=f32)
        acc = acc + jnp.sum(jnp.where(own_row & (lane_chunk == c), rs, 0.0), axis=0, keepdims=True)
        if c == N_CHUNKS - 1:
            gelu = 0.5 * acc * (1.0 + lax.erf(acc * (2.0 ** -0.5)))
            w_ref[t] = g_ref[t] * gelu
        return acc

    _two_token_pipeline(ids_ref, tbl_ref, tile_a, tile_b, consume, jnp.zeros((1, PEER_SLOTS), f32))


def _peer_u(ids_flat, h2r, gate3, tbl, tok_off, n_tok):
    t = T_PEER
    b0 = tok_off // t
    return pl.pallas_call(
        _peer_u_kernel,
        grid=(n_tok // t,),
        in_specs=[pl.BlockSpec((t * PEER_SLOTS,), lambda i: (i + b0,), memory_space=pltpu.SMEM),
                  pl.BlockSpec((t * SUBLANES, LANES), lambda i: (i + b0, 0)),
                  pl.BlockSpec((t, 1, PEER_SLOTS), lambda i: (i + b0, 0, 0)),
                  _const_spec(tbl.shape)],
        out_specs=pl.BlockSpec((t, 1, PEER_SLOTS), lambda i: (i, 0, 0)),
        out_shape=jax.ShapeDtypeStruct((n_tok, 1, PEER_SLOTS), f32),
        scratch_shapes=[_tile_scratch(), _tile_scratch()],
        compiler_params=_cparams(("parallel",)),
        name="peer_u",
    )(ids_flat, h2r, gate3, tbl)


def _peer_v_sc(ids, w, tbl_words, tok_off):
    n_tok = w.shape[0]
    tpw = n_tok // SC_WORKERS
    lanes = SC_LANES
    mesh = plsc.VectorSubcoreMesh(core_axis_name="c", subcore_axis_name="s")

    @functools.partial(
        pl.kernel, mesh=mesh,
        out_type=jax.ShapeDtypeStruct((n_tok, D_MODEL), f32),
        scratch_types=[
            pltpu.VMEM((SC_HALF,), i32), pltpu.VMEM((SC_HALF,), i32),
            pltpu.VMEM((SC_HALF, ROW_WORDS_FLAT), i32), pltpu.VMEM((SC_HALF, ROW_WORDS_FLAT), i32),
            pltpu.VMEM((PEER_SLOTS,), f32),
            pltpu.VMEM((D_MODEL,), f32),
            pltpu.SemaphoreType.DMA, pltpu.SemaphoreType.DMA,
        ],
        compiler_params=dataclasses.replace(pltpu.CompilerParams(), needs_layout_passes=False),
        name="peer_v_sc",
    )
    def run(ids_hbm, w_hbm, tbl_hbm, out_hbm, idx_a, idx_b, rows_a, rows_b, w_v, out_v, sem_a, sem_b):
        wid = lax.axis_index("s") * SC_CORES + lax.axis_index("c")
        base = wid * tpw

        def accumulate(rows, w_off):
            for db in range(ROW_WORDS_FLAT // SC_DIM_BLOCK):
                def group_body(g, carry):
                    accs = []
                    for wc in range(SC_DIM_BLOCK // lanes):
                        accs.append(out_v[pl.ds(db * SC_DIM_BLOCK + wc * lanes, lanes)])
                        accs.append(out_v[pl.ds(ROW_WORDS_FLAT + db * SC_DIM_BLOCK + wc * lanes, lanes)])
                    j0 = g * SC_ROW_GROUP
                    wchunk = w_v[pl.ds(pl.multiple_of((w_off + j0) // lanes * lanes, lanes), lanes)]
                    sub = (w_off + j0) % lanes
                    for r in range(SC_ROW_GROUP):
                        wj = jnp.take(wchunk, jnp.full((lanes,), sub + r, i32))
                        for wc in range(SC_DIM_BLOCK // lanes):
                            word = rows[j0 + r, pl.ds(db * SC_DIM_BLOCK + wc * lanes, lanes)]
                            lo = lax.bitcast_convert_type(word << 16, f32)
                            hi = lax.bitcast_convert_type(word & jnp.int32(-65536), f32)
                            accs[2 * wc] = accs[2 * wc] + wj * lo
                            accs[2 * wc + 1] = accs[2 * wc + 1] + wj * hi
                    for wc in range(SC_DIM_BLOCK // lanes):
                        out_v[pl.ds(db * SC_DIM_BLOCK + wc * lanes, lanes)] = accs[2 * wc]
                        out_v[pl.ds(ROW_WORDS_FLAT + db * SC_DIM_BLOCK + wc * lanes, lanes)] = accs[2 * wc + 1]
                    return carry
                lax.fori_loop(0, SC_HALF // SC_ROW_GROUP, group_body, 0)

        def token(i, carry):
            t = base + i
            pltpu.sync_copy(ids_hbm.at[tok_off + t, pl.ds(0, SC_HALF)], idx_a)
            pltpu.sync_copy(ids_hbm.at[tok_off + t, pl.ds(SC_HALF, SC_HALF)], idx_b)
            copy_a = pltpu.async_copy(tbl_hbm.at[idx_a], rows_a, sem_a)
            copy_b = pltpu.async_copy(tbl_hbm.at[idx_b], rows_b, sem_b)
            pltpu.sync_copy(w_hbm.at[t], w_v)
            for q in range(D_MODEL // lanes):
                out_v[pl.ds(q * lanes, lanes)] = jnp.zeros((lanes,), f32)
            copy_a.wait()
            accumulate(rows_a, 0)
            copy_b.wait()
            accumulate(rows_b, SC_HALF)
            pltpu.sync_copy(out_v, out_hbm.at[t])
            return carry

        lax.fori_loop(0, tpw, token, 0)

    return run(ids, w, tbl_words)


def _final_kernel(x1_ref, p_ref, g_ref, y_ref):
    x = x1_ref[...] + p_ref[...]
    ms = jnp.mean(x * x, axis=-1, keepdims=True)
    y_ref[...] = x * lax.rsqrt(ms + EPS) * g_ref[...]


def _final(x1, peer_out, final_g):
    n = x1.shape[0]
    t = 512
    row = pl.BlockSpec((t, D_MODEL), lambda i: (i, 0))
    return pl.pallas_call(
        _final_kernel,
        grid=(n // t,),
        in_specs=[row, row, _const_spec((1, D_MODEL))],
        out_specs=row,
        out_shape=jax.ShapeDtypeStruct((n, D_MODEL), f32),
        compiler_params=_cparams(("parallel",)),
        name="final_norm",
    )(x1, peer_out, final_g)


def _rotary_tables(seq):
    half = ROT_DIM // 2
    inv = ROPE_THETA ** (-jnp.arange(half, dtype=f32) * 2.0 / ROT_DIM)
    ang = jnp.arange(seq, dtype=jnp.int32).astype(f32)[:, None] * inv[None, :]
    cos = jnp.cos(ang)
    sin = jnp.sin(ang)
    pad = HEAD_DIM - ROT_DIM
    one = jnp.ones((seq, pad), f32)
    zero = jnp.zeros((seq, pad), f32)
    zh = jnp.zeros((seq, half), f32)
    cos_h = jnp.concatenate([cos, cos, one], axis=1)
    sa_h = jnp.concatenate([-sin, zh, zero], axis=1)
    sb_h = jnp.concatenate([zh, sin, zero], axis=1)
    rep = LANES // HEAD_DIM
    return tuple(jnp.tile(t, (1, rep)) for t in (cos_h, sa_h, sb_h))


def _layer(x, params, tables):
    (norm1_g, w_in_bf, b_gate, merge_w, keys_bf, u_tbl, v_tbl, final_g) = params
    batch, seq, _ = x.shape
    n = batch * seq
    x2 = x.reshape(n, D_MODEL)
    cos_t, sa_t, sb_t = tables
    q, k, v, u, gates = _inproj(x2, seq, norm1_g, w_in_bf, b_gate, cos_t, sa_t, sb_t)
    os_, ls_ = [], []
    for g in range(N_GROUPS):
        o, lse = _attention_group(q[g], k[g], v[g], batch, seq, g)
        os_.append(o)
        ls_.append(lse)
    x1, h2r, qp = _merge(x2, seq, u, gates, os_, ls_, merge_w)
    ids, gate = _route(qp, keys_bf)
    ids_flat = ids.reshape(n * PEER_SLOTS)
    gate3 = gate.reshape(n, 1, PEER_SLOTS)
    n_chunk = n // PEER_TOKEN_CHUNKS
    pv = []
    for ch in range(PEER_TOKEN_CHUNKS):
        w3 = _peer_u(ids_flat, h2r, gate3, u_tbl, ch * n_chunk, n_chunk)
        pv.append(_peer_v_sc(ids, w3.reshape(n_chunk, PEER_SLOTS), v_tbl, ch * n_chunk))
    y = _final(x1, jnp.concatenate(pv, axis=0), final_g)
    return y.reshape(batch, seq, D_MODEL)


def kernel(x_prompt, x_sample, norm1_g, w_in, b_gate, w_attn_up, conv_dw_w, conv_dw_b, conv_ln_g, conv_ln_b,
           conv_pw_w, conv_pw_b, w_out, norm2_g, peer_wq, peer_keys, peer_u, peer_v, final_g):
    assert w_in.shape[0] == 1, "one encoder layer followed by the final norm"
    row = lambda a: a.reshape(1, -1)
    tables = _rotary_tables(max(x_prompt.shape[1], x_sample.shape[1]))
    l = 0
    merge_w = (conv_dw_w[l], row(conv_dw_b[l]), row(conv_ln_g[l]), row(conv_ln_b[l]),
               conv_pw_w[l].astype(bf16), row(conv_pw_b[l]), w_attn_up[l].astype(bf16),
               w_out[l].astype(bf16), row(norm2_g[l]), peer_wq[l].astype(bf16))
    keys_bf = peer_keys[l].astype(bf16).reshape(2 * PEER_HEADS, N_SUBKEYS, -1)
    params = (row(norm1_g[l]), w_in[l].astype(bf16), row(b_gate[l]), merge_w, keys_bf,
              _pack_table(peer_u[l]), _pack_table(peer_v[l]).reshape(N_EXPERTS, ROW_WORDS_FLAT), row(final_g))

    def group(x):
        step = x.shape[0] // BATCH_SUBGROUPS
        parts = [_layer(x[b:b + step], params, tables) for b in range(0, x.shape[0], step)]
        return jnp.concatenate(parts, axis=0)

    return (group(x_prompt), group(x_sample))
```

```python
import dataclasses
import functools
import math

import numpy as np
import jax
import jax.numpy as jnp
from jax import lax
from jax.experimental import pallas as pl
from jax.experimental.pallas import tpu as pltpu
from jax.experimental.pallas import tpu_sc as plsc

f32 = jnp.float32
bf16 = jnp.bfloat16
i32 = jnp.int32

D_MODEL = 1024
HEAD_DIM = 64
N_GROUPS = 3
HEADS_PER_GROUP = 4
GROUP_W = HEADS_PER_GROUP * HEAD_DIM
ATTN_W = N_GROUPS * GROUP_W
WINDOWS = (128, 512, 2048)
DILATIONS = (1, 4, 16)
HALF_WIN = 64
ROT_DIM = HEAD_DIM // 4
ROPE_THETA = 500000.0
NEG_INF = -1e30
CONV_W = 512
CONV_K = 31
CONV_HALO = 16
PEER_HEADS = 8
N_SUBKEYS = 128
N_EXPERTS = N_SUBKEYS * N_SUBKEYS
PEER_TOPK = 16
PEER_SLOTS = PEER_HEADS * PEER_TOPK
EPS = 1e-6

LANES = 128
SUBLANES = 8
ROW_WORDS = D_MODEL // 2 // LANES
VMEM_LIMIT = 56 * 1024 * 1024

T_PROJ = 256
T_PEER = 256
PAIRS_PER_STEP = 8
GATHER_CHUNK = 32
N_CHUNKS = (8 * 16) // GATHER_CHUNK
BATCH_SUBGROUPS = 2
PEER_CHUNK_SHARES = (4, 4, 4, 2, 1, 1)

SC_CORES = 2
SC_SUBCORES = 16
SC_LANES = 16
SC_WORKERS = SC_CORES * SC_SUBCORES
ROW_WORDS_FLAT = D_MODEL // 2
SC_HALF = PEER_SLOTS // 2
SC_DIM_BLOCK = 128
SC_ROW_GROUP = 8
Q_SUB = 128


def _cparams(sem):
    return pltpu.CompilerParams(dimension_semantics=sem, vmem_limit_bytes=VMEM_LIMIT)


def _const_spec(shape):
    nd = len(shape)
    return pl.BlockSpec(shape, lambda *_: (0,) * nd, pipeline_mode=pl.Buffered(1))


def _inproj_kernel(x_ref, g_ref, w_ref, bg_ref, cos_ref, sa_ref, sb_ref, *refs):
    qkv_refs = (refs[0:N_GROUPS], refs[N_GROUPS:2 * N_GROUPS], refs[2 * N_GROUPS:3 * N_GROUPS])
    u_ref, gate_ref, stage_ref = refs[3 * N_GROUPS:]
    x = x_ref[...]
    ms = jnp.mean(x * x, axis=-1, keepdims=True)
    h = (x * lax.rsqrt(ms + EPS) * g_ref[...]).astype(bf16)

    def proj(lo, hi):
        return jnp.dot(h, w_ref[:, lo:hi], preferred_element_type=f32)

    cos = cos_ref[...]
    sa = sa_ref[...]
    sb = sb_ref[...]

    def rotary(tc, scale):
        r = tc * cos + pltpu.roll(tc, LANES - ROT_DIM // 2, 1) * sa + pltpu.roll(tc, ROT_DIM // 2, 1) * sb
        return r * scale

    def emit(t, out_refs, fn):
        for c in range(ATTN_W // LANES):
            g, half = divmod(c, GROUP_W // LANES)
            d = DILATIONS[g]
            chunk = fn(t[:, c * LANES:(c + 1) * LANES])
            if d == 1:
                out_refs[g][:, half * LANES:(half + 1) * LANES] = chunk.astype(bf16)
                continue
            stage_ref[...] = chunk
            for r in range(d):
                col = r * GROUP_W + half * LANES
                out_refs[g][:, col:col + LANES] = stage_ref[pl.ds(r, T_PROJ // d, stride=d), :].astype(bf16)

    emit(proj(0, ATTN_W), qkv_refs[0], lambda tc: rotary(tc, HEAD_DIM ** -0.5))
    emit(proj(ATTN_W, 2 * ATTN_W), qkv_refs[1], lambda tc: rotary(tc, 1.0))
    emit(proj(2 * ATTN_W, 3 * ATTN_W), qkv_refs[2], lambda tc: tc)
    c0 = 3 * ATTN_W
    a = proj(c0, c0 + CONV_W)
    b = proj(c0 + CONV_W, c0 + 2 * CONV_W)
    u_ref[...] = a * jax.nn.sigmoid(b)
    gates = proj(c0 + 2 * CONV_W, c0 + 2 * CONV_W + 2 * D_MODEL) + bg_ref[...]
    gate_ref[...] = jax.nn.sigmoid(gates).astype(bf16)


def _inproj(x2, seq, norm1_g, w_in_bf, b_gate, cos_t, sa_t, sb_t):
    n = x2.shape[0]
    nsb = seq // T_PROJ
    in_cols = w_in_bf.shape[1]
    row = lambda w: pl.BlockSpec((T_PROJ, w), lambda i: (i, 0))
    pos = pl.BlockSpec((T_PROJ, LANES), lambda i: (i % nsb, 0))
    grp_specs = [pl.BlockSpec((T_PROJ // d, d * GROUP_W), lambda i: (i, 0)) for d in DILATIONS]
    grp_shapes = [jax.ShapeDtypeStruct((n // d, d * GROUP_W), bf16) for d in DILATIONS]
    outs = pl.pallas_call(
        _inproj_kernel,
        grid=(n // T_PROJ,),
        in_specs=[row(D_MODEL), _const_spec((1, D_MODEL)), _const_spec((D_MODEL, in_cols)),
                  _const_spec((1, 2 * D_MODEL)), pos, pos, pos],
        out_specs=grp_specs * 3 + [row(CONV_W), row(2 * D_MODEL)],
        out_shape=grp_shapes * 3
        + [jax.ShapeDtypeStruct((n, CONV_W), f32), jax.ShapeDtypeStruct((n, 2 * D_MODEL), bf16)],
        scratch_shapes=[pltpu.VMEM((T_PROJ, LANES), f32)],
        compiler_params=_cparams(("parallel",)),
        name="inproj",
    )(x2, norm1_g, w_in_bf, b_gate, cos_t, sa_t, sb_t)
    q, k, v = outs[0:N_GROUPS], outs[N_GROUPS:2 * N_GROUPS], outs[2 * N_GROUPS:3 * N_GROUPS]
    return q, k, v, outs[3 * N_GROUPS], outs[3 * N_GROUPS + 1]


def _attn_kernel(q_ref, kp_ref, kc_ref, kn_ref, vp_ref, vc_ref, vn_ref, o_ref, lse_ref,
                 kw_ref, vw_ref, *, tq, n_rows):
    i = pl.program_id(2)
    kw_ref[0:HALF_WIN] = kp_ref[0]
    kw_ref[HALF_WIN:HALF_WIN + tq] = kc_ref[0]
    kw_ref[HALF_WIN + tq:2 * HALF_WIN + tq] = kn_ref[0]
    vw_ref[0:HALF_WIN] = vp_ref[0]
    vw_ref[HALF_WIN:HALF_WIN + tq] = vc_ref[0]
    vw_ref[HALF_WIN + tq:2 * HALF_WIN + tq] = vn_ref[0]

    win = Q_SUB + 2 * HALF_WIN
    qi = lax.broadcasted_iota(i32, (Q_SUB, win), 0)
    kj = lax.broadcasted_iota(i32, (Q_SUB, win), 1)
    band = (kj - qi >= 0) & (kj - qi <= 2 * HALF_WIN)
    head_of_lane = lax.broadcasted_iota(i32, (1, GROUP_W), 1) // HEAD_DIM

    for s in range(tq // Q_SUB):
        qs = q_ref[0, s * Q_SUB:(s + 1) * Q_SUB, :]
        kwin = kw_ref[s * Q_SUB:s * Q_SUB + win, :]
        vwin = vw_ref[s * Q_SUB:s * Q_SUB + win, :]
        key_row = i * tq + (s * Q_SUB - HALF_WIN) + kj
        ok = band & (key_row >= 0) & (key_row < n_rows)
        o_acc = jnp.zeros((Q_SUB, GROUP_W), f32)
        l_acc = jnp.zeros((Q_SUB, GROUP_W), f32)
        for h in range(HEADS_PER_GROUP):
            hm = head_of_lane == h
            qh = jnp.where(hm, qs, jnp.zeros_like(qs))
            sc = lax.dot_general(qh, kwin, (((1,), (1,)), ((), ())), preferred_element_type=f32)
            sc = jnp.where(ok, sc, NEG_INF)
            m = jnp.max(sc, axis=-1, keepdims=True)
            p = jnp.exp(sc - m)
            den = jnp.sum(p, axis=-1, keepdims=True)
            pv = jnp.dot(p.astype(bf16), vwin, preferred_element_type=f32)
            o_acc = jnp.where(hm, pv / den, o_acc)
            l_acc = jnp.where(hm, m + jnp.log(den), l_acc)
        o_ref[0, s * Q_SUB:(s + 1) * Q_SUB, :] = o_acc
        lse_ref[0, s * Q_SUB:(s + 1) * Q_SUB, :] = l_acc


def _attention_group(q, k, v, batch, seq, g):
    d = DILATIONS[g]
    n_rows = seq // d
    tq = min(512, n_rows)
    nblk = n_rows // tq
    hb = tq // HALF_WIN
    n_halo_blocks = n_rows // HALF_WIN
    view = lambda t: t.reshape(batch, n_rows, d * GROUP_W)
    cur = pl.BlockSpec((1, tq, GROUP_W), lambda b, r, i: (b, i, r))
    prev = pl.BlockSpec((1, HALF_WIN, GROUP_W), lambda b, r, i: (b, jnp.maximum(i * hb - 1, 0), r))
    nxt = pl.BlockSpec((1, HALF_WIN, GROUP_W),
                       lambda b, r, i: (b, jnp.minimum((i + 1) * hb, n_halo_blocks - 1), r))
    o, lse = pl.pallas_call(
        functools.partial(_attn_kernel, tq=tq, n_rows=n_rows),
        grid=(batch, d, nblk),
        in_specs=[cur, prev, cur, nxt, prev, cur, nxt],
        out_specs=[cur, cur],
        out_shape=[jax.ShapeDtypeStruct((batch, n_rows, d * GROUP_W), f32)] * 2,
        scratch_shapes=[pltpu.VMEM((tq + 2 * HALF_WIN, GROUP_W), bf16)] * 2,
        compiler_params=_cparams(("parallel", "parallel", "parallel")),
        name=f"attn_g{g}",
    )(view(q), view(k), view(k), view(k), view(v), view(v), view(v))
    return o.reshape(batch * seq, GROUP_W), lse.reshape(batch * seq, GROUP_W)


def _merge_kernel(x_ref, up_ref, uc_ref, un_ref, gate_ref,
                  o0_ref, o1_ref, o2_ref, l0_ref, l1_ref, l2_ref,
                  dww_ref, dwb_ref, lng_ref, lnb_ref, pww_ref, pwb_ref, wup_ref, wout_ref, n2g_ref, wq_ref,
                  x1_ref, h2_ref, qp_ref, ue_ref, cv_ref, *, nsb):
    i = pl.program_id(0)
    t = T_PROJ
    first = (i % nsb) == 0
    last = (i % nsb) == nsb - 1
    ue_ref[0:CONV_HALO] = jnp.where(first, 0.0, up_ref[...])
    ue_ref[CONV_HALO:CONV_HALO + t] = uc_ref[...]
    ue_ref[CONV_HALO + t:2 * CONV_HALO + t] = jnp.where(last, 0.0, un_ref[...])

    rc = 32
    off = CONV_HALO - CONV_K // 2
    for r0 in range(0, t, rc):
        acc = jnp.zeros((rc, CONV_W), f32)
        for j in range(CONV_K):
            acc = acc + ue_ref[r0 + off + j:r0 + off + j + rc, :] * dww_ref[j:j + 1, :]
        cv_ref[r0:r0 + rc, :] = acc
    c = cv_ref[...] + dwb_ref[...]
    mu = jnp.mean(c, axis=-1, keepdims=True)
    cc = c - mu
    var = jnp.mean(cc * cc, axis=-1, keepdims=True)
    un = cc * lax.rsqrt(var + EPS) * lng_ref[...] + lnb_ref[...]
    sw = un * jax.nn.sigmoid(un)
    conv = jnp.dot(sw.astype(bf16), pww_ref[...], preferred_element_type=f32) + pwb_ref[...]

    l0 = l0_ref[...]
    l1 = l1_ref[...]
    l2 = l2_ref[...]
    lm = jnp.maximum(jnp.maximum(l0, l1), l2)
    e0 = jnp.exp(l0 - lm)
    e1 = jnp.exp(l1 - lm)
    e2 = jnp.exp(l2 - lm)
    comb = (e0 * o0_ref[...] + e1 * o1_ref[...] + e2 * o2_ref[...]) / (e0 + e1 + e2)
    attn = jnp.dot(comb.astype(bf16), wup_ref[...], preferred_element_type=f32)

    g_attn = gate_ref[:, 0:D_MODEL].astype(f32)
    g_conv = gate_ref[:, D_MODEL:2 * D_MODEL].astype(f32)
    mixed = g_attn * attn + g_conv * conv
    x1 = x_ref[...] + jnp.dot(mixed.astype(bf16), wout_ref[...], preferred_element_type=f32)
    x1_ref[...] = x1
    ms = jnp.mean(x1 * x1, axis=-1, keepdims=True)
    h2 = x1 * lax.rsqrt(ms + EPS) * n2g_ref[...]
    for c in range(D_MODEL // LANES):
        h2_ref[pl.ds(c, t, stride=SUBLANES), :] = h2[:, c * LANES:(c + 1) * LANES]
    qp_ref[...] = jnp.dot(h2.astype(bf16), wq_ref[...], preferred_element_type=f32).astype(bf16)


def _merge(x2, seq, u, gates, os_, ls_, weights):
    n = x2.shape[0]
    t = T_PROJ
    nsb = seq // t
    hb = t // CONV_HALO
    nhalo = n // CONV_HALO
    row = lambda w: pl.BlockSpec((t, w), lambda i: (i, 0))
    prev = pl.BlockSpec((CONV_HALO, CONV_W), lambda i: (jnp.maximum(i * hb - 1, 0), 0))
    nxt = pl.BlockSpec((CONV_HALO, CONV_W), lambda i: (jnp.minimum((i + 1) * hb, nhalo - 1), 0))
    wspecs = [_const_spec(w.shape) for w in weights]
    qw = weights[-1].shape[1]
    return pl.pallas_call(
        functools.partial(_merge_kernel, nsb=nsb),
        grid=(n // t,),
        in_specs=[row(D_MODEL), prev, row(CONV_W), nxt, row(2 * D_MODEL)] + [row(GROUP_W)] * 6 + wspecs,
        out_specs=[row(D_MODEL), pl.BlockSpec((t * SUBLANES, LANES), lambda i: (i, 0)), row(qw)],
        out_shape=[jax.ShapeDtypeStruct((n, D_MODEL), f32), jax.ShapeDtypeStruct((n * SUBLANES, LANES), f32),
                   jax.ShapeDtypeStruct((n, qw), bf16)],
        scratch_shapes=[pltpu.VMEM((t + 2 * CONV_HALO, CONV_W), f32), pltpu.VMEM((t, CONV_W), f32)],
        compiler_params=_cparams(("parallel",)),
        name="merge",
    )(x2, u, u, u, gates, *os_, *ls_, *weights)


def _candidate_slabs():
    slabs = [("row", 0, 0, 16), ("row", 1, 0, 8)]
    for j in range(PEER_TOPK):
        hi = PEER_TOPK // (j + 1)
        if hi > 2:
            slabs.append(("col", j, 2, hi))
    return slabs


def _route_kernel(qp_ref, keys_ref, ids_ref, gate_ref, val_ref, idx_ref, best_ref, idt_ref, gt_ref):
    t = T_PROJ
    k_iota = lax.broadcasted_iota(i32, (N_SUBKEYS, t), 0)
    for hc in range(2 * PEER_HEADS):
        q = qp_ref[:, hc * N_SUBKEYS:(hc + 1) * N_SUBKEYS]
        s = lax.dot_general(keys_ref[hc], q, (((1,), (1,)), ((), ())), preferred_element_type=f32)
        for r in range(PEER_TOPK):
            m = jnp.max(s, axis=0, keepdims=True)
            am = jnp.min(jnp.where(s == m, k_iota, N_SUBKEYS), axis=0, keepdims=True)
            s = jnp.where(k_iota == am, -jnp.inf, s)
            val_ref[hc, pl.ds(r, 1), :] = m
            idx_ref[hc, pl.ds(r, 1), :] = am

    r_iota = lax.broadcasted_iota(i32, (PEER_TOPK, t), 0)
    slabs = _candidate_slabs()
    for h in range(PEER_HEADS):
        v0 = val_ref[2 * h]
        v1 = val_ref[2 * h + 1]
        i0 = idx_ref[2 * h]
        i1 = idx_ref[2 * h + 1]
        cands, flats, eids = [], [], []
        for kind, fixed, lo, hi in slabs:
            if kind == "row":
                c = v0[fixed:fixed + 1, :] + v1
                fl = fixed * PEER_TOPK + r_iota
                ei = i0[fixed:fixed + 1, :] * N_SUBKEYS + i1
            else:
                c = v0 + v1[fixed:fixed + 1, :]
                fl = r_iota * PEER_TOPK + fixed
                ei = i0 * N_SUBKEYS + i1[fixed:fixed + 1, :]
            valid = (r_iota >= lo) & (r_iota < hi)
            cands.append(jnp.where(valid, c, -jnp.inf))
            flats.append(fl)
            eids.append(ei)
        big = PEER_TOPK * PEER_TOPK
        for r in range(PEER_TOPK):
            m = functools.reduce(jnp.maximum, cands)
            m = jnp.max(m, axis=0, keepdims=True)
            fsel = functools.reduce(jnp.minimum, [jnp.where(c == m, fl, big) for c, fl in zip(cands, flats)])
            fsel = jnp.min(fsel, axis=0, keepdims=True)
            hit = [fl == fsel for fl in flats]
            eid = functools.reduce(jnp.maximum, [jnp.where(hh, ei, -1) for hh, ei in zip(hit, eids)])
            eid = jnp.max(eid, axis=0, keepdims=True)
            cands = [jnp.where(hh, -jnp.inf, c) for hh, c in zip(hit, cands)]
            best_ref[pl.ds(r, 1), :] = m
            idt_ref[pl.ds(h * PEER_TOPK + r, 1), :] = eid
        b = best_ref[...]
        e = jnp.exp(b - jnp.max(b, axis=0, keepdims=True))
        gt_ref[h * PEER_TOPK:(h + 1) * PEER_TOPK, :] = e / jnp.sum(e, axis=0, keepdims=True)
    ids_ref[...] = idt_ref[...].T
    gate_ref[...] = gt_ref[...].T


def _route(qp, keys_bf):
    n, qw = qp.shape
    t = T_PROJ
    return pl.pallas_call(
        _route_kernel,
        grid=(n // t,),
        in_specs=[pl.BlockSpec((t, qw), lambda i: (i, 0)), _const_spec(keys_bf.shape)],
        out_specs=[pl.BlockSpec((t, PEER_SLOTS), lambda i: (i, 0))] * 2,
        out_shape=[jax.ShapeDtypeStruct((n, PEER_SLOTS), i32), jax.ShapeDtypeStruct((n, PEER_SLOTS), f32)],
        scratch_shapes=[pltpu.VMEM((2 * PEER_HEADS, PEER_TOPK, t), f32), pltpu.VMEM((2 * PEER_HEADS, PEER_TOPK, t), i32),
                        pltpu.VMEM((PEER_TOPK, t), f32), pltpu.VMEM((PEER_SLOTS, t), i32),
                        pltpu.VMEM((PEER_SLOTS, t), f32)],
        compiler_params=_cparams(("parallel",)),
        name="route",
    )(qp, keys_bf)


def _pack_table(tbl):
    bits = lax.bitcast_convert_type(tbl.astype(bf16), jnp.uint16).astype(jnp.uint32)
    half = D_MODEL // 2
    words = bits[:, :half] | (bits[:, half:] << 16)
    return lax.bitcast_convert_type(words, i32).reshape(tbl.shape[0], ROW_WORDS, LANES)


def _unpack(words):
    lo = pltpu.bitcast(words << 16, f32)
    hi = pltpu.bitcast(words & jnp.int32(-65536), f32)
    return lo, hi


def _split2(x):
    hi = x.astype(bf16)
    lo = (x - hi.astype(f32)).astype(bf16)
    return hi, lo


def _gather_chunk(ids_ref, tbl_ref, tile_ref, c, base):
    e = None
    for k in range(GATHER_CHUNK):
        e = ids_ref[base + k]
        tile_ref[c, k * ROW_WORDS:(k + 1) * ROW_WORDS, :] = tbl_ref[e]
    return e


def _two_token_pipeline(ids_ref, tbl_ref, tile_a, tile_b, consume, init):
    tile_b[...] = jnp.zeros(tile_b.shape, i32)
    last = T_PEER - 1

    def half_step(t_gather, gather_tile, t_consume, consume_tile, dep):
        base = jnp.minimum(t_gather, last) * PEER_SLOTS
        tc = jnp.clip(t_consume, 0, last)
        acc = init
        for c in range(N_CHUNKS):
            dep = _gather_chunk(ids_ref, tbl_ref, gather_tile, c, base + c * GATHER_CHUNK + (dep >> 31))
            acc = consume(consume_tile, c, tc, acc)
        return dep

    def pairs_step(i, dep):
        for p in range(PAIRS_PER_STEP):
            even = 2 * (PAIRS_PER_STEP * i + p)
            dep = half_step(even, tile_a, even - 1, tile_b, dep)
            dep = half_step(even + 1, tile_b, even, tile_a, dep)
        return dep

    lax.fori_loop(0, T_PEER // (2 * PAIRS_PER_STEP) + 1, pairs_step, jnp.int32(0))


def _tile_scratch():
    return pltpu.VMEM((N_CHUNKS, GATHER_CHUNK * ROW_WORDS, LANES), i32)


def _peer_u_kernel(ids_ref, xr_ref, g_ref, tbl_ref, w_ref, tile_a, tile_b):
    crow = GATHER_CHUNK * ROW_WORDS
    half = crow // 2
    kk = lax.broadcasted_iota(i32, (2 * LANES, LANES), 0)
    nn = lax.broadcasted_iota(i32, (2 * LANES, LANES), 1)
    summer = ((kk < LANES) == (nn % GATHER_CHUNK < GATHER_CHUNK // 2)).astype(bf16)
    rr = lax.broadcasted_iota(i32, (half, LANES), 0)
    ll = lax.broadcasted_iota(i32, (half, LANES), 1)
    own_row = ll % (GATHER_CHUNK // 2) == rr // ROW_WORDS
    lane_chunk = ll // GATHER_CHUNK

    def consume(tile, c, t, acc):
        xt = xr_ref[pl.ds(pl.multiple_of(t * SUBLANES, SUBLANES), SUBLANES), :]
        xlo = xt[0:ROW_WORDS]
        xhi = xt[ROW_WORDS:2 * ROW_WORDS]
        x2lo = pltpu.repeat(jnp.concatenate([xlo, xlo], axis=0), crow // SUBLANES, axis=0)
        x2hi = pltpu.repeat(jnp.concatenate([xhi, xhi], axis=0), crow // SUBLANES, axis=0)
        lo, hi = _unpack(tile[c])
        p = lo * x2lo + hi * x2hi
        p_hi, p_lo = _split2(jnp.concatenate([p[0:half], p[half:crow]], axis=1))
        rs = jnp.dot(p_hi, summer, preferred_element_type=f32) + jnp.dot(p_lo, summer, preferred_element_type=f32)
        acc = acc + jnp.sum(jnp.where(own_row & (lane_chunk == c), rs, 0.0), axis=0, keepdims=True)
        if c == N_CHUNKS - 1:
            gelu = 0.5 * acc * (1.0 + lax.erf(acc * (2.0 ** -0.5)))
            w_ref[t] = g_ref[t] * gelu
        return acc

    _two_token_pipeline(ids_ref, tbl_ref, tile_a, tile_b, consume, jnp.zeros((1, PEER_SLOTS), f32))


def _peer_u(ids_flat, h2r, gate3, tbl, tok_off, n_tok):
    t = T_PEER
    b0 = tok_off // t
    return pl.pallas_call(
        _peer_u_kernel,
        grid=(n_tok // t,),
        in_specs=[pl.BlockSpec((t * PEER_SLOTS,), lambda i: (i + b0,), memory_space=pltpu.SMEM),
                  pl.BlockSpec((t * SUBLANES, LANES), lambda i: (i + b0, 0)),
                  pl.BlockSpec((t, 1, PEER_SLOTS), lambda i: (i + b0, 0, 0)),
                  _const_spec(tbl.shape)],
        out_specs=pl.BlockSpec((t, 1, PEER_SLOTS), lambda i: (i, 0, 0)),
        out_shape=jax.ShapeDtypeStruct((n_tok, 1, PEER_SLOTS), f32),
        scratch_shapes=[_tile_scratch(), _tile_scratch()],
        compiler_params=_cparams(("parallel",)),
        name="peer_u",
    )(ids_flat, h2r, gate3, tbl)


def _peer_v_sc(ids, w, tbl_words, tok_off):
    n_tok = w.shape[0]
    tpw = n_tok // SC_WORKERS
    lanes = SC_LANES
    mesh = plsc.VectorSubcoreMesh(core_axis_name="c", subcore_axis_name="s")

    @functools.partial(
        pl.kernel, mesh=mesh,
        out_type=jax.ShapeDtypeStruct((n_tok, D_MODEL), f32),
        scratch_types=[
            pltpu.VMEM((SC_HALF,), i32), pltpu.VMEM((SC_HALF,), i32),
            pltpu.VMEM((SC_HALF, ROW_WORDS_FLAT), i32), pltpu.VMEM((SC_HALF, ROW_WORDS_FLAT), i32),
            pltpu.VMEM((PEER_SLOTS,), f32),
            pltpu.VMEM((D_MODEL,), f32),
            pltpu.SemaphoreType.DMA, pltpu.SemaphoreType.DMA,
        ],
        compiler_params=dataclasses.replace(pltpu.CompilerParams(), needs_layout_passes=False),
        name="peer_v_sc",
    )
    def run(ids_hbm, w_hbm, tbl_hbm, out_hbm, idx_a, idx_b, rows_a, rows_b, w_v, out_v, sem_a, sem_b):
        wid = lax.axis_index("s") * SC_CORES + lax.axis_index("c")
        base = wid * tpw

        def accumulate(rows, w_off):
            for db in range(ROW_WORDS_FLAT // SC_DIM_BLOCK):
                def group_body(g, carry):
                    accs = []
                    for wc in range(SC_DIM_BLOCK // lanes):
                        accs.append(out_v[pl.ds(db * SC_DIM_BLOCK + wc * lanes, lanes)])
                        accs.append(out_v[pl.ds(ROW_WORDS_FLAT + db * SC_DIM_BLOCK + wc * lanes, lanes)])
                    j0 = g * SC_ROW_GROUP
                    wchunk = w_v[pl.ds(pl.multiple_of((w_off + j0) // lanes * lanes, lanes), lanes)]
                    sub = (w_off + j0) % lanes
                    for r in range(SC_ROW_GROUP):
                        wj = jnp.take(wchunk, jnp.full((lanes,), sub + r, i32))
                        for wc in range(SC_DIM_BLOCK // lanes):
                            word = rows[j0 + r, pl.ds(db * SC_DIM_BLOCK + wc * lanes, lanes)]
                            lo = lax.bitcast_convert_type(word << 16, f32)
                            hi = lax.bitcast_convert_type(word & jnp.int32(-65536), f32)
                            accs[2 * wc] = accs[2 * wc] + wj * lo
                            accs[2 * wc + 1] = accs[2 * wc + 1] + wj * hi
                    for wc in range(SC_DIM_BLOCK // lanes):
                        out_v[pl.ds(db * SC_DIM_BLOCK + wc * lanes, lanes)] = accs[2 * wc]
                        out_v[pl.ds(ROW_WORDS_FLAT + db * SC_DIM_BLOCK + wc * lanes, lanes)] = accs[2 * wc + 1]
                    return carry
                lax.fori_loop(0, SC_HALF // SC_ROW_GROUP, group_body, 0)

        def token(i, carry):
            t = base + i
            pltpu.sync_copy(ids_hbm.at[tok_off + t, pl.ds(0, SC_HALF)], idx_a)
            pltpu.sync_copy(ids_hbm.at[tok_off + t, pl.ds(SC_HALF, SC_HALF)], idx_b)
            copy_a = pltpu.async_copy(tbl_hbm.at[idx_a], rows_a, sem_a)
            copy_b = pltpu.async_copy(tbl_hbm.at[idx_b], rows_b, sem_b)
            pltpu.sync_copy(w_hbm.at[t], w_v)
            for q in range(D_MODEL // lanes):
                out_v[pl.ds(q * lanes, lanes)] = jnp.zeros((lanes,), f32)
            copy_a.wait()
            accumulate(rows_a, 0)
            copy_b.wait()
            accumulate(rows_b, SC_HALF)
            pltpu.sync_copy(out_v, out_hbm.at[t])
            return carry

        lax.fori_loop(0, tpw, token, 0)

    return run(ids, w, tbl_words)


def _final_kernel(x1_ref, p_ref, g_ref, y_ref):
    x = x1_ref[...] + p_ref[...]
    ms = jnp.mean(x * x, axis=-1, keepdims=True)
    y_ref[...] = x * lax.rsqrt(ms + EPS) * g_ref[...]


def _final(x1, peer_out, final_g, tok_off):
    n = peer_out.shape[0]
    t = T_PROJ
    b0 = tok_off // t
    row = pl.BlockSpec((t, D_MODEL), lambda i: (i, 0))
    return pl.pallas_call(
        _final_kernel,
        grid=(n // t,),
        in_specs=[pl.BlockSpec((t, D_MODEL), lambda i: (i + b0, 0)), row, _const_spec((1, D_MODEL))],
        out_specs=row,
        out_shape=jax.ShapeDtypeStruct((n, D_MODEL), f32),
        compiler_params=_cparams(("parallel",)),
        name="final_norm",
    )(x1, peer_out, final_g)


def _rotary_tables(seq):
    half = ROT_DIM // 2
    inv = ROPE_THETA ** (-jnp.arange(half, dtype=f32) * 2.0 / ROT_DIM)
    ang = jnp.arange(seq, dtype=jnp.int32).astype(f32)[:, None] * inv[None, :]
    cos = jnp.cos(ang)
    sin = jnp.sin(ang)
    pad = HEAD_DIM - ROT_DIM
    one = jnp.ones((seq, pad), f32)
    zero = jnp.zeros((seq, pad), f32)
    zh = jnp.zeros((seq, half), f32)
    cos_h = jnp.concatenate([cos, cos, one], axis=1)
    sa_h = jnp.concatenate([-sin, zh, zero], axis=1)
    sb_h = jnp.concatenate([zh, sin, zero], axis=1)
    rep = LANES // HEAD_DIM
    return tuple(jnp.tile(t, (1, rep)) for t in (cos_h, sa_h, sb_h))


def _layer(x, params, tables):
    (norm1_g, w_in_bf, b_gate, merge_w, keys_bf, u_tbl, v_tbl, final_g) = params
    batch, seq, _ = x.shape
    n = batch * seq
    x2 = x.reshape(n, D_MODEL)
    cos_t, sa_t, sb_t = tables
    q, k, v, u, gates = _inproj(x2, seq, norm1_g, w_in_bf, b_gate, cos_t, sa_t, sb_t)
    os_, ls_ = [], []
    for g in range(N_GROUPS):
        o, lse = _attention_group(q[g], k[g], v[g], batch, seq, g)
        os_.append(o)
        ls_.append(lse)
    x1, h2r, qp = _merge(x2, seq, u, gates, os_, ls_, merge_w)
    ids, gate = _route(qp, keys_bf)
    ids_flat = ids.reshape(n * PEER_SLOTS)
    gate3 = gate.reshape(n, 1, PEER_SLOTS)
    ys = []
    off = 0
    for share in PEER_CHUNK_SHARES:
        n_chunk = n * share // sum(PEER_CHUNK_SHARES)
        w3 = _peer_u(ids_flat, h2r, gate3, u_tbl, off, n_chunk)
        pv = _peer_v_sc(ids, w3.reshape(n_chunk, PEER_SLOTS), v_tbl, off)
        ys.append(_final(x1, pv, final_g, off))
        off += n_chunk
    return ys


def kernel(x_prompt, x_sample, norm1_g, w_in, b_gate, w_attn_up, conv_dw_w, conv_dw_b, conv_ln_g, conv_ln_b,
           conv_pw_w, conv_pw_b, w_out, norm2_g, peer_wq, peer_keys, peer_u, peer_v, final_g):
    assert w_in.shape[0] == 1, "one encoder layer followed by the final norm"
    row = lambda a: a.reshape(1, -1)
    tables = _rotary_tables(max(x_prompt.shape[1], x_sample.shape[1]))
    l = 0
    merge_w = (conv_dw_w[l], row(conv_dw_b[l]), row(conv_ln_g[l]), row(conv_ln_b[l]),
               conv_pw_w[l].astype(bf16), row(conv_pw_b[l]), w_attn_up[l].astype(bf16),
               w_out[l].astype(bf16), row(norm2_g[l]), peer_wq[l].astype(bf16))
    keys_bf = peer_keys[l].astype(bf16).reshape(2 * PEER_HEADS, N_SUBKEYS, -1)
    params = (row(norm1_g[l]), w_in[l].astype(bf16), row(b_gate[l]), merge_w, keys_bf,
              _pack_table(peer_u[l]), _pack_table(peer_v[l]).reshape(N_EXPERTS, ROW_WORDS_FLAT), row(final_g))

    def group(x):
        step = x.shape[0] // BATCH_SUBGROUPS
        chunks = []
        for b in range(0, x.shape[0], step):
            chunks += _layer(x[b:b + step], params, tables)
        return jnp.concatenate(chunks, axis=0).reshape(x.shape)

    return (group(x_prompt), group(x_sample))
```

```python
import dataclasses
import functools
import math

import numpy as np
import jax
import jax.numpy as jnp
from jax import lax
from jax.experimental import pallas as pl
from jax.experimental.pallas import tpu as pltpu
from jax.experimental.pallas import tpu_sc as plsc

f32 = jnp.float32
bf16 = jnp.bfloat16
i32 = jnp.int32

D_MODEL = 1024
HEAD_DIM = 64
N_GROUPS = 3
HEADS_PER_GROUP = 4
GROUP_W = HEADS_PER_GROUP * HEAD_DIM
ATTN_W = N_GROUPS * GROUP_W
WINDOWS = (128, 512, 2048)
DILATIONS = (1, 4, 16)
HALF_WIN = 64
ROT_DIM = HEAD_DIM // 4
ROPE_THETA = 500000.0
NEG_INF = -1e30
CONV_W = 512
CONV_K = 31
CONV_HALO = 16
PEER_HEADS = 8
N_SUBKEYS = 128
N_EXPERTS = N_SUBKEYS * N_SUBKEYS
PEER_TOPK = 16
PEER_SLOTS = PEER_HEADS * PEER_TOPK
EPS = 1e-6

LANES = 128
SUBLANES = 8
ROW_WORDS = D_MODEL // 2 // LANES
VMEM_LIMIT = 56 * 1024 * 1024

T_PROJ = 256
T_PEER = 256
PAIRS_PER_STEP = 8
GATHER_CHUNK = 32
N_CHUNKS = (8 * 16) // GATHER_CHUNK
BATCH_SUBGROUPS = 2
PEER_CHUNK_SHARES = (4, 4, 4, 2, 1, 1)

SC_CORES = 2
SC_SUBCORES = 16
SC_LANES = 16
SC_WORKERS = SC_CORES * SC_SUBCORES
ROW_WORDS_FLAT = D_MODEL // 2
SC_HALF = PEER_SLOTS // 2
SC_DIM_BLOCK = 128
SC_ROW_GROUP = 8
Q_SUB = 128


def _cparams(sem):
    return pltpu.CompilerParams(dimension_semantics=sem, vmem_limit_bytes=VMEM_LIMIT)


def _const_spec(shape):
    nd = len(shape)
    return pl.BlockSpec(shape, lambda *_: (0,) * nd, pipeline_mode=pl.Buffered(1))


def _inproj_kernel(x_ref, g_ref, w_ref, bg_ref, cos_ref, sa_ref, sb_ref, *refs):
    qkv_refs = (refs[0:N_GROUPS], refs[N_GROUPS:2 * N_GROUPS], refs[2 * N_GROUPS:3 * N_GROUPS])
    u_ref, gate_ref, stage_ref = refs[3 * N_GROUPS:]
    x = x_ref[...]
    ms = jnp.mean(x * x, axis=-1, keepdims=True)
    h = (x * lax.rsqrt(ms + EPS) * g_ref[...]).astype(bf16)

    def proj(lo, hi):
        return jnp.dot(h, w_ref[:, lo:hi], preferred_element_type=f32)

    cos = cos_ref[...]
    sa = sa_ref[...]
    sb = sb_ref[...]

    def rotary(tc, scale):
        r = tc * cos + pltpu.roll(tc, LANES - ROT_DIM // 2, 1) * sa + pltpu.roll(tc, ROT_DIM // 2, 1) * sb
        return r * scale

    def emit(t, out_refs, fn):
        for c in range(ATTN_W // LANES):
            g, half = divmod(c, GROUP_W // LANES)
            d = DILATIONS[g]
            chunk = fn(t[:, c * LANES:(c + 1) * LANES])
            if d == 1:
                out_refs[g][:, half * LANES:(half + 1) * LANES] = chunk.astype(bf16)
                continue
            stage_ref[...] = chunk
            for r in range(d):
                col = r * GROUP_W + half * LANES
                out_refs[g][:, col:col + LANES] = stage_ref[pl.ds(r, T_PROJ // d, stride=d), :].astype(bf16)

    emit(proj(0, ATTN_W), qkv_refs[0], lambda tc: rotary(tc, HEAD_DIM ** -0.5))
    emit(proj(ATTN_W, 2 * ATTN_W), qkv_refs[1], lambda tc: rotary(tc, 1.0))
    emit(proj(2 * ATTN_W, 3 * ATTN_W), qkv_refs[2], lambda tc: tc)
    c0 = 3 * ATTN_W
    a = proj(c0, c0 + CONV_W)
    b = proj(c0 + CONV_W, c0 + 2 * CONV_W)
    u_ref[...] = a * jax.nn.sigmoid(b)
    gates = proj(c0 + 2 * CONV_W, c0 + 2 * CONV_W + 2 * D_MODEL) + bg_ref[...]
    gate_ref[...] = jax.nn.sigmoid(gates).astype(bf16)


def _inproj(x2, seq, norm1_g, w_in_bf, b_gate, cos_t, sa_t, sb_t):
    n = x2.shape[0]
    nsb = seq // T_PROJ
    in_cols = w_in_bf.shape[1]
    row = lambda w: pl.BlockSpec((T_PROJ, w), lambda i: (i, 0))
    pos = pl.BlockSpec((T_PROJ, LANES), lambda i: (i % nsb, 0))
    grp_specs = [pl.BlockSpec((T_PROJ // d, d * GROUP_W), lambda i: (i, 0)) for d in DILATIONS]
    grp_shapes = [jax.ShapeDtypeStruct((n // d, d * GROUP_W), bf16) for d in DILATIONS]
    outs = pl.pallas_call(
        _inproj_kernel,
        grid=(n // T_PROJ,),
        in_specs=[row(D_MODEL), _const_spec((1, D_MODEL)), _const_spec((D_MODEL, in_cols)),
                  _const_spec((1, 2 * D_MODEL)), pos, pos, pos],
        out_specs=grp_specs * 3 + [row(CONV_W), row(2 * D_MODEL)],
        out_shape=grp_shapes * 3
        + [jax.ShapeDtypeStruct((n, CONV_W), f32), jax.ShapeDtypeStruct((n, 2 * D_MODEL), bf16)],
        scratch_shapes=[pltpu.VMEM((T_PROJ, LANES), f32)],
        compiler_params=_cparams(("parallel",)),
        name="inproj",
    )(x2, norm1_g, w_in_bf, b_gate, cos_t, sa_t, sb_t)
    q, k, v = outs[0:N_GROUPS], outs[N_GROUPS:2 * N_GROUPS], outs[2 * N_GROUPS:3 * N_GROUPS]
    return q, k, v, outs[3 * N_GROUPS], outs[3 * N_GROUPS + 1]


def _attn_kernel(q_ref, kp_ref, kc_ref, kn_ref, vp_ref, vc_ref, vn_ref, o_ref, lse_ref,
                 kw_ref, vw_ref, *, tq, n_rows):
    i = pl.program_id(2)
    kw_ref[0:HALF_WIN] = kp_ref[0]
    kw_ref[HALF_WIN:HALF_WIN + tq] = kc_ref[0]
    kw_ref[HALF_WIN + tq:2 * HALF_WIN + tq] = kn_ref[0]
    vw_ref[0:HALF_WIN] = vp_ref[0]
    vw_ref[HALF_WIN:HALF_WIN + tq] = vc_ref[0]
    vw_ref[HALF_WIN + tq:2 * HALF_WIN + tq] = vn_ref[0]

    win = Q_SUB + 2 * HALF_WIN
    qi = lax.broadcasted_iota(i32, (Q_SUB, win), 0)
    kj = lax.broadcasted_iota(i32, (Q_SUB, win), 1)
    band = (kj - qi >= 0) & (kj - qi <= 2 * HALF_WIN)
    head_of_lane = lax.broadcasted_iota(i32, (1, GROUP_W), 1) // HEAD_DIM

    for s in range(tq // Q_SUB):
        qs = q_ref[0, s * Q_SUB:(s + 1) * Q_SUB, :]
        kwin = kw_ref[s * Q_SUB:s * Q_SUB + win, :]
        vwin = vw_ref[s * Q_SUB:s * Q_SUB + win, :]
        key_row = i * tq + (s * Q_SUB - HALF_WIN) + kj
        ok = band & (key_row >= 0) & (key_row < n_rows)
        o_acc = jnp.zeros((Q_SUB, GROUP_W), f32)
        l_acc = jnp.zeros((Q_SUB, GROUP_W), f32)
        for h in range(HEADS_PER_GROUP):
            hm = head_of_lane == h
            qh = jnp.where(hm, qs, jnp.zeros_like(qs))
            sc = lax.dot_general(qh, kwin, (((1,), (1,)), ((), ())), preferred_element_type=f32)
            sc = jnp.where(ok, sc, NEG_INF)
            m = jnp.max(sc, axis=-1, keepdims=True)
            p = jnp.exp(sc - m)
            den = jnp.sum(p, axis=-1, keepdims=True)
            pv = jnp.dot(p.astype(bf16), vwin, preferred_element_type=f32)
            o_acc = jnp.where(hm, pv / den, o_acc)
            l_acc = jnp.where(hm, m + jnp.log(den), l_acc)
        o_ref[0, s * Q_SUB:(s + 1) * Q_SUB, :] = o_acc
        lse_ref[0, s * Q_SUB:(s + 1) * Q_SUB, :] = l_acc


def _attention_group(q, k, v, batch, seq, g):
    d = DILATIONS[g]
    n_rows = seq // d
    tq = min(512, n_rows)
    nblk = n_rows // tq
    hb = tq // HALF_WIN
    n_halo_blocks = n_rows // HALF_WIN
    view = lambda t: t.reshape(batch, n_rows, d * GROUP_W)
    cur = pl.BlockSpec((1, tq, GROUP_W), lambda b, r, i: (b, i, r))
    prev = pl.BlockSpec((1, HALF_WIN, GROUP_W), lambda b, r, i: (b, jnp.maximum(i * hb - 1, 0), r))
    nxt = pl.BlockSpec((1, HALF_WIN, GROUP_W),
                       lambda b, r, i: (b, jnp.minimum((i + 1) * hb, n_halo_blocks - 1), r))
    o, lse = pl.pallas_call(
        functools.partial(_attn_kernel, tq=tq, n_rows=n_rows),
        grid=(batch, d, nblk),
        in_specs=[cur, prev, cur, nxt, prev, cur, nxt],
        out_specs=[cur, cur],
        out_shape=[jax.ShapeDtypeStruct((batch, n_rows, d * GROUP_W), f32)] * 2,
        scratch_shapes=[pltpu.VMEM((tq + 2 * HALF_WIN, GROUP_W), bf16)] * 2,
        compiler_params=_cparams(("parallel", "parallel", "parallel")),
        name=f"attn_g{g}",
    )(view(q), view(k), view(k), view(k), view(v), view(v), view(v))
    return o.reshape(batch * seq, GROUP_W), lse.reshape(batch * seq, GROUP_W)


def _merge_kernel(x_ref, up_ref, uc_ref, un_ref, gate_ref,
                  o0_ref, o1_ref, o2_ref, l0_ref, l1_ref, l2_ref,
                  dww_ref, dwb_ref, lng_ref, lnb_ref, pww_ref, pwb_ref, wup_ref, wout_ref, n2g_ref, wq_ref,
                  x1_ref, h2_ref, qp_ref, ue_ref, cv_ref, *, nsb):
    i = pl.program_id(0)
    t = T_PROJ
    first = (i % nsb) == 0
    last = (i % nsb) == nsb - 1
    ue_ref[0:CONV_HALO] = jnp.where(first, 0.0, up_ref[...])
    ue_ref[CONV_HALO:CONV_HALO + t] = uc_ref[...]
    ue_ref[CONV_HALO + t:2 * CONV_HALO + t] = jnp.where(last, 0.0, un_ref[...])

    rc = 32
    off = CONV_HALO - CONV_K // 2
    for r0 in range(0, t, rc):
        acc = jnp.zeros((rc, CONV_W), f32)
        for j in range(CONV_K):
            acc = acc + ue_ref[r0 + off + j:r0 + off + j + rc, :] * dww_ref[j:j + 1, :]
        cv_ref[r0:r0 + rc, :] = acc
    c = cv_ref[...] + dwb_ref[...]
    mu = jnp.mean(c, axis=-1, keepdims=True)
    cc = c - mu
    var = jnp.mean(cc * cc, axis=-1, keepdims=True)
    un = cc * lax.rsqrt(var + EPS) * lng_ref[...] + lnb_ref[...]
    sw = un * jax.nn.sigmoid(un)
    conv = jnp.dot(sw.astype(bf16), pww_ref[...], preferred_element_type=f32) + pwb_ref[...]

    l0 = l0_ref[...]
    l1 = l1_ref[...]
    l2 = l2_ref[...]
    lm = jnp.maximum(jnp.maximum(l0, l1), l2)
    e0 = jnp.exp(l0 - lm)
    e1 = jnp.exp(l1 - lm)
    e2 = jnp.exp(l2 - lm)
    comb = (e0 * o0_ref[...] + e1 * o1_ref[...] + e2 * o2_ref[...]) / (e0 + e1 + e2)
    attn = jnp.dot(comb.astype(bf16), wup_ref[...], preferred_element_type=f32)

    g_attn = gate_ref[:, 0:D_MODEL].astype(f32)
    g_conv = gate_ref[:, D_MODEL:2 * D_MODEL].astype(f32)
    mixed = g_attn * attn + g_conv * conv
    x1 = x_ref[...] + jnp.dot(mixed.astype(bf16), wout_ref[...], preferred_element_type=f32)
    x1_ref[...] = x1
    ms = jnp.mean(x1 * x1, axis=-1, keepdims=True)
    h2 = x1 * lax.rsqrt(ms + EPS) * n2g_ref[...]
    for c in range(D_MODEL // LANES):
        h2_ref[pl.ds(c, t, stride=SUBLANES), :] = h2[:, c * LANES:(c + 1) * LANES]
    qp_ref[...] = jnp.dot(h2.astype(bf16), wq_ref[...], preferred_element_type=f32).astype(bf16)


def _merge(x2, seq, u, gates, os_, ls_, weights):
    n = x2.shape[0]
    t = T_PROJ
    nsb = seq // t
    hb = t // CONV_HALO
    nhalo = n // CONV_HALO
    row = lambda w: pl.BlockSpec((t, w), lambda i: (i, 0))
    prev = pl.BlockSpec((CONV_HALO, CONV_W), lambda i: (jnp.maximum(i * hb - 1, 0), 0))
    nxt = pl.BlockSpec((CONV_HALO, CONV_W), lambda i: (jnp.minimum((i + 1) * hb, nhalo - 1), 0))
    wspecs = [_const_spec(w.shape) for w in weights]
    qw = weights[-1].shape[1]
    return pl.pallas_call(
        functools.partial(_merge_kernel, nsb=nsb),
        grid=(n // t,),
        in_specs=[row(D_MODEL), prev, row(CONV_W), nxt, row(2 * D_MODEL)] + [row(GROUP_W)] * 6 + wspecs,
        out_specs=[row(D_MODEL), pl.BlockSpec((t * SUBLANES, LANES), lambda i: (i, 0)), row(qw)],
        out_shape=[jax.ShapeDtypeStruct((n, D_MODEL), f32), jax.ShapeDtypeStruct((n * SUBLANES, LANES), f32),
                   jax.ShapeDtypeStruct((n, qw), bf16)],
        scratch_shapes=[pltpu.VMEM((t + 2 * CONV_HALO, CONV_W), f32), pltpu.VMEM((t, CONV_W), f32)],
        compiler_params=_cparams(("parallel",)),
        name="merge",
    )(x2, u, u, u, gates, *os_, *ls_, *weights)


def _candidate_slabs():
    slabs = [("row", 0, 0, 16), ("row", 1, 0, 8)]
    for j in range(PEER_TOPK):
        hi = PEER_TOPK // (j + 1)
        if hi > 2:
            slabs.append(("col", j, 2, hi))
    return slabs


def _route_kernel(qp_ref, keys_ref, ids_ref, gate_ref, val_ref, idx_ref, best_ref, idt_ref, gt_ref):
    t = T_PROJ
    k_iota = lax.broadcasted_iota(i32, (N_SUBKEYS, t), 0)
    for hc in range(2 * PEER_HEADS):
        q = qp_ref[:, hc * N_SUBKEYS:(hc + 1) * N_SUBKEYS]
        s = lax.dot_general(keys_ref[hc], q, (((1,), (1,)), ((), ())), preferred_element_type=f32)
        for r in range(PEER_TOPK):
            m = jnp.max(s, axis=0, keepdims=True)
            am = jnp.min(jnp.where(s == m, k_iota, N_SUBKEYS), axis=0, keepdims=True)
            s = jnp.where(k_iota == am, -jnp.inf, s)
            val_ref[hc, pl.ds(r, 1), :] = m
            idx_ref[hc, pl.ds(r, 1), :] = am

    r_iota = lax.broadcasted_iota(i32, (PEER_TOPK, t), 0)
    slabs = _candidate_slabs()
    for h in range(PEER_HEADS):
        v0 = val_ref[2 * h]
        v1 = val_ref[2 * h + 1]
        i0 = idx_ref[2 * h]
        i1 = idx_ref[2 * h + 1]
        cands, flats, eids = [], [], []
        for kind, fixed, lo, hi in slabs:
            if kind == "row":
                c = v0[fixed:fixed + 1, :] + v1
                fl = fixed * PEER_TOPK + r_iota
                ei = i0[fixed:fixed + 1, :] * N_SUBKEYS + i1
            else:
                c = v0 + v1[fixed:fixed + 1, :]
                fl = r_iota * PEER_TOPK + fixed
                ei = i0 * N_SUBKEYS + i1[fixed:fixed + 1, :]
            valid = (r_iota >= lo) & (r_iota < hi)
            cands.append(jnp.where(valid, c, -jnp.inf))
            flats.append(fl)
            eids.append(ei)
        big = PEER_TOPK * PEER_TOPK
        for r in range(PEER_TOPK):
            m = functools.reduce(jnp.maximum, cands)
            m = jnp.max(m, axis=0, keepdims=True)
            fsel = functools.reduce(jnp.minimum, [jnp.where(c == m, fl, big) for c, fl in zip(cands, flats)])
            fsel = jnp.min(fsel, axis=0, keepdims=True)
            hit = [fl == fsel for fl in flats]
            eid = functools.reduce(jnp.maximum, [jnp.where(hh, ei, -1) for hh, ei in zip(hit, eids)])
            eid = jnp.max(eid, axis=0, keepdims=True)
            cands = [jnp.where(hh, -jnp.inf, c) for hh, c in zip(hit, cands)]
            best_ref[pl.ds(r, 1), :] = m
            idt_ref[pl.ds(h * PEER_TOPK + r, 1), :] = eid
        b = best_ref[...]
        e = jnp.exp(b - jnp.max(b, axis=0, keepdims=True))
        gt_ref[h * PEER_TOPK:(h + 1) * PEER_TOPK, :] = e / jnp.sum(e, axis=0, keepdims=True)
    ids_ref[...] = idt_ref[...].T
    gate_ref[...] = gt_ref[...].T


def _route(qp, keys_bf):
    n, qw = qp.shape
    t = T_PROJ
    return pl.pallas_call(
        _route_kernel,
        grid=(n // t,),
        in_specs=[pl.BlockSpec((t, qw), lambda i: (i, 0)), _const_spec(keys_bf.shape)],
        out_specs=[pl.BlockSpec((t, PEER_SLOTS), lambda i: (i, 0))] * 2,
        out_shape=[jax.ShapeDtypeStruct((n, PEER_SLOTS), i32), jax.ShapeDtypeStruct((n, PEER_SLOTS), f32)],
        scratch_shapes=[pltpu.VMEM((2 * PEER_HEADS, PEER_TOPK, t), f32), pltpu.VMEM((2 * PEER_HEADS, PEER_TOPK, t), i32),
                        pltpu.VMEM((PEER_TOPK, t), f32), pltpu.VMEM((PEER_SLOTS, t), i32),
                        pltpu.VMEM((PEER_SLOTS, t), f32)],
        compiler_params=_cparams(("parallel",)),
        name="route",
    )(qp, keys_bf)


def _pack_table(tbl):
    bits = lax.bitcast_convert_type(tbl.astype(bf16), jnp.uint16).astype(jnp.uint32)
    half = D_MODEL // 2
    words = bits[:, :half] | (bits[:, half:] << 16)
    return lax.bitcast_convert_type(words, i32).reshape(tbl.shape[0], ROW_WORDS, LANES)


def _unpack(words):
    lo = pltpu.bitcast(words << 16, f32)
    hi = pltpu.bitcast(words & jnp.int32(-65536), f32)
    return lo, hi


def _split2(x):
    hi = x.astype(bf16)
    lo = (x - hi.astype(f32)).astype(bf16)
    return hi, lo


def _gather_chunk(ids_ref, tbl_ref, tile_ref, c, base):
    e = None
    for k in range(GATHER_CHUNK):
        e = ids_ref[base + k]
        tile_ref[c, k * ROW_WORDS:(k + 1) * ROW_WORDS, :] = tbl_ref[e]
    return e


def _two_token_pipeline(ids_ref, tbl_ref, tile_a, tile_b, consume, init):
    tile_b[...] = jnp.zeros(tile_b.shape, i32)
    last = T_PEER - 1

    def half_step(t_gather, gather_tile, t_consume, consume_tile, dep):
        base = jnp.minimum(t_gather, last) * PEER_SLOTS
        tc = jnp.clip(t_consume, 0, last)
        acc = init
        for c in range(N_CHUNKS):
            dep = _gather_chunk(ids_ref, tbl_ref, gather_tile, c, base + c * GATHER_CHUNK + (dep >> 31))
            acc = consume(consume_tile, c, tc, acc)
        return dep

    def pairs_step(i, dep):
        for p in range(PAIRS_PER_STEP):
            even = 2 * (PAIRS_PER_STEP * i + p)
            dep = half_step(even, tile_a, even - 1, tile_b, dep)
            dep = half_step(even + 1, tile_b, even, tile_a, dep)
        return dep

    lax.fori_loop(0, T_PEER // (2 * PAIRS_PER_STEP) + 1, pairs_step, jnp.int32(0))


def _tile_scratch():
    return pltpu.VMEM((N_CHUNKS, GATHER_CHUNK * ROW_WORDS, LANES), i32)


def _peer_u_kernel(ids_ref, xr_ref, g_ref, tbl_ref, w_ref, tile_a, tile_b):
    crow = GATHER_CHUNK * ROW_WORDS
    half = crow // 2
    kk = lax.broadcasted_iota(i32, (2 * LANES, LANES), 0)
    nn = lax.broadcasted_iota(i32, (2 * LANES, LANES), 1)
    summer = ((kk < LANES) == (nn % GATHER_CHUNK < GATHER_CHUNK // 2)).astype(bf16)
    rr = lax.broadcasted_iota(i32, (half, LANES), 0)
    ll = lax.broadcasted_iota(i32, (half, LANES), 1)
    own_row = ll % (GATHER_CHUNK // 2) == rr // ROW_WORDS
    lane_chunk = ll // GATHER_CHUNK

    def consume(tile, c, t, acc):
        xt = xr_ref[pl.ds(pl.multiple_of(t * SUBLANES, SUBLANES), SUBLANES), :]
        xlo = xt[0:ROW_WORDS]
        xhi = xt[ROW_WORDS:2 * ROW_WORDS]
        x2lo = pltpu.repeat(jnp.concatenate([xlo, xlo], axis=0), crow // SUBLANES, axis=0)
        x2hi = pltpu.repeat(jnp.concatenate([xhi, xhi], axis=0), crow // SUBLANES, axis=0)
        lo, hi = _unpack(tile[c])
        p = lo * x2lo + hi * x2hi
        p_hi, p_lo = _split2(jnp.concatenate([p[0:half], p[half:crow]], axis=1))
        rs = jnp.dot(p_hi, summer, preferred_element_type=f32) + jnp.dot(p_lo, summer, preferred_element_type=f32)
        acc = acc + jnp.sum(jnp.where(own_row & (lane_chunk == c), rs, 0.0), axis=0, keepdims=True)
        if c == N_CHUNKS - 1:
            gelu = 0.5 * acc * (1.0 + lax.erf(acc * (2.0 ** -0.5)))
            w_ref[t] = g_ref[t] * gelu
        return acc

    _two_token_pipeline(ids_ref, tbl_ref, tile_a, tile_b, consume, jnp.zeros((1, PEER_SLOTS), f32))


def _peer_u(ids_flat, h2r, gate3, tbl, tok_off, n_tok):
    t = T_PEER
    b0 = tok_off // t
    return pl.pallas_call(
        _peer_u_kernel,
        grid=(n_tok // t,),
        in_specs=[pl.BlockSpec((t * PEER_SLOTS,), lambda i: (i + b0,), memory_space=pltpu.SMEM),
                  pl.BlockSpec((t * SUBLANES, LANES), lambda i: (i + b0, 0)),
                  pl.BlockSpec((t, 1, PEER_SLOTS), lambda i: (i + b0, 0, 0)),
                  _const_spec(tbl.shape)],
        out_specs=pl.BlockSpec((t, 1, PEER_SLOTS), lambda i: (i, 0, 0)),
        out_shape=jax.ShapeDtypeStruct((n_tok, 1, PEER_SLOTS), f32),
        scratch_shapes=[_tile_scratch(), _tile_scratch()],
        compiler_params=_cparams(("parallel",)),
        name="peer_u",
    )(ids_flat, h2r, gate3, tbl)


def _peer_v_sc(ids, w, tbl_words, tok_off):
    n_tok = w.shape[0]
    tpw = n_tok // SC_WORKERS
    lanes = SC_LANES
    mesh = plsc.VectorSubcoreMesh(core_axis_name="c", subcore_axis_name="s")

    @functools.partial(
        pl.kernel, mesh=mesh,
        out_type=jax.ShapeDtypeStruct((n_tok, D_MODEL), f32),
        scratch_types=[
            pltpu.VMEM((SC_HALF,), i32), pltpu.VMEM((SC_HALF,), i32),
            pltpu.VMEM((SC_HALF, ROW_WORDS_FLAT), i32), pltpu.VMEM((SC_HALF, ROW_WORDS_FLAT), i32),
            pltpu.VMEM((PEER_SLOTS,), f32),
            pltpu.VMEM((D_MODEL,), f32),
            pltpu.SemaphoreType.DMA, pltpu.SemaphoreType.DMA,
        ],
        compiler_params=dataclasses.replace(pltpu.CompilerParams(), needs_layout_passes=False),
        name="peer_v_sc",
    )
    def run(ids_hbm, w_hbm, tbl_hbm, out_hbm, idx_a, idx_b, rows_a, rows_b, w_v, out_v, sem_a, sem_b):
        wid = lax.axis_index("s") * SC_CORES + lax.axis_index("c")
        base = wid * tpw

        def accumulate(rows, w_off):
            for db in range(ROW_WORDS_FLAT // SC_DIM_BLOCK):
                def group_body(g, carry):
                    accs = []
                    for wc in range(SC_DIM_BLOCK // lanes):
                        accs.append(out_v[pl.ds(db * SC_DIM_BLOCK + wc * lanes, lanes)])
                        accs.append(out_v[pl.ds(ROW_WORDS_FLAT + db * SC_DIM_BLOCK + wc * lanes, lanes)])
                    j0 = g * SC_ROW_GROUP
                    wchunk = w_v[pl.ds(pl.multiple_of((w_off + j0) // lanes * lanes, lanes), lanes)]
                    sub = (w_off + j0) % lanes
                    for r in range(SC_ROW_GROUP):
                        wj = jnp.take(wchunk, jnp.full((lanes,), sub + r, i32))
                        for wc in range(SC_DIM_BLOCK // lanes):
                            word = rows[j0 + r, pl.ds(db * SC_DIM_BLOCK + wc * lanes, lanes)]
                            lo = lax.bitcast_convert_type(word << 16, f32)
                            hi = lax.bitcast_convert_type(word & jnp.int32(-65536), f32)
                            accs[2 * wc] = accs[2 * wc] + wj * lo
                            accs[2 * wc + 1] = accs[2 * wc + 1] + wj * hi
                    for wc in range(SC_DIM_BLOCK // lanes):
                        out_v[pl.ds(db * SC_DIM_BLOCK + wc * lanes, lanes)] = accs[2 * wc]
                        out_v[pl.ds(ROW_WORDS_FLAT + db * SC_DIM_BLOCK + wc * lanes, lanes)] = accs[2 * wc + 1]
                    return carry
                lax.fori_loop(0, SC_HALF // SC_ROW_GROUP, group_body, 0)

        def token(i, carry):
            t = base + i
            pltpu.sync_copy(ids_hbm.at[tok_off + t, pl.ds(0, SC_HALF)], idx_a)
            pltpu.sync_copy(ids_hbm.at[tok_off + t, pl.ds(SC_HALF, SC_HALF)], idx_b)
            copy_a = pltpu.async_copy(tbl_hbm.at[idx_a], rows_a, sem_a)
            copy_b = pltpu.async_copy(tbl_hbm.at[idx_b], rows_b, sem_b)
            pltpu.sync_copy(w_hbm.at[t], w_v)
            for q in range(D_MODEL // lanes):
                out_v[pl.ds(q * lanes, lanes)] = jnp.zeros((lanes,), f32)
            copy_a.wait()
            accumulate(rows_a, 0)
            copy_b.wait()
            accumulate(rows_b, SC_HALF)
            pltpu.sync_copy(out_v, out_hbm.at[t])
            return carry

        lax.fori_loop(0, tpw, token, 0)

    return run(ids, w, tbl_words)


def _final_kernel(x1_ref, p_ref, g_ref, y_ref):
    x = x1_ref[...] + p_ref[...]
    ms = jnp.mean(x * x, axis=-1, keepdims=True)
    y_ref[...] = x * lax.rsqrt(ms + EPS) * g_ref[...]


def _final(x1, peer_out, final_g, tok_off):
    n = peer_out.shape[0]
    t = T_PROJ
    b0 = tok_off // t
    row = pl.BlockSpec((t, D_MODEL), lambda i: (i, 0))
    return pl.pallas_call(
        _final_kernel,
        grid=(n // t,),
        in_specs=[pl.BlockSpec((t, D_MODEL), lambda i: (i + b0, 0)), row, _const_spec((1, D_MODEL))],
        out_specs=row,
        out_shape=jax.ShapeDtypeStruct((n, D_MODEL), f32),
        compiler_params=_cparams(("parallel",)),
        name="final_norm",
    )(x1, peer_out, final_g)


def _rotary_tables(seq):
    half = ROT_DIM // 2
    inv = ROPE_THETA ** (-jnp.arange(half, dtype=f32) * 2.0 / ROT_DIM)
    ang = jnp.arange(seq, dtype=jnp.int32).astype(f32)[:, None] * inv[None, :]
    cos = jnp.cos(ang)
    sin = jnp.sin(ang)
    pad = HEAD_DIM - ROT_DIM
    one = jnp.ones((seq, pad), f32)
    zero = jnp.zeros((seq, pad), f32)
    zh = jnp.zeros((seq, half), f32)
    cos_h = jnp.concatenate([cos, cos, one], axis=1)
    sa_h = jnp.concatenate([-sin, zh, zero], axis=1)
    sb_h = jnp.concatenate([zh, sin, zero], axis=1)
    rep = LANES // HEAD_DIM
    return tuple(jnp.tile(t, (1, rep)) for t in (cos_h, sa_h, sb_h))


def _layer(x, params, tables):
    (norm1_g, w_in_bf, b_gate, merge_w, keys_bf, u_tbl, v_tbl, final_g) = params
    batch, seq, _ = x.shape
    n = batch * seq
    x2 = x.reshape(n, D_MODEL)
    cos_t, sa_t, sb_t = tables
    q, k, v, u, gates = _inproj(x2, seq, norm1_g, w_in_bf, b_gate, cos_t, sa_t, sb_t)
    os_, ls_ = [], []
    for g in range(N_GROUPS):
        o, lse = _attention_group(q[g], k[g], v[g], batch, seq, g)
        os_.append(o)
        ls_.append(lse)
    x1, h2r, qp = _merge(x2, seq, u, gates, os_, ls_, merge_w)
    ids, gate = _route(qp, keys_bf)
    ids_flat = ids.reshape(n * PEER_SLOTS)
    gate3 = gate.reshape(n, 1, PEER_SLOTS)
    pending = []
    off = 0
    for share in PEER_CHUNK_SHARES:
        n_chunk = n * share // sum(PEER_CHUNK_SHARES)
        w3 = _peer_u(ids_flat, h2r, gate3, u_tbl, off, n_chunk)
        pv = _peer_v_sc(ids, w3.reshape(n_chunk, PEER_SLOTS), v_tbl, off)
        pending.append((x1, pv, off))
        off += n_chunk
    return pending


def kernel(x_prompt, x_sample, norm1_g, w_in, b_gate, w_attn_up, conv_dw_w, conv_dw_b, conv_ln_g, conv_ln_b,
           conv_pw_w, conv_pw_b, w_out, norm2_g, peer_wq, peer_keys, peer_u, peer_v, final_g):
    assert w_in.shape[0] == 1, "one encoder layer followed by the final norm"
    row = lambda a: a.reshape(1, -1)
    tables = _rotary_tables(max(x_prompt.shape[1], x_sample.shape[1]))
    l = 0
    merge_w = (conv_dw_w[l], row(conv_dw_b[l]), row(conv_ln_g[l]), row(conv_ln_b[l]),
               conv_pw_w[l].astype(bf16), row(conv_pw_b[l]), w_attn_up[l].astype(bf16),
               w_out[l].astype(bf16), row(norm2_g[l]), peer_wq[l].astype(bf16))
    keys_bf = peer_keys[l].astype(bf16).reshape(2 * PEER_HEADS, N_SUBKEYS, -1)
    params = (row(norm1_g[l]), w_in[l].astype(bf16), row(b_gate[l]), merge_w, keys_bf,
              _pack_table(peer_u[l]), _pack_table(peer_v[l]).reshape(N_EXPERTS, ROW_WORDS_FLAT), row(final_g))

    def group(x):
        step = x.shape[0] // BATCH_SUBGROUPS
        pending = []
        for b in range(0, x.shape[0], step):
            pending += _layer(x[b:b + step], params, tables)
        return pending

    def finish(x, pending):
        ys = [_final(x1, pv, params[-1], off) for x1, pv, off in pending]
        return jnp.concatenate(ys, axis=0).reshape(x.shape)

    pend_p = group(x_prompt)
    pend_s = group(x_sample)
    return (finish(x_prompt, pend_p), finish(x_sample, pend_s))
```

```python
import dataclasses
import functools
import math

import numpy as np
import jax
import jax.numpy as jnp
from jax import lax
from jax.experimental import pallas as pl
from jax.experimental.pallas import tpu as pltpu
from jax.experimental.pallas import tpu_sc as plsc

f32 = jnp.float32
bf16 = jnp.bfloat16
i32 = jnp.int32

D_MODEL = 1024
HEAD_DIM = 64
N_GROUPS = 3
HEADS_PER_GROUP = 4
GROUP_W = HEADS_PER_GROUP * HEAD_DIM
ATTN_W = N_GROUPS * GROUP_W
WINDOWS = (128, 512, 2048)
DILATIONS = (1, 4, 16)
HALF_WIN = 64
ROT_DIM = HEAD_DIM // 4
ROPE_THETA = 500000.0
NEG_INF = -1e30
CONV_W = 512
CONV_K = 31
CONV_HALO = 16
PEER_HEADS = 8
N_SUBKEYS = 128
N_EXPERTS = N_SUBKEYS * N_SUBKEYS
PEER_TOPK = 16
PEER_SLOTS = PEER_HEADS * PEER_TOPK
EPS = 1e-6

LANES = 128
SUBLANES = 8
ROW_WORDS = D_MODEL // 2 // LANES
VMEM_LIMIT = 56 * 1024 * 1024

T_PROJ = 256
T_PEER = 256
PAIRS_PER_STEP = 8
GATHER_CHUNK = 32
N_CHUNKS = (8 * 16) // GATHER_CHUNK
BATCH_SUBGROUPS = 2
PEER_CHUNK_SHARES = (4, 4, 4, 2, 1, 1)

SC_CORES = 2
SC_SUBCORES = 16
SC_LANES = 16
SC_WORKERS = SC_CORES * SC_SUBCORES
ROW_WORDS_FLAT = D_MODEL // 2
SC_HALF = PEER_SLOTS // 2
SC_DIM_BLOCK = 128
SC_ROW_GROUP = 8
Q_SUB = 128


def _cparams(sem):
    return pltpu.CompilerParams(dimension_semantics=sem, vmem_limit_bytes=VMEM_LIMIT)


def _const_spec(shape):
    nd = len(shape)
    return pl.BlockSpec(shape, lambda *_: (0,) * nd, pipeline_mode=pl.Buffered(1))


def _inproj_kernel(x_ref, g_ref, w_ref, bg_ref, cos_ref, sa_ref, sb_ref, *refs):
    qkv_refs = (refs[0:N_GROUPS], refs[N_GROUPS:2 * N_GROUPS], refs[2 * N_GROUPS:3 * N_GROUPS])
    u_ref, gate_ref, stage_ref = refs[3 * N_GROUPS:]
    x = x_ref[...]
    ms = jnp.mean(x * x, axis=-1, keepdims=True)
    h = (x * lax.rsqrt(ms + EPS) * g_ref[...]).astype(bf16)

    def proj(lo, hi):
        return jnp.dot(h, w_ref[:, lo:hi], preferred_element_type=f32)

    cos = cos_ref[...]
    sa = sa_ref[...]
    sb = sb_ref[...]

    def rotary(tc, scale):
        r = tc * cos + pltpu.roll(tc, LANES - ROT_DIM // 2, 1) * sa + pltpu.roll(tc, ROT_DIM // 2, 1) * sb
        return r * scale

    def emit(t, out_refs, fn):
        for c in range(ATTN_W // LANES):
            g, half = divmod(c, GROUP_W // LANES)
            d = DILATIONS[g]
            chunk = fn(t[:, c * LANES:(c + 1) * LANES])
            if d == 1:
                out_refs[g][:, half * LANES:(half + 1) * LANES] = chunk.astype(bf16)
                continue
            stage_ref[...] = chunk
            for r in range(d):
                col = r * GROUP_W + half * LANES
                out_refs[g][:, col:col + LANES] = stage_ref[pl.ds(r, T_PROJ // d, stride=d), :].astype(bf16)

    emit(proj(0, ATTN_W), qkv_refs[0], lambda tc: rotary(tc, HEAD_DIM ** -0.5))
    emit(proj(ATTN_W, 2 * ATTN_W), qkv_refs[1], lambda tc: rotary(tc, 1.0))
    emit(proj(2 * ATTN_W, 3 * ATTN_W), qkv_refs[2], lambda tc: tc)
    c0 = 3 * ATTN_W
    a = proj(c0, c0 + CONV_W)
    b = proj(c0 + CONV_W, c0 + 2 * CONV_W)
    u_ref[...] = a * jax.nn.sigmoid(b)
    gates = proj(c0 + 2 * CONV_W, c0 + 2 * CONV_W + 2 * D_MODEL) + bg_ref[...]
    gate_ref[...] = jax.nn.sigmoid(gates).astype(bf16)


def _inproj(x2, seq, norm1_g, w_in_bf, b_gate, cos_t, sa_t, sb_t):
    n = x2.shape[0]
    nsb = seq // T_PROJ
    in_cols = w_in_bf.shape[1]
    row = lambda w: pl.BlockSpec((T_PROJ, w), lambda i: (i, 0))
    pos = pl.BlockSpec((T_PROJ, LANES), lambda i: (i % nsb, 0))
    grp_specs = [pl.BlockSpec((T_PROJ // d, d * GROUP_W), lambda i: (i, 0)) for d in DILATIONS]
    grp_shapes = [jax.ShapeDtypeStruct((n // d, d * GROUP_W), bf16) for d in DILATIONS]
    outs = pl.pallas_call(
        _inproj_kernel,
        grid=(n // T_PROJ,),
        in_specs=[row(D_MODEL), _const_spec((1, D_MODEL)), _const_spec((D_MODEL, in_cols)),
                  _const_spec((1, 2 * D_MODEL)), pos, pos, pos],
        out_specs=grp_specs * 3 + [row(CONV_W), row(2 * D_MODEL)],
        out_shape=grp_shapes * 3
        + [jax.ShapeDtypeStruct((n, CONV_W), f32), jax.ShapeDtypeStruct((n, 2 * D_MODEL), bf16)],
        scratch_shapes=[pltpu.VMEM((T_PROJ, LANES), f32)],
        compiler_params=_cparams(("parallel",)),
        name="inproj",
    )(x2, norm1_g, w_in_bf, b_gate, cos_t, sa_t, sb_t)
    q, k, v = outs[0:N_GROUPS], outs[N_GROUPS:2 * N_GROUPS], outs[2 * N_GROUPS:3 * N_GROUPS]
    return q, k, v, outs[3 * N_GROUPS], outs[3 * N_GROUPS + 1]


def _attn_kernel(q_ref, kp_ref, kc_ref, kn_ref, vp_ref, vc_ref, vn_ref, o_ref, lse_ref,
                 kw_ref, vw_ref, *, tq, n_rows):
    i = pl.program_id(2)
    kw_ref[0:HALF_WIN] = kp_ref[0]
    kw_ref[HALF_WIN:HALF_WIN + tq] = kc_ref[0]
    kw_ref[HALF_WIN + tq:2 * HALF_WIN + tq] = kn_ref[0]
    vw_ref[0:HALF_WIN] = vp_ref[0]
    vw_ref[HALF_WIN:HALF_WIN + tq] = vc_ref[0]
    vw_ref[HALF_WIN + tq:2 * HALF_WIN + tq] = vn_ref[0]

    win = Q_SUB + 2 * HALF_WIN
    qi = lax.broadcasted_iota(i32, (Q_SUB, win), 0)
    kj = lax.broadcasted_iota(i32, (Q_SUB, win), 1)
    band = (kj - qi >= 0) & (kj - qi <= 2 * HALF_WIN)
    head_of_lane = lax.broadcasted_iota(i32, (1, GROUP_W), 1) // HEAD_DIM

    for s in range(tq // Q_SUB):
        qs = q_ref[0, s * Q_SUB:(s + 1) * Q_SUB, :]
        kwin = kw_ref[s * Q_SUB:s * Q_SUB + win, :]
        vwin = vw_ref[s * Q_SUB:s * Q_SUB + win, :]
        key_row = i * tq + (s * Q_SUB - HALF_WIN) + kj
        ok = band & (key_row >= 0) & (key_row < n_rows)
        o_acc = jnp.zeros((Q_SUB, GROUP_W), f32)
        l_acc = jnp.zeros((Q_SUB, GROUP_W), f32)
        for h in range(HEADS_PER_GROUP):
            hm = head_of_lane == h
            qh = jnp.where(hm, qs, jnp.zeros_like(qs))
            sc = lax.dot_general(qh, kwin, (((1,), (1,)), ((), ())), preferred_element_type=f32)
            sc = jnp.where(ok, sc, NEG_INF)
            m = jnp.max(sc, axis=-1, keepdims=True)
            p = jnp.exp(sc - m)
            den = jnp.sum(p, axis=-1, keepdims=True)
            pv = jnp.dot(p.astype(bf16), vwin, preferred_element_type=f32)
            o_acc = jnp.where(hm, pv / den, o_acc)
            l_acc = jnp.where(hm, m + jnp.log(den), l_acc)
        o_ref[0, s * Q_SUB:(s + 1) * Q_SUB, :] = o_acc
        lse_ref[0, s * Q_SUB:(s + 1) * Q_SUB, :] = l_acc


def _attention_group(q, k, v, batch, seq, g):
    d = DILATIONS[g]
    n_rows = seq // d
    tq = min(512, n_rows)
    nblk = n_rows // tq
    hb = tq // HALF_WIN
    n_halo_blocks = n_rows // HALF_WIN
    view = lambda t: t.reshape(batch, n_rows, d * GROUP_W)
    cur = pl.BlockSpec((1, tq, GROUP_W), lambda b, r, i: (b, i, r))
    prev = pl.BlockSpec((1, HALF_WIN, GROUP_W), lambda b, r, i: (b, jnp.maximum(i * hb - 1, 0), r))
    nxt = pl.BlockSpec((1, HALF_WIN, GROUP_W),
                       lambda b, r, i: (b, jnp.minimum((i + 1) * hb, n_halo_blocks - 1), r))
    o, lse = pl.pallas_call(
        functools.partial(_attn_kernel, tq=tq, n_rows=n_rows),
        grid=(batch, d, nblk),
        in_specs=[cur, prev, cur, nxt, prev, cur, nxt],
        out_specs=[cur, cur],
        out_shape=[jax.ShapeDtypeStruct((batch, n_rows, d * GROUP_W), f32)] * 2,
        scratch_shapes=[pltpu.VMEM((tq + 2 * HALF_WIN, GROUP_W), bf16)] * 2,
        compiler_params=_cparams(("parallel", "parallel", "parallel")),
        name=f"attn_g{g}",
    )(view(q), view(k), view(k), view(k), view(v), view(v), view(v))
    return o.reshape(batch * seq, GROUP_W), lse.reshape(batch * seq, GROUP_W)


def _merge_kernel(x_ref, up_ref, uc_ref, un_ref, gate_ref,
                  o0_ref, o1_ref, o2_ref, l0_ref, l1_ref, l2_ref,
                  dww_ref, dwb_ref, lng_ref, lnb_ref, pww_ref, pwb_ref, wup_ref, wout_ref, n2g_ref, wq_ref,
                  x1_ref, h2_ref, qp_ref, ue_ref, cv_ref, *, nsb):
    i = pl.program_id(0)
    t = T_PROJ
    first = (i % nsb) == 0
    last = (i % nsb) == nsb - 1
    ue_ref[0:CONV_HALO] = jnp.where(first, 0.0, up_ref[...])
    ue_ref[CONV_HALO:CONV_HALO + t] = uc_ref[...]
    ue_ref[CONV_HALO + t:2 * CONV_HALO + t] = jnp.where(last, 0.0, un_ref[...])

    rc = 32
    off = CONV_HALO - CONV_K // 2
    for r0 in range(0, t, rc):
        acc = jnp.zeros((rc, CONV_W), f32)
        for j in range(CONV_K):
            acc = acc + ue_ref[r0 + off + j:r0 + off + j + rc, :] * dww_ref[j:j + 1, :]
        cv_ref[r0:r0 + rc, :] = acc
    c = cv_ref[...] + dwb_ref[...]
    mu = jnp.mean(c, axis=-1, keepdims=True)
    cc = c - mu
    var = jnp.mean(cc * cc, axis=-1, keepdims=True)
    un = cc * lax.rsqrt(var + EPS) * lng_ref[...] + lnb_ref[...]
    sw = un * jax.nn.sigmoid(un)
    conv = jnp.dot(sw.astype(bf16), pww_ref[...], preferred_element_type=f32) + pwb_ref[...]

    l0 = l0_ref[...]
    l1 = l1_ref[...]
    l2 = l2_ref[...]
    lm = jnp.maximum(jnp.maximum(l0, l1), l2)
    e0 = jnp.exp(l0 - lm)
    e1 = jnp.exp(l1 - lm)
    e2 = jnp.exp(l2 - lm)
    comb = (e0 * o0_ref[...] + e1 * o1_ref[...] + e2 * o2_ref[...]) / (e0 + e1 + e2)
    attn = jnp.dot(comb.astype(bf16), wup_ref[...], preferred_element_type=f32)

    g_attn = gate_ref[:, 0:D_MODEL].astype(f32)
    g_conv = gate_ref[:, D_MODEL:2 * D_MODEL].astype(f32)
    mixed = g_attn * attn + g_conv * conv
    x1 = x_ref[...] + jnp.dot(mixed.astype(bf16), wout_ref[...], preferred_element_type=f32)
    x1_ref[...] = x1
    ms = jnp.mean(x1 * x1, axis=-1, keepdims=True)
    h2 = x1 * lax.rsqrt(ms + EPS) * n2g_ref[...]
    for c in range(D_MODEL // LANES):
        h2_ref[pl.ds(c, t, stride=SUBLANES), :] = h2[:, c * LANES:(c + 1) * LANES]
    qp_ref[...] = jnp.dot(h2.astype(bf16), wq_ref[...], preferred_element_type=f32).astype(bf16)


def _merge(x2, seq, u, gates, os_, ls_, weights):
    n = x2.shape[0]
    t = T_PROJ
    nsb = seq // t
    hb = t // CONV_HALO
    nhalo = n // CONV_HALO
    row = lambda w: pl.BlockSpec((t, w), lambda i: (i, 0))
    prev = pl.BlockSpec((CONV_HALO, CONV_W), lambda i: (jnp.maximum(i * hb - 1, 0), 0))
    nxt = pl.BlockSpec((CONV_HALO, CONV_W), lambda i: (jnp.minimum((i + 1) * hb, nhalo - 1), 0))
    wspecs = [_const_spec(w.shape) for w in weights]
    qw = weights[-1].shape[1]
    return pl.pallas_call(
        functools.partial(_merge_kernel, nsb=nsb),
        grid=(n // t,),
        in_specs=[row(D_MODEL), prev, row(CONV_W), nxt, row(2 * D_MODEL)] + [row(GROUP_W)] * 6 + wspecs,
        out_specs=[row(D_MODEL), pl.BlockSpec((t * SUBLANES, LANES), lambda i: (i, 0)), row(qw)],
        out_shape=[jax.ShapeDtypeStruct((n, D_MODEL), f32), jax.ShapeDtypeStruct((n * SUBLANES, LANES), f32),
                   jax.ShapeDtypeStruct((n, qw), bf16)],
        scratch_shapes=[pltpu.VMEM((t + 2 * CONV_HALO, CONV_W), f32), pltpu.VMEM((t, CONV_W), f32)],
        compiler_params=_cparams(("parallel",)),
        name="merge",
    )(x2, u, u, u, gates, *os_, *ls_, *weights)


def _candidate_slabs():
    slabs = [("row", 0, 0, 16), ("row", 1, 0, 8)]
    for j in range(PEER_TOPK):
        hi = PEER_TOPK // (j + 1)
        if hi > 2:
            slabs.append(("col", j, 2, hi))
    return slabs


def _route_kernel(qp_ref, keys_ref, ids_ref, gate_ref, val_ref, idx_ref, best_ref, idt_ref, gt_ref):
    t = T_PROJ
    k_iota = lax.broadcasted_iota(i32, (N_SUBKEYS, t), 0)
    for hc in range(2 * PEER_HEADS):
        q = qp_ref[:, hc * N_SUBKEYS:(hc + 1) * N_SUBKEYS]
        s = lax.dot_general(keys_ref[hc], q, (((1,), (1,)), ((), ())), preferred_element_type=f32)
        for r in range(PEER_TOPK):
            m = jnp.max(s, axis=0, keepdims=True)
            am = jnp.min(jnp.where(s == m, k_iota, N_SUBKEYS), axis=0, keepdims=True)
            s = jnp.where(k_iota == am, -jnp.inf, s)
            val_ref[hc, pl.ds(r, 1), :] = m
            idx_ref[hc, pl.ds(r, 1), :] = am

    r_iota = lax.broadcasted_iota(i32, (PEER_TOPK, t), 0)
    slabs = _candidate_slabs()
    for h in range(PEER_HEADS):
        v0 = val_ref[2 * h]
        v1 = val_ref[2 * h + 1]
        i0 = idx_ref[2 * h]
        i1 = idx_ref[2 * h + 1]
        cands, flats, eids = [], [], []
        for kind, fixed, lo, hi in slabs:
            if kind == "row":
                c = v0[fixed:fixed + 1, :] + v1
                fl = fixed * PEER_TOPK + r_iota
                ei = i0[fixed:fixed + 1, :] * N_SUBKEYS + i1
            else:
                c = v0 + v1[fixed:fixed + 1, :]
                fl = r_iota * PEER_TOPK + fixed
                ei = i0 * N_SUBKEYS + i1[fixed:fixed + 1, :]
            valid = (r_iota >= lo) & (r_iota < hi)
            cands.append(jnp.where(valid, c, -jnp.inf))
            flats.append(fl)
            eids.append(ei)
        big = PEER_TOPK * PEER_TOPK
        for r in range(PEER_TOPK):
            m = functools.reduce(jnp.maximum, cands)
            m = jnp.max(m, axis=0, keepdims=True)
            fsel = functools.reduce(jnp.minimum, [jnp.where(c == m, fl, big) for c, fl in zip(cands, flats)])
            fsel = jnp.min(fsel, axis=0, keepdims=True)
            hit = [fl == fsel for fl in flats]
            eid = functools.reduce(jnp.maximum, [jnp.where(hh, ei, -1) for hh, ei in zip(hit, eids)])
            eid = jnp.max(eid, axis=0, keepdims=True)
            cands = [jnp.where(hh, -jnp.inf, c) for hh, c in zip(hit, cands)]
            best_ref[pl.ds(r, 1), :] = m
            idt_ref[pl.ds(h * PEER_TOPK + r, 1), :] = eid
        b = best_ref[...]
        e = jnp.exp(b - jnp.max(b, axis=0, keepdims=True))
        gt_ref[h * PEER_TOPK:(h + 1) * PEER_TOPK, :] = e / jnp.sum(e, axis=0, keepdims=True)
    ids_ref[...] = idt_ref[...].T
    gate_ref[...] = gt_ref[...].T


def _route(qp, keys_bf):
    n, qw = qp.shape
    t = T_PROJ
    return pl.pallas_call(
        _route_kernel,
        grid=(n // t,),
        in_specs=[pl.BlockSpec((t, qw), lambda i: (i, 0)), _const_spec(keys_bf.shape)],
        out_specs=[pl.BlockSpec((t, PEER_SLOTS), lambda i: (i, 0))] * 2,
        out_shape=[jax.ShapeDtypeStruct((n, PEER_SLOTS), i32), jax.ShapeDtypeStruct((n, PEER_SLOTS), f32)],
        scratch_shapes=[pltpu.VMEM((2 * PEER_HEADS, PEER_TOPK, t), f32), pltpu.VMEM((2 * PEER_HEADS, PEER_TOPK, t), i32),
                        pltpu.VMEM((PEER_TOPK, t), f32), pltpu.VMEM((PEER_SLOTS, t), i32),
                        pltpu.VMEM((PEER_SLOTS, t), f32)],
        compiler_params=_cparams(("parallel",)),
        name="route",
    )(qp, keys_bf)


def _pack_table(tbl):
    bits = lax.bitcast_convert_type(tbl.astype(bf16), jnp.uint16).astype(jnp.uint32)
    half = D_MODEL // 2
    words = bits[:, :half] | (bits[:, half:] << 16)
    return lax.bitcast_convert_type(words, i32).reshape(tbl.shape[0], ROW_WORDS, LANES)


def _unpack(words):
    lo = pltpu.bitcast(words << 16, f32)
    hi = pltpu.bitcast(words & jnp.int32(-65536), f32)
    return lo, hi


def _split2(x):
    hi = x.astype(bf16)
    lo = (x - hi.astype(f32)).astype(bf16)
    return hi, lo


def _gather_chunk(ids_ref, tbl_ref, tile_ref, c, base):
    e = None
    for k in range(GATHER_CHUNK):
        e = ids_ref[base + k]
        tile_ref[c, k * ROW_WORDS:(k + 1) * ROW_WORDS, :] = tbl_ref[e]
    return e


def _two_token_pipeline(ids_ref, tbl_ref, tile_a, tile_b, consume, init):
    tile_b[...] = jnp.zeros(tile_b.shape, i32)
    last = T_PEER - 1

    def half_step(t_gather, gather_tile, t_consume, consume_tile, dep):
        base = jnp.minimum(t_gather, last) * PEER_SLOTS
        tc = jnp.clip(t_consume, 0, last)
        acc = init
        for c in range(N_CHUNKS):
            dep = _gather_chunk(ids_ref, tbl_ref, gather_tile, c, base + c * GATHER_CHUNK + (dep >> 31))
            acc = consume(consume_tile, c, tc, acc)
        return dep

    def pairs_step(i, dep):
        for p in range(PAIRS_PER_STEP):
            even = 2 * (PAIRS_PER_STEP * i + p)
            dep = half_step(even, tile_a, even - 1, tile_b, dep)
            dep = half_step(even + 1, tile_b, even, tile_a, dep)
        return dep

    lax.fori_loop(0, T_PEER // (2 * PAIRS_PER_STEP) + 1, pairs_step, jnp.int32(0))


def _tile_scratch():
    return pltpu.VMEM((N_CHUNKS, GATHER_CHUNK * ROW_WORDS, LANES), i32)


def _peer_u_kernel(ids_ref, xr_ref, g_ref, tbl_ref, w_ref, tile_a, tile_b):
    crow = GATHER_CHUNK * ROW_WORDS
    half = crow // 2
    kk = lax.broadcasted_iota(i32, (2 * LANES, LANES), 0)
    nn = lax.broadcasted_iota(i32, (2 * LANES, LANES), 1)
    summer = ((kk < LANES) == (nn % GATHER_CHUNK < GATHER_CHUNK // 2)).astype(bf16)
    rr = lax.broadcasted_iota(i32, (half, LANES), 0)
    ll = lax.broadcasted_iota(i32, (half, LANES), 1)
    own_row = ll % (GATHER_CHUNK // 2) == rr // ROW_WORDS
    lane_chunk = ll // GATHER_CHUNK

    def consume(tile, c, t, acc):
        xt = xr_ref[pl.ds(pl.multiple_of(t * SUBLANES, SUBLANES), SUBLANES), :]
        xlo = xt[0:ROW_WORDS]
        xhi = xt[ROW_WORDS:2 * ROW_WORDS]
        x2lo = pltpu.repeat(jnp.concatenate([xlo, xlo], axis=0), crow // SUBLANES, axis=0)
        x2hi = pltpu.repeat(jnp.concatenate([xhi, xhi], axis=0), crow // SUBLANES, axis=0)
        lo, hi = _unpack(tile[c])
        p = lo * x2lo + hi * x2hi
        p_hi, p_lo = _split2(jnp.concatenate([p[0:half], p[half:crow]], axis=1))
        rs = jnp.dot(p_hi, summer, preferred_element_type=f32) + jnp.dot(p_lo, summer, preferred_element_type=f32)
        acc = acc + jnp.sum(jnp.where(own_row & (lane_chunk == c), rs, 0.0), axis=0, keepdims=True)
        if c == N_CHUNKS - 1:
            gelu = 0.5 * acc * (1.0 + lax.erf(acc * (2.0 ** -0.5)))
            w_ref[t] = g_ref[t] * gelu
        return acc

    _two_token_pipeline(ids_ref, tbl_ref, tile_a, tile_b, consume, jnp.zeros((1, PEER_SLOTS), f32))


def _peer_u(ids_flat, h2r, gate3, tbl, tok_off, n_tok):
    t = T_PEER
    b0 = tok_off // t
    return pl.pallas_call(
        _peer_u_kernel,
        grid=(n_tok // t,),
        in_specs=[pl.BlockSpec((t * PEER_SLOTS,), lambda i: (i + b0,), memory_space=pltpu.SMEM),
                  pl.BlockSpec((t * SUBLANES, LANES), lambda i: (i + b0, 0)),
                  pl.BlockSpec((t, 1, PEER_SLOTS), lambda i: (i + b0, 0, 0)),
                  _const_spec(tbl.shape)],
        out_specs=pl.BlockSpec((t, 1, PEER_SLOTS), lambda i: (i, 0, 0)),
        out_shape=jax.ShapeDtypeStruct((n_tok, 1, PEER_SLOTS), f32),
        scratch_shapes=[_tile_scratch(), _tile_scratch()],
        compiler_params=_cparams(("parallel",)),
        name="peer_u",
    )(ids_flat, h2r, gate3, tbl)


def _peer_v_sc(ids, w, tbl_words, tok_off):
    n_tok = w.shape[0]
    tpw = n_tok // SC_WORKERS
    lanes = SC_LANES
    mesh = plsc.VectorSubcoreMesh(core_axis_name="c", subcore_axis_name="s")

    @functools.partial(
        pl.kernel, mesh=mesh,
        out_type=jax.ShapeDtypeStruct((n_tok, D_MODEL), f32),
        scratch_types=[
            pltpu.VMEM((SC_HALF,), i32), pltpu.VMEM((SC_HALF,), i32),
            pltpu.VMEM((SC_HALF, ROW_WORDS_FLAT), i32), pltpu.VMEM((SC_HALF, ROW_WORDS_FLAT), i32),
            pltpu.VMEM((PEER_SLOTS,), f32),
            pltpu.VMEM((D_MODEL,), f32),
            pltpu.SemaphoreType.DMA, pltpu.SemaphoreType.DMA,
        ],
        compiler_params=dataclasses.replace(pltpu.CompilerParams(), needs_layout_passes=False),
        name="peer_v_sc",
    )
    def run(ids_hbm, w_hbm, tbl_hbm, out_hbm, idx_a, idx_b, rows_a, rows_b, w_v, out_v, sem_a, sem_b):
        wid = lax.axis_index("s") * SC_CORES + lax.axis_index("c")
        base = wid * tpw

        def accumulate(rows, w_off):
            for db in range(ROW_WORDS_FLAT // SC_DIM_BLOCK):
                def group_body(g, carry):
                    accs = []
                    for wc in range(SC_DIM_BLOCK // lanes):
                        accs.append(out_v[pl.ds(db * SC_DIM_BLOCK + wc * lanes, lanes)])
                        accs.append(out_v[pl.ds(ROW_WORDS_FLAT + db * SC_DIM_BLOCK + wc * lanes, lanes)])
                    j0 = g * SC_ROW_GROUP
                    wchunk = w_v[pl.ds(pl.multiple_of((w_off + j0) // lanes * lanes, lanes), lanes)]
                    sub = (w_off + j0) % lanes
                    for r in range(SC_ROW_GROUP):
                        wj = jnp.take(wchunk, jnp.full((lanes,), sub + r, i32))
                        for wc in range(SC_DIM_BLOCK // lanes):
                            word = rows[j0 + r, pl.ds(db * SC_DIM_BLOCK + wc * lanes, lanes)]
                            lo = lax.bitcast_convert_type(word << 16, f32)
                            hi = lax.bitcast_convert_type(word & jnp.int32(-65536), f32)
                            accs[2 * wc] = accs[2 * wc] + wj * lo
                            accs[2 * wc + 1] = accs[2 * wc + 1] + wj * hi
                    for wc in range(SC_DIM_BLOCK // lanes):
                        out_v[pl.ds(db * SC_DIM_BLOCK + wc * lanes, lanes)] = accs[2 * wc]
                        out_v[pl.ds(ROW_WORDS_FLAT + db * SC_DIM_BLOCK + wc * lanes, lanes)] = accs[2 * wc + 1]
                    return carry
                lax.fori_loop(0, SC_HALF // SC_ROW_GROUP, group_body, 0)

        def token(i, carry):
            t = base + i
            pltpu.sync_copy(ids_hbm.at[tok_off + t, pl.ds(0, SC_HALF)], idx_a)
            pltpu.sync_copy(ids_hbm.at[tok_off + t, pl.ds(SC_HALF, SC_HALF)], idx_b)
            copy_a = pltpu.async_copy(tbl_hbm.at[idx_a], rows_a, sem_a)
            copy_b = pltpu.async_copy(tbl_hbm.at[idx_b], rows_b, sem_b)
            pltpu.sync_copy(w_hbm.at[t], w_v)
            for q in range(D_MODEL // lanes):
                out_v[pl.ds(q * lanes, lanes)] = jnp.zeros((lanes,), f32)
            copy_a.wait()
            accumulate(rows_a, 0)
            copy_b.wait()
            accumulate(rows_b, SC_HALF)
            pltpu.sync_copy(out_v, out_hbm.at[t])
            return carry

        lax.fori_loop(0, tpw, token, 0)

    return run(ids, w, tbl_words)


def _final_kernel(x1_ref, p_ref, g_ref, y_ref):
    x = x1_ref[...] + p_ref[...]
    ms = jnp.mean(x * x, axis=-1, keepdims=True)
    y_ref[...] = x * lax.rsqrt(ms + EPS) * g_ref[...]


def _final(x1, peer_out, final_g, tok_off):
    n = peer_out.shape[0]
    t = T_PROJ
    b0 = tok_off // t
    row = pl.BlockSpec((t, D_MODEL), lambda i: (i, 0))
    return pl.pallas_call(
        _final_kernel,
        grid=(n // t,),
        in_specs=[pl.BlockSpec((t, D_MODEL), lambda i: (i + b0, 0)), row, _const_spec((1, D_MODEL))],
        out_specs=row,
        out_shape=jax.ShapeDtypeStruct((n, D_MODEL), f32),
        compiler_params=_cparams(("parallel",)),
        name="final_norm",
    )(x1, peer_out, final_g)


def _rotary_tables(seq):
    half = ROT_DIM // 2
    inv = ROPE_THETA ** (-jnp.arange(half, dtype=f32) * 2.0 / ROT_DIM)
    ang = jnp.arange(seq, dtype=jnp.int32).astype(f32)[:, None] * inv[None, :]
    cos = jnp.cos(ang)
    sin = jnp.sin(ang)
    pad = HEAD_DIM - ROT_DIM
    one = jnp.ones((seq, pad), f32)
    zero = jnp.zeros((seq, pad), f32)
    zh = jnp.zeros((seq, half), f32)
    cos_h = jnp.concatenate([cos, cos, one], axis=1)
    sa_h = jnp.concatenate([-sin, zh, zero], axis=1)
    sb_h = jnp.concatenate([zh, sin, zero], axis=1)
    rep = LANES // HEAD_DIM
    return tuple(jnp.tile(t, (1, rep)) for t in (cos_h, sa_h, sb_h))


def _front_ops(x, params, tables, st):
    (norm1_g, w_in_bf, b_gate, merge_w, keys_bf, _, _, _) = params
    batch, seq, _ = x.shape
    n = batch * seq
    x2 = x.reshape(n, D_MODEL)

    def inproj():
        st["q"], st["k"], st["v"], st["u"], st["gates"] = _inproj(x2, seq, norm1_g, w_in_bf, b_gate, *tables)
        st["o"], st["lse"] = [], []

    def attention(g):
        o, lse = _attention_group(st["q"][g], st["k"][g], st["v"][g], batch, seq, g)
        st["o"].append(o)
        st["lse"].append(lse)

    def merge():
        st["x1"], st["h2r"], st["qp"] = _merge(x2, seq, st["u"], st["gates"], st["o"], st["lse"], merge_w)

    def route():
        ids, gate = _route(st["qp"], keys_bf)
        st["ids"], st["gate3"], st["n"] = ids, gate.reshape(n, 1, PEER_SLOTS), n

    return [inproj] + [functools.partial(attention, g) for g in range(N_GROUPS)] + [merge, route]


def _peer_ops(params, st, pending):
    u_tbl, v_tbl = params[5], params[6]

    def chunk(off, n_chunk):
        ids = st["ids"]
        w3 = _peer_u(ids.reshape(st["n"] * PEER_SLOTS), st["h2r"], st["gate3"], u_tbl, off, n_chunk)
        pv = _peer_v_sc(ids, w3.reshape(n_chunk, PEER_SLOTS), v_tbl, off)
        pending.append((st["x1"], pv, off))

    ops, off = [], 0
    for share in PEER_CHUNK_SHARES:
        n_chunk = st["tokens"] * share // sum(PEER_CHUNK_SHARES)
        ops.append(functools.partial(chunk, off, n_chunk))
        off += n_chunk
    return ops


def kernel(x_prompt, x_sample, norm1_g, w_in, b_gate, w_attn_up, conv_dw_w, conv_dw_b, conv_ln_g, conv_ln_b,
           conv_pw_w, conv_pw_b, w_out, norm2_g, peer_wq, peer_keys, peer_u, peer_v, final_g):
    assert w_in.shape[0] == 1, "one encoder layer followed by the final norm"
    row = lambda a: a.reshape(1, -1)
    tables = _rotary_tables(max(x_prompt.shape[1], x_sample.shape[1]))
    l = 0
    merge_w = (conv_dw_w[l], row(conv_dw_b[l]), row(conv_ln_g[l]), row(conv_ln_b[l]),
               conv_pw_w[l].astype(bf16), row(conv_pw_b[l]), w_attn_up[l].astype(bf16),
               w_out[l].astype(bf16), row(norm2_g[l]), peer_wq[l].astype(bf16))
    keys_bf = peer_keys[l].astype(bf16).reshape(2 * PEER_HEADS, N_SUBKEYS, -1)
    params = (row(norm1_g[l]), w_in[l].astype(bf16), row(b_gate[l]), merge_w, keys_bf,
              _pack_table(peer_u[l]), _pack_table(peer_v[l]).reshape(N_EXPERTS, ROW_WORDS_FLAT), row(final_g))

    subs = []
    for x in (x_prompt, x_sample):
        step = x.shape[0] // BATCH_SUBGROUPS
        subs += [x[b:b + step] for b in range(0, x.shape[0], step)]
    states = [dict(tokens=sub.shape[0] * sub.shape[1]) for sub in subs]
    pendings = [[] for _ in subs]
    for op in _front_ops(subs[0], params, tables, states[0]):
        op()
    for i in range(len(subs)):
        peer = _peer_ops(params, states[i], pendings[i])
        front = _front_ops(subs[i + 1], params, tables, states[i + 1]) if i + 1 < len(subs) else []
        for j in range(max(len(peer), len(front))):
            if j < len(peer):
                peer[j]()
            if j < len(front):
                front[j]()

    def finish(x, pending):
        ys = [_final(x1, pv, params[-1], off) for x1, pv, off in pending]
        return jnp.concatenate(ys, axis=0).reshape(x.shape)

    half = BATCH_SUBGROUPS
    return (finish(x_prompt, sum(pendings[:half], [])), finish(x_sample, sum(pendings[half:], [])))
```

```python
import dataclasses
import functools
import math

import numpy as np
import jax
import jax.numpy as jnp
from jax import lax
from jax.experimental import pallas as pl
from jax.experimental.pallas import tpu as pltpu
from jax.experimental.pallas import tpu_sc as plsc

f32 = jnp.float32
bf16 = jnp.bfloat16
i32 = jnp.int32

D_MODEL = 1024
HEAD_DIM = 64
N_GROUPS = 3
HEADS_PER_GROUP = 4
GROUP_W = HEADS_PER_GROUP * HEAD_DIM
ATTN_W = N_GROUPS * GROUP_W
WINDOWS = (128, 512, 2048)
DILATIONS = (1, 4, 16)
HALF_WIN = 64
ROT_DIM = HEAD_DIM // 4
ROPE_THETA = 500000.0
NEG_INF = -1e30
CONV_W = 512
CONV_K = 31
CONV_HALO = 16
PEER_HEADS = 8
N_SUBKEYS = 128
N_EXPERTS = N_SUBKEYS * N_SUBKEYS
PEER_TOPK = 16
PEER_SLOTS = PEER_HEADS * PEER_TOPK
EPS = 1e-6

LANES = 128
SUBLANES = 8
ROW_WORDS = D_MODEL // 2 // LANES
VMEM_LIMIT = 56 * 1024 * 1024

T_PROJ = 256
T_PEER = 256
PAIRS_PER_STEP = 8
GATHER_CHUNK = 32
N_CHUNKS = (8 * 16) // GATHER_CHUNK
BATCH_SUBGROUPS = 4
PEER_CHUNK_SHARES = (4, 4, 4, 2, 1, 1)

SC_CORES = 2
SC_SUBCORES = 16
SC_LANES = 16
SC_WORKERS = SC_CORES * SC_SUBCORES
ROW_WORDS_FLAT = D_MODEL // 2
SC_HALF = PEER_SLOTS // 2
SC_DIM_BLOCK = 128
SC_ROW_GROUP = 8
Q_SUB = 128


def _cparams(sem):
    return pltpu.CompilerParams(dimension_semantics=sem, vmem_limit_bytes=VMEM_LIMIT)


def _const_spec(shape):
    nd = len(shape)
    return pl.BlockSpec(shape, lambda *_: (0,) * nd, pipeline_mode=pl.Buffered(1))


def _inproj_kernel(x_ref, g_ref, w_ref, bg_ref, cos_ref, sa_ref, sb_ref, *refs):
    qkv_refs = (refs[0:N_GROUPS], refs[N_GROUPS:2 * N_GROUPS], refs[2 * N_GROUPS:3 * N_GROUPS])
    u_ref, gate_ref, stage_ref = refs[3 * N_GROUPS:]
    x = x_ref[...]
    ms = jnp.mean(x * x, axis=-1, keepdims=True)
    h = (x * lax.rsqrt(ms + EPS) * g_ref[...]).astype(bf16)

    def proj(lo, hi):
        return jnp.dot(h, w_ref[:, lo:hi], preferred_element_type=f32)

    cos = cos_ref[...]
    sa = sa_ref[...]
    sb = sb_ref[...]

    def rotary(tc, scale):
        r = tc * cos + pltpu.roll(tc, LANES - ROT_DIM // 2, 1) * sa + pltpu.roll(tc, ROT_DIM // 2, 1) * sb
        return r * scale

    def emit(t, out_refs, fn):
        for c in range(ATTN_W // LANES):
            g, half = divmod(c, GROUP_W // LANES)
            d = DILATIONS[g]
            chunk = fn(t[:, c * LANES:(c + 1) * LANES])
            if d == 1:
                out_refs[g][:, half * LANES:(half + 1) * LANES] = chunk.astype(bf16)
                continue
            stage_ref[...] = chunk
            for r in range(d):
                col = r * GROUP_W + half * LANES
                out_refs[g][:, col:col + LANES] = stage_ref[pl.ds(r, T_PROJ // d, stride=d), :].astype(bf16)

    emit(proj(0, ATTN_W), qkv_refs[0], lambda tc: rotary(tc, HEAD_DIM ** -0.5))
    emit(proj(ATTN_W, 2 * ATTN_W), qkv_refs[1], lambda tc: rotary(tc, 1.0))
    emit(proj(2 * ATTN_W, 3 * ATTN_W), qkv_refs[2], lambda tc: tc)
    c0 = 3 * ATTN_W
    a = proj(c0, c0 + CONV_W)
    b = proj(c0 + CONV_W, c0 + 2 * CONV_W)
    u_ref[...] = a * jax.nn.sigmoid(b)
    gates = proj(c0 + 2 * CONV_W, c0 + 2 * CONV_W + 2 * D_MODEL) + bg_ref[...]
    gate_ref[...] = jax.nn.sigmoid(gates).astype(bf16)


def _inproj(x2, seq, norm1_g, w_in_bf, b_gate, cos_t, sa_t, sb_t):
    n = x2.shape[0]
    nsb = seq // T_PROJ
    in_cols = w_in_bf.shape[1]
    row = lambda w: pl.BlockSpec((T_PROJ, w), lambda i: (i, 0))
    pos = pl.BlockSpec((T_PROJ, LANES), lambda i: (i % nsb, 0))
    grp_specs = [pl.BlockSpec((T_PROJ // d, d * GROUP_W), lambda i: (i, 0)) for d in DILATIONS]
    grp_shapes = [jax.ShapeDtypeStruct((n // d, d * GROUP_W), bf16) for d in DILATIONS]
    outs = pl.pallas_call(
        _inproj_kernel,
        grid=(n // T_PROJ,),
        in_specs=[row(D_MODEL), _const_spec((1, D_MODEL)), _const_spec((D_MODEL, in_cols)),
                  _const_spec((1, 2 * D_MODEL)), pos, pos, pos],
        out_specs=grp_specs * 3 + [row(CONV_W), row(2 * D_MODEL)],
        out_shape=grp_shapes * 3
        + [jax.ShapeDtypeStruct((n, CONV_W), f32), jax.ShapeDtypeStruct((n, 2 * D_MODEL), bf16)],
        scratch_shapes=[pltpu.VMEM((T_PROJ, LANES), f32)],
        compiler_params=_cparams(("parallel",)),
        name="inproj",
    )(x2, norm1_g, w_in_bf, b_gate, cos_t, sa_t, sb_t)
    q, k, v = outs[0:N_GROUPS], outs[N_GROUPS:2 * N_GROUPS], outs[2 * N_GROUPS:3 * N_GROUPS]
    return q, k, v, outs[3 * N_GROUPS], outs[3 * N_GROUPS + 1]


def _attn_kernel(q_ref, kp_ref, kc_ref, kn_ref, vp_ref, vc_ref, vn_ref, o_ref, lse_ref,
                 kw_ref, vw_ref, *, tq, n_rows):
    i = pl.program_id(2)
    kw_ref[0:HALF_WIN] = kp_ref[0]
    kw_ref[HALF_WIN:HALF_WIN + tq] = kc_ref[0]
    kw_ref[HALF_WIN + tq:2 * HALF_WIN + tq] = kn_ref[0]
    vw_ref[0:HALF_WIN] = vp_ref[0]
    vw_ref[HALF_WIN:HALF_WIN + tq] = vc_ref[0]
    vw_ref[HALF_WIN + tq:2 * HALF_WIN + tq] = vn_ref[0]

    win = Q_SUB + 2 * HALF_WIN
    qi = lax.broadcasted_iota(i32, (Q_SUB, win), 0)
    kj = lax.broadcasted_iota(i32, (Q_SUB, win), 1)
    band = (kj - qi >= 0) & (kj - qi <= 2 * HALF_WIN)
    head_of_lane = lax.broadcasted_iota(i32, (1, GROUP_W), 1) // HEAD_DIM

    for s in range(tq // Q_SUB):
        qs = q_ref[0, s * Q_SUB:(s + 1) * Q_SUB, :]
        kwin = kw_ref[s * Q_SUB:s * Q_SUB + win, :]
        vwin = vw_ref[s * Q_SUB:s * Q_SUB + win, :]
        key_row = i * tq + (s * Q_SUB - HALF_WIN) + kj
        ok = band & (key_row >= 0) & (key_row < n_rows)
        o_acc = jnp.zeros((Q_SUB, GROUP_W), f32)
        l_acc = jnp.zeros((Q_SUB, GROUP_W), f32)
        for h in range(HEADS_PER_GROUP):
            hm = head_of_lane == h
            qh = jnp.where(hm, qs, jnp.zeros_like(qs))
            sc = lax.dot_general(qh, kwin, (((1,), (1,)), ((), ())), preferred_element_type=f32)
            sc = jnp.where(ok, sc, NEG_INF)
            m = jnp.max(sc, axis=-1, keepdims=True)
            p = jnp.exp(sc - m)
            den = jnp.sum(p, axis=-1, keepdims=True)
            pv = jnp.dot(p.astype(bf16), vwin, preferred_element_type=f32)
            o_acc = jnp.where(hm, pv / den, o_acc)
            l_acc = jnp.where(hm, m + jnp.log(den), l_acc)
        o_ref[0, s * Q_SUB:(s + 1) * Q_SUB, :] = o_acc
        lse_ref[0, s * Q_SUB:(s + 1) * Q_SUB, :] = l_acc


def _attention_group(q, k, v, batch, seq, g):
    d = DILATIONS[g]
    n_rows = seq // d
    tq = min(512, n_rows)
    nblk = n_rows // tq
    hb = tq // HALF_WIN
    n_halo_blocks = n_rows // HALF_WIN
    view = lambda t: t.reshape(batch, n_rows, d * GROUP_W)
    cur = pl.BlockSpec((1, tq, GROUP_W), lambda b, r, i: (b, i, r))
    prev = pl.BlockSpec((1, HALF_WIN, GROUP_W), lambda b, r, i: (b, jnp.maximum(i * hb - 1, 0), r))
    nxt = pl.BlockSpec((1, HALF_WIN, GROUP_W),
                       lambda b, r, i: (b, jnp.minimum((i + 1) * hb, n_halo_blocks - 1), r))
    o, lse = pl.pallas_call(
        functools.partial(_attn_kernel, tq=tq, n_rows=n_rows),
        grid=(batch, d, nblk),
        in_specs=[cur, prev, cur, nxt, prev, cur, nxt],
        out_specs=[cur, cur],
        out_shape=[jax.ShapeDtypeStruct((batch, n_rows, d * GROUP_W), f32)] * 2,
        scratch_shapes=[pltpu.VMEM((tq + 2 * HALF_WIN, GROUP_W), bf16)] * 2,
        compiler_params=_cparams(("parallel", "parallel", "parallel")),
        name=f"attn_g{g}",
    )(view(q), view(k), view(k), view(k), view(v), view(v), view(v))
    return o.reshape(batch * seq, GROUP_W), lse.reshape(batch * seq, GROUP_W)


def _merge_kernel(x_ref, up_ref, uc_ref, un_ref, gate_ref,
                  o0_ref, o1_ref, o2_ref, l0_ref, l1_ref, l2_ref,
                  dww_ref, dwb_ref, lng_ref, lnb_ref, pww_ref, pwb_ref, wup_ref, wout_ref, n2g_ref, wq_ref,
                  x1_ref, h2_ref, qp_ref, ue_ref, cv_ref, *, nsb):
    i = pl.program_id(0)
    t = T_PROJ
    first = (i % nsb) == 0
    last = (i % nsb) == nsb - 1
    ue_ref[0:CONV_HALO] = jnp.where(first, 0.0, up_ref[...])
    ue_ref[CONV_HALO:CONV_HALO + t] = uc_ref[...]
    ue_ref[CONV_HALO + t:2 * CONV_HALO + t] = jnp.where(last, 0.0, un_ref[...])

    rc = 32
    off = CONV_HALO - CONV_K // 2
    for r0 in range(0, t, rc):
        acc = jnp.zeros((rc, CONV_W), f32)
        for j in range(CONV_K):
            acc = acc + ue_ref[r0 + off + j:r0 + off + j + rc, :] * dww_ref[j:j + 1, :]
        cv_ref[r0:r0 + rc, :] = acc
    c = cv_ref[...] + dwb_ref[...]
    mu = jnp.mean(c, axis=-1, keepdims=True)
    cc = c - mu
    var = jnp.mean(cc * cc, axis=-1, keepdims=True)
    un = cc * lax.rsqrt(var + EPS) * lng_ref[...] + lnb_ref[...]
    sw = un * jax.nn.sigmoid(un)
    conv = jnp.dot(sw.astype(bf16), pww_ref[...], preferred_element_type=f32) + pwb_ref[...]

    l0 = l0_ref[...]
    l1 = l1_ref[...]
    l2 = l2_ref[...]
    lm = jnp.maximum(jnp.maximum(l0, l1), l2)
    e0 = jnp.exp(l0 - lm)
    e1 = jnp.exp(l1 - lm)
    e2 = jnp.exp(l2 - lm)
    comb = (e0 * o0_ref[...] + e1 * o1_ref[...] + e2 * o2_ref[...]) / (e0 + e1 + e2)
    attn = jnp.dot(comb.astype(bf16), wup_ref[...], preferred_element_type=f32)

    g_attn = gate_ref[:, 0:D_MODEL].astype(f32)
    g_conv = gate_ref[:, D_MODEL:2 * D_MODEL].astype(f32)
    mixed = g_attn * attn + g_conv * conv
    x1 = x_ref[...] + jnp.dot(mixed.astype(bf16), wout_ref[...], preferred_element_type=f32)
    x1_ref[...] = x1
    ms = jnp.mean(x1 * x1, axis=-1, keepdims=True)
    h2 = x1 * lax.rsqrt(ms + EPS) * n2g_ref[...]
    for c in range(D_MODEL // LANES):
        h2_ref[pl.ds(c, t, stride=SUBLANES), :] = h2[:, c * LANES:(c + 1) * LANES]
    qp_ref[...] = jnp.dot(h2.astype(bf16), wq_ref[...], preferred_element_type=f32).astype(bf16)


def _merge(x2, seq, u, gates, os_, ls_, weights):
    n = x2.shape[0]
    t = T_PROJ
    nsb = seq // t
    hb = t // CONV_HALO
    nhalo = n // CONV_HALO
    row = lambda w: pl.BlockSpec((t, w), lambda i: (i, 0))
    prev = pl.BlockSpec((CONV_HALO, CONV_W), lambda i: (jnp.maximum(i * hb - 1, 0), 0))
    nxt = pl.BlockSpec((CONV_HALO, CONV_W), lambda i: (jnp.minimum((i + 1) * hb, nhalo - 1), 0))
    wspecs = [_const_spec(w.shape) for w in weights]
    qw = weights[-1].shape[1]
    return pl.pallas_call(
        functools.partial(_merge_kernel, nsb=nsb),
        grid=(n // t,),
        in_specs=[row(D_MODEL), prev, row(CONV_W), nxt, row(2 * D_MODEL)] + [row(GROUP_W)] * 6 + wspecs,
        out_specs=[row(D_MODEL), pl.BlockSpec((t * SUBLANES, LANES), lambda i: (i, 0)), row(qw)],
        out_shape=[jax.ShapeDtypeStruct((n, D_MODEL), f32), jax.ShapeDtypeStruct((n * SUBLANES, LANES), f32),
                   jax.ShapeDtypeStruct((n, qw), bf16)],
        scratch_shapes=[pltpu.VMEM((t + 2 * CONV_HALO, CONV_W), f32), pltpu.VMEM((t, CONV_W), f32)],
        compiler_params=_cparams(("parallel",)),
        name="merge",
    )(x2, u, u, u, gates, *os_, *ls_, *weights)


def _candidate_slabs():
    slabs = [("row", 0, 0, 16), ("row", 1, 0, 8)]
    for j in range(PEER_TOPK):
        hi = PEER_TOPK // (j + 1)
        if hi > 2:
            slabs.append(("col", j, 2, hi))
    return slabs


def _route_kernel(qp_ref, keys_ref, ids_ref, gate_ref, val_ref, idx_ref, best_ref, idt_ref, gt_ref):
    t = T_PROJ
    k_iota = lax.broadcasted_iota(i32, (N_SUBKEYS, t), 0)
    for hc in range(2 * PEER_HEADS):
        q = qp_ref[:, hc * N_SUBKEYS:(hc + 1) * N_SUBKEYS]
        s = lax.dot_general(keys_ref[hc], q, (((1,), (1,)), ((), ())), preferred_element_type=f32)
        for r in range(PEER_TOPK):
            m = jnp.max(s, axis=0, keepdims=True)
            am = jnp.min(jnp.where(s == m, k_iota, N_SUBKEYS), axis=0, keepdims=True)
            s = jnp.where(k_iota == am, -jnp.inf, s)
            val_ref[hc, pl.ds(r, 1), :] = m
            idx_ref[hc, pl.ds(r, 1), :] = am

    r_iota = lax.broadcasted_iota(i32, (PEER_TOPK, t), 0)
    slabs = _candidate_slabs()
    for h in range(PEER_HEADS):
        v0 = val_ref[2 * h]
        v1 = val_ref[2 * h + 1]
        i0 = idx_ref[2 * h]
        i1 = idx_ref[2 * h + 1]
        cands, flats, eids = [], [], []
        for kind, fixed, lo, hi in slabs:
            if kind == "row":
                c = v0[fixed:fixed + 1, :] + v1
                fl = fixed * PEER_TOPK + r_iota
                ei = i0[fixed:fixed + 1, :] * N_SUBKEYS + i1
            else:
                c = v0 + v1[fixed:fixed + 1, :]
                fl = r_iota * PEER_TOPK + fixed
                ei = i0 * N_SUBKEYS + i1[fixed:fixed + 1, :]
            valid = (r_iota >= lo) & (r_iota < hi)
            cands.append(jnp.where(valid, c, -jnp.inf))
            flats.append(fl)
            eids.append(ei)
        big = PEER_TOPK * PEER_TOPK
        for r in range(PEER_TOPK):
            m = functools.reduce(jnp.maximum, cands)
            m = jnp.max(m, axis=0, keepdims=True)
            fsel = functools.reduce(jnp.minimum, [jnp.where(c == m, fl, big) for c, fl in zip(cands, flats)])
            fsel = jnp.min(fsel, axis=0, keepdims=True)
            hit = [fl == fsel for fl in flats]
            eid = functools.reduce(jnp.maximum, [jnp.where(hh, ei, -1) for hh, ei in zip(hit, eids)])
            eid = jnp.max(eid, axis=0, keepdims=True)
            cands = [jnp.where(hh, -jnp.inf, c) for hh, c in zip(hit, cands)]
            best_ref[pl.ds(r, 1), :] = m
            idt_ref[pl.ds(h * PEER_TOPK + r, 1), :] = eid
        b = best_ref[...]
        e = jnp.exp(b - jnp.max(b, axis=0, keepdims=True))
        gt_ref[h * PEER_TOPK:(h + 1) * PEER_TOPK, :] = e / jnp.sum(e, axis=0, keepdims=True)
    ids_ref[...] = idt_ref[...].T
    gate_ref[...] = gt_ref[...].T


def _route(qp, keys_bf):
    n, qw = qp.shape
    t = T_PROJ
    return pl.pallas_call(
        _route_kernel,
        grid=(n // t,),
        in_specs=[pl.BlockSpec((t, qw), lambda i: (i, 0)), _const_spec(keys_bf.shape)],
        out_specs=[pl.BlockSpec((t, PEER_SLOTS), lambda i: (i, 0))] * 2,
        out_shape=[jax.ShapeDtypeStruct((n, PEER_SLOTS), i32), jax.ShapeDtypeStruct((n, PEER_SLOTS), f32)],
        scratch_shapes=[pltpu.VMEM((2 * PEER_HEADS, PEER_TOPK, t), f32), pltpu.VMEM((2 * PEER_HEADS, PEER_TOPK, t), i32),
                        pltpu.VMEM((PEER_TOPK, t), f32), pltpu.VMEM((PEER_SLOTS, t), i32),
                        pltpu.VMEM((PEER_SLOTS, t), f32)],
        compiler_params=_cparams(("parallel",)),
        name="route",
    )(qp, keys_bf)


def _pack_table(tbl):
    bits = lax.bitcast_convert_type(tbl.astype(bf16), jnp.uint16).astype(jnp.uint32)
    half = D_MODEL // 2
    words = bits[:, :half] | (bits[:, half:] << 16)
    return lax.bitcast_convert_type(words, i32).reshape(tbl.shape[0], ROW_WORDS, LANES)


def _unpack(words):
    lo = pltpu.bitcast(words << 16, f32)
    hi = pltpu.bitcast(words & jnp.int32(-65536), f32)
    return lo, hi


def _split2(x):
    hi = x.astype(bf16)
    lo = (x - hi.astype(f32)).astype(bf16)
    return hi, lo


def _gather_chunk(ids_ref, tbl_ref, tile_ref, c, base):
    e = None
    for k in range(GATHER_CHUNK):
        e = ids_ref[base + k]
        tile_ref[c, k * ROW_WORDS:(k + 1) * ROW_WORDS, :] = tbl_ref[e]
    return e


def _two_token_pipeline(ids_ref, tbl_ref, tile_a, tile_b, consume, init):
    tile_b[...] = jnp.zeros(tile_b.shape, i32)
    last = T_PEER - 1

    def half_step(t_gather, gather_tile, t_consume, consume_tile, dep):
        base = jnp.minimum(t_gather, last) * PEER_SLOTS
        tc = jnp.clip(t_consume, 0, last)
        acc = init
        for c in range(N_CHUNKS):
            dep = _gather_chunk(ids_ref, tbl_ref, gather_tile, c, base + c * GATHER_CHUNK + (dep >> 31))
            acc = consume(consume_tile, c, tc, acc)
        return dep

    def pairs_step(i, dep):
        for p in range(PAIRS_PER_STEP):
            even = 2 * (PAIRS_PER_STEP * i + p)
            dep = half_step(even, tile_a, even - 1, tile_b, dep)
            dep = half_step(even + 1, tile_b, even, tile_a, dep)
        return dep

    lax.fori_loop(0, T_PEER // (2 * PAIRS_PER_STEP) + 1, pairs_step, jnp.int32(0))


def _tile_scratch():
    return pltpu.VMEM((N_CHUNKS, GATHER_CHUNK * ROW_WORDS, LANES), i32)


def _peer_u_kernel(ids_ref, xr_ref, g_ref, tbl_ref, w_ref, tile_a, tile_b):
    crow = GATHER_CHUNK * ROW_WORDS
    half = crow // 2
    kk = lax.broadcasted_iota(i32, (2 * LANES, LANES), 0)
    nn = lax.broadcasted_iota(i32, (2 * LANES, LANES), 1)
    summer = ((kk < LANES) == (nn % GATHER_CHUNK < GATHER_CHUNK // 2)).astype(bf16)
    rr = lax.broadcasted_iota(i32, (half, LANES), 0)
    ll = lax.broadcasted_iota(i32, (half, LANES), 1)
    own_row = ll % (GATHER_CHUNK // 2) == rr // ROW_WORDS
    lane_chunk = ll // GATHER_CHUNK

    def consume(tile, c, t, acc):
        xt = xr_ref[pl.ds(pl.multiple_of(t * SUBLANES, SUBLANES), SUBLANES), :]
        xlo = xt[0:ROW_WORDS]
        xhi = xt[ROW_WORDS:2 * ROW_WORDS]
        x2lo = pltpu.repeat(jnp.concatenate([xlo, xlo], axis=0), crow // SUBLANES, axis=0)
        x2hi = pltpu.repeat(jnp.concatenate([xhi, xhi], axis=0), crow // SUBLANES, axis=0)
        lo, hi = _unpack(tile[c])
        p = lo * x2lo + hi * x2hi
        p_hi, p_lo = _split2(jnp.concatenate([p[0:half], p[half:crow]], axis=1))
        rs = jnp.dot(p_hi, summer, preferred_element_type=f32) + jnp.dot(p_lo, summer, preferred_element_type=f32)
        acc = acc + jnp.sum(jnp.where(own_row & (lane_chunk == c), rs, 0.0), axis=0, keepdims=True)
        if c == N_CHUNKS - 1:
            gelu = 0.5 * acc * (1.0 + lax.erf(acc * (2.0 ** -0.5)))
            w_ref[t] = g_ref[t] * gelu
        return acc

    _two_token_pipeline(ids_ref, tbl_ref, tile_a, tile_b, consume, jnp.zeros((1, PEER_SLOTS), f32))


def _peer_u(ids_flat, h2r, gate3, tbl, tok_off, n_tok):
    t = T_PEER
    b0 = tok_off // t
    return pl.pallas_call(
        _peer_u_kernel,
        grid=(n_tok // t,),
        in_specs=[pl.BlockSpec((t * PEER_SLOTS,), lambda i: (i + b0,), memory_space=pltpu.SMEM),
                  pl.BlockSpec((t * SUBLANES, LANES), lambda i: (i + b0, 0)),
                  pl.BlockSpec((t, 1, PEER_SLOTS), lambda i: (i + b0, 0, 0)),
                  _const_spec(tbl.shape)],
        out_specs=pl.BlockSpec((t, 1, PEER_SLOTS), lambda i: (i, 0, 0)),
        out_shape=jax.ShapeDtypeStruct((n_tok, 1, PEER_SLOTS), f32),
        scratch_shapes=[_tile_scratch(), _tile_scratch()],
        compiler_params=_cparams(("parallel",)),
        name="peer_u",
    )(ids_flat, h2r, gate3, tbl)


def _peer_v_sc(ids, w, tbl_words, tok_off):
    n_tok = w.shape[0]
    tpw = n_tok // SC_WORKERS
    lanes = SC_LANES
    mesh = plsc.VectorSubcoreMesh(core_axis_name="c", subcore_axis_name="s")

    @functools.partial(
        pl.kernel, mesh=mesh,
        out_type=jax.ShapeDtypeStruct((n_tok, D_MODEL), f32),
        scratch_types=[
            pltpu.VMEM((SC_HALF,), i32), pltpu.VMEM((SC_HALF,), i32),
            pltpu.VMEM((SC_HALF, ROW_WORDS_FLAT), i32), pltpu.VMEM((SC_HALF, ROW_WORDS_FLAT), i32),
            pltpu.VMEM((PEER_SLOTS,), f32),
            pltpu.VMEM((D_MODEL,), f32),
            pltpu.SemaphoreType.DMA, pltpu.SemaphoreType.DMA,
        ],
        compiler_params=dataclasses.replace(pltpu.CompilerParams(), needs_layout_passes=False),
        name="peer_v_sc",
    )
    def run(ids_hbm, w_hbm, tbl_hbm, out_hbm, idx_a, idx_b, rows_a, rows_b, w_v, out_v, sem_a, sem_b):
        wid = lax.axis_index("s") * SC_CORES + lax.axis_index("c")
        base = wid * tpw

        def accumulate(rows, w_off):
            for db in range(ROW_WORDS_FLAT // SC_DIM_BLOCK):
                def group_body(g, carry):
                    accs = []
                    for wc in range(SC_DIM_BLOCK // lanes):
                        accs.append(out_v[pl.ds(db * SC_DIM_BLOCK + wc * lanes, lanes)])
                        accs.append(out_v[pl.ds(ROW_WORDS_FLAT + db * SC_DIM_BLOCK + wc * lanes, lanes)])
                    j0 = g * SC_ROW_GROUP
                    wchunk = w_v[pl.ds(pl.multiple_of((w_off + j0) // lanes * lanes, lanes), lanes)]
                    sub = (w_off + j0) % lanes
                    for r in range(SC_ROW_GROUP):
                        wj = jnp.take(wchunk, jnp.full((lanes,), sub + r, i32))
                        for wc in range(SC_DIM_BLOCK // lanes):
                            word = rows[j0 + r, pl.ds(db * SC_DIM_BLOCK + wc * lanes, lanes)]
                            lo = lax.bitcast_convert_type(word << 16, f32)
                            hi = lax.bitcast_convert_type(word & jnp.int32(-65536), f32)
                            accs[2 * wc] = accs[2 * wc] + wj * lo
                            accs[2 * wc + 1] = accs[2 * wc + 1] + wj * hi
                    for wc in range(SC_DIM_BLOCK // lanes):
                        out_v[pl.ds(db * SC_DIM_BLOCK + wc * lanes, lanes)] = accs[2 * wc]
                        out_v[pl.ds(ROW_WORDS_FLAT + db * SC_DIM_BLOCK + wc * lanes, lanes)] = accs[2 * wc + 1]
                    return carry
                lax.fori_loop(0, SC_HALF // SC_ROW_GROUP, group_body, 0)

        def token(i, carry):
            t = base + i
            pltpu.sync_copy(ids_hbm.at[tok_off + t, pl.ds(0, SC_HALF)], idx_a)
            pltpu.sync_copy(ids_hbm.at[tok_off + t, pl.ds(SC_HALF, SC_HALF)], idx_b)
            copy_a = pltpu.async_copy(tbl_hbm.at[idx_a], rows_a, sem_a)
            copy_b = pltpu.async_copy(tbl_hbm.at[idx_b], rows_b, sem_b)
            pltpu.sync_copy(w_hbm.at[t], w_v)
            for q in range(D_MODEL // lanes):
                out_v[pl.ds(q * lanes, lanes)] = jnp.zeros((lanes,), f32)
            copy_a.wait()
            accumulate(rows_a, 0)
            copy_b.wait()
            accumulate(rows_b, SC_HALF)
            pltpu.sync_copy(out_v, out_hbm.at[t])
            return carry

        lax.fori_loop(0, tpw, token, 0)

    return run(ids, w, tbl_words)


def _final_kernel(x1_ref, p_ref, g_ref, y_ref):
    x = x1_ref[...] + p_ref[...]
    ms = jnp.mean(x * x, axis=-1, keepdims=True)
    y_ref[...] = x * lax.rsqrt(ms + EPS) * g_ref[...]


def _final(x1, peer_out, final_g, tok_off):
    n = peer_out.shape[0]
    t = T_PROJ
    b0 = tok_off // t
    row = pl.BlockSpec((t, D_MODEL), lambda i: (i, 0))
    return pl.pallas_call(
        _final_kernel,
        grid=(n // t,),
        in_specs=[pl.BlockSpec((t, D_MODEL), lambda i: (i + b0, 0)), row, _const_spec((1, D_MODEL))],
        out_specs=row,
        out_shape=jax.ShapeDtypeStruct((n, D_MODEL), f32),
        compiler_params=_cparams(("parallel",)),
        name="final_norm",
    )(x1, peer_out, final_g)


def _rotary_tables(seq):
    half = ROT_DIM // 2
    inv = ROPE_THETA ** (-jnp.arange(half, dtype=f32) * 2.0 / ROT_DIM)
    ang = jnp.arange(seq, dtype=jnp.int32).astype(f32)[:, None] * inv[None, :]
    cos = jnp.cos(ang)
    sin = jnp.sin(ang)
    pad = HEAD_DIM - ROT_DIM
    one = jnp.ones((seq, pad), f32)
    zero = jnp.zeros((seq, pad), f32)
    zh = jnp.zeros((seq, half), f32)
    cos_h = jnp.concatenate([cos, cos, one], axis=1)
    sa_h = jnp.concatenate([-sin, zh, zero], axis=1)
    sb_h = jnp.concatenate([zh, sin, zero], axis=1)
    rep = LANES // HEAD_DIM
    return tuple(jnp.tile(t, (1, rep)) for t in (cos_h, sa_h, sb_h))


def _front_ops(x, params, tables, st):
    (norm1_g, w_in_bf, b_gate, merge_w, keys_bf, _, _, _) = params
    batch, seq, _ = x.shape
    n = batch * seq
    x2 = x.reshape(n, D_MODEL)

    def inproj():
        st["q"], st["k"], st["v"], st["u"], st["gates"] = _inproj(x2, seq, norm1_g, w_in_bf, b_gate, *tables)
        st["o"], st["lse"] = [], []

    def attention(g):
        o, lse = _attention_group(st["q"][g], st["k"][g], st["v"][g], batch, seq, g)
        st["o"].append(o)
        st["lse"].append(lse)

    def merge():
        st["x1"], st["h2r"], st["qp"] = _merge(x2, seq, st["u"], st["gates"], st["o"], st["lse"], merge_w)

    def route():
        ids, gate = _route(st["qp"], keys_bf)
        st["ids"], st["gate3"], st["n"] = ids, gate.reshape(n, 1, PEER_SLOTS), n

    return [inproj] + [functools.partial(attention, g) for g in range(N_GROUPS)] + [merge, route]


def _peer_ops(params, st, pending):
    u_tbl, v_tbl = params[5], params[6]

    def chunk(off, n_chunk):
        ids = st["ids"]
        w3 = _peer_u(ids.reshape(st["n"] * PEER_SLOTS), st["h2r"], st["gate3"], u_tbl, off, n_chunk)
        w3, st["gate3"] = lax.optimization_barrier((w3, st["gate3"]))
        pv = _peer_v_sc(ids, w3.reshape(n_chunk, PEER_SLOTS), v_tbl, off)
        pending.append((st["x1"], pv, off))

    ops, off = [], 0
    for share in PEER_CHUNK_SHARES:
        n_chunk = st["tokens"] * share // sum(PEER_CHUNK_SHARES)
        ops.append(functools.partial(chunk, off, n_chunk))
        off += n_chunk
    return ops


def kernel(x_prompt, x_sample, norm1_g, w_in, b_gate, w_attn_up, conv_dw_w, conv_dw_b, conv_ln_g, conv_ln_b,
           conv_pw_w, conv_pw_b, w_out, norm2_g, peer_wq, peer_keys, peer_u, peer_v, final_g):
    assert w_in.shape[0] == 1, "one encoder layer followed by the final norm"
    row = lambda a: a.reshape(1, -1)
    tables = _rotary_tables(max(x_prompt.shape[1], x_sample.shape[1]))
    l = 0
    merge_w = (conv_dw_w[l], row(conv_dw_b[l]), row(conv_ln_g[l]), row(conv_ln_b[l]),
               conv_pw_w[l].astype(bf16), row(conv_pw_b[l]), w_attn_up[l].astype(bf16),
               w_out[l].astype(bf16), row(norm2_g[l]), peer_wq[l].astype(bf16))
    keys_bf = peer_keys[l].astype(bf16).reshape(2 * PEER_HEADS, N_SUBKEYS, -1)
    params = (row(norm1_g[l]), w_in[l].astype(bf16), row(b_gate[l]), merge_w, keys_bf,
              _pack_table(peer_u[l]), _pack_table(peer_v[l]).reshape(N_EXPERTS, ROW_WORDS_FLAT), row(final_g))

    subs = []
    for x in (x_prompt, x_sample):
        step = x.shape[0] // BATCH_SUBGROUPS
        subs += [x[b:b + step] for b in range(0, x.shape[0], step)]
    states = [dict(tokens=sub.shape[0] * sub.shape[1]) for sub in subs]
    pendings = [[] for _ in subs]
    for op in _front_ops(subs[0], params, tables, states[0]):
        op()
    for i in range(len(subs)):
        peer = _peer_ops(params, states[i], pendings[i])
        front = _front_ops(subs[i + 1], params, tables, states[i + 1]) if i + 1 < len(subs) else []
        for j in range(max(len(peer), len(front))):
            if j < len(peer):
                peer[j]()
            if j < len(front):
                front[j]()

    def finish(x, pending):
        ys = [_final(x1, pv, params[-1], off) for x1, pv, off in pending]
        return jnp.concatenate(ys, axis=0).reshape(x.shape)

    half = BATCH_SUBGROUPS
    return (finish(x_prompt, sum(pendings[:half], [])), finish(x_sample, sum(pendings[half:], [])))
```

```python
import dataclasses
import functools
import math

import numpy as np
import jax
import jax.numpy as jnp
from jax import lax
from jax.experimental import pallas as pl
from jax.experimental.pallas import tpu as pltpu
from jax.experimental.pallas import tpu_sc as plsc

f32 = jnp.float32
bf16 = jnp.bfloat16
i32 = jnp.int32

D_MODEL = 1024
HEAD_DIM = 64
N_GROUPS = 3
HEADS_PER_GROUP = 4
GROUP_W = HEADS_PER_GROUP * HEAD_DIM
ATTN_W = N_GROUPS * GROUP_W
WINDOWS = (128, 512, 2048)
DILATIONS = (1, 4, 16)
HALF_WIN = 64
ROT_DIM = HEAD_DIM // 4
ROPE_THETA = 500000.0
NEG_INF = -1e30
CONV_W = 512
CONV_K = 31
CONV_HALO = 16
PEER_HEADS = 8
N_SUBKEYS = 128
N_EXPERTS = N_SUBKEYS * N_SUBKEYS
PEER_TOPK = 16
PEER_SLOTS = PEER_HEADS * PEER_TOPK
EPS = 1e-6

LANES = 128
SUBLANES = 8
ROW_WORDS = D_MODEL // 2 // LANES
VMEM_LIMIT = 56 * 1024 * 1024

T_PROJ = 256
T_PEER = 256
PAIRS_PER_STEP = 8
GATHER_CHUNK = 32
N_CHUNKS = (8 * 16) // GATHER_CHUNK
BATCH_SUBGROUPS = 4
PEER_CHUNK_SHARES = (1, 1, 2, 4, 4, 4)

SC_CORES = 2
SC_SUBCORES = 16
SC_LANES = 16
SC_WORKERS = SC_CORES * SC_SUBCORES
ROW_WORDS_FLAT = D_MODEL // 2
SC_HALF = PEER_SLOTS // 2
SC_DIM_BLOCK = 128
SC_ROW_GROUP = 8
Q_SUB = 128


def _cparams(sem):
    return pltpu.CompilerParams(dimension_semantics=sem, vmem_limit_bytes=VMEM_LIMIT)


def _const_spec(shape):
    nd = len(shape)
    return pl.BlockSpec(shape, lambda *_: (0,) * nd, pipeline_mode=pl.Buffered(1))


def _inproj_kernel(x_ref, g_ref, w_ref, bg_ref, cos_ref, sa_ref, sb_ref, *refs):
    qkv_refs = (refs[0:N_GROUPS], refs[N_GROUPS:2 * N_GROUPS], refs[2 * N_GROUPS:3 * N_GROUPS])
    u_ref, gate_ref, stage_ref = refs[3 * N_GROUPS:]
    x = x_ref[...]
    ms = jnp.mean(x * x, axis=-1, keepdims=True)
    h = (x * lax.rsqrt(ms + EPS) * g_ref[...]).astype(bf16)

    def proj(lo, hi):
        return jnp.dot(h, w_ref[:, lo:hi], preferred_element_type=f32)

    cos = cos_ref[...]
    sa = sa_ref[...]
    sb = sb_ref[...]

    def rotary(tc, scale):
        r = tc * cos + pltpu.roll(tc, LANES - ROT_DIM // 2, 1) * sa + pltpu.roll(tc, ROT_DIM // 2, 1) * sb
        return r * scale

    def emit(t, out_refs, fn):
        for c in range(ATTN_W // LANES):
            g, half = divmod(c, GROUP_W // LANES)
            d = DILATIONS[g]
            chunk = fn(t[:, c * LANES:(c + 1) * LANES])
            if d == 1:
                out_refs[g][:, half * LANES:(half + 1) * LANES] = chunk.astype(bf16)
                continue
            stage_ref[...] = chunk
            for r in range(d):
                col = r * GROUP_W + half * LANES
                out_refs[g][:, col:col + LANES] = stage_ref[pl.ds(r, T_PROJ // d, stride=d), :].astype(bf16)

    emit(proj(0, ATTN_W), qkv_refs[0], lambda tc: rotary(tc, HEAD_DIM ** -0.5))
    emit(proj(ATTN_W, 2 * ATTN_W), qkv_refs[1], lambda tc: rotary(tc, 1.0))
    emit(proj(2 * ATTN_W, 3 * ATTN_W), qkv_refs[2], lambda tc: tc)
    c0 = 3 * ATTN_W
    a = proj(c0, c0 + CONV_W)
    b = proj(c0 + CONV_W, c0 + 2 * CONV_W)
    u_ref[...] = a * jax.nn.sigmoid(b)
    gates = proj(c0 + 2 * CONV_W, c0 + 2 * CONV_W + 2 * D_MODEL) + bg_ref[...]
    gate_ref[...] = jax.nn.sigmoid(gates).astype(bf16)


def _inproj(x2, seq, norm1_g, w_in_bf, b_gate, cos_t, sa_t, sb_t):
    n = x2.shape[0]
    nsb = seq // T_PROJ
    in_cols = w_in_bf.shape[1]
    row = lambda w: pl.BlockSpec((T_PROJ, w), lambda i: (i, 0))
    pos = pl.BlockSpec((T_PROJ, LANES), lambda i: (i % nsb, 0))
    grp_specs = [pl.BlockSpec((T_PROJ // d, d * GROUP_W), lambda i: (i, 0)) for d in DILATIONS]
    grp_shapes = [jax.ShapeDtypeStruct((n // d, d * GROUP_W), bf16) for d in DILATIONS]
    outs = pl.pallas_call(
        _inproj_kernel,
        grid=(n // T_PROJ,),
        in_specs=[row(D_MODEL), _const_spec((1, D_MODEL)), _const_spec((D_MODEL, in_cols)),
                  _const_spec((1, 2 * D_MODEL)), pos, pos, pos],
        out_specs=grp_specs * 3 + [row(CONV_W), row(2 * D_MODEL)],
        out_shape=grp_shapes * 3
        + [jax.ShapeDtypeStruct((n, CONV_W), f32), jax.ShapeDtypeStruct((n, 2 * D_MODEL), bf16)],
        scratch_shapes=[pltpu.VMEM((T_PROJ, LANES), f32)],
        compiler_params=_cparams(("parallel",)),
        name="inproj",
    )(x2, norm1_g, w_in_bf, b_gate, cos_t, sa_t, sb_t)
    q, k, v = outs[0:N_GROUPS], outs[N_GROUPS:2 * N_GROUPS], outs[2 * N_GROUPS:3 * N_GROUPS]
    return q, k, v, outs[3 * N_GROUPS], outs[3 * N_GROUPS + 1]


def _attn_kernel(q_ref, kp_ref, kc_ref, kn_ref, vp_ref, vc_ref, vn_ref, o_ref, lse_ref,
                 kw_ref, vw_ref, *, tq, n_rows):
    i = pl.program_id(2)
    kw_ref[0:HALF_WIN] = kp_ref[0]
    kw_ref[HALF_WIN:HALF_WIN + tq] = kc_ref[0]
    kw_ref[HALF_WIN + tq:2 * HALF_WIN + tq] = kn_ref[0]
    vw_ref[0:HALF_WIN] = vp_ref[0]
    vw_ref[HALF_WIN:HALF_WIN + tq] = vc_ref[0]
    vw_ref[HALF_WIN + tq:2 * HALF_WIN + tq] = vn_ref[0]

    win = Q_SUB + 2 * HALF_WIN
    qi = lax.broadcasted_iota(i32, (Q_SUB, win), 0)
    kj = lax.broadcasted_iota(i32, (Q_SUB, win), 1)
    band = (kj - qi >= 0) & (kj - qi <= 2 * HALF_WIN)
    head_of_lane = lax.broadcasted_iota(i32, (1, GROUP_W), 1) // HEAD_DIM

    for s in range(tq // Q_SUB):
        qs = q_ref[0, s * Q_SUB:(s + 1) * Q_SUB, :]
        kwin = kw_ref[s * Q_SUB:s * Q_SUB + win, :]
        vwin = vw_ref[s * Q_SUB:s * Q_SUB + win, :]
        key_row = i * tq + (s * Q_SUB - HALF_WIN) + kj
        ok = band & (key_row >= 0) & (key_row < n_rows)
        o_acc = jnp.zeros((Q_SUB, GROUP_W), f32)
        l_acc = jnp.zeros((Q_SUB, GROUP_W), f32)
        for h in range(HEADS_PER_GROUP):
            hm = head_of_lane == h
            qh = jnp.where(hm, qs, jnp.zeros_like(qs))
            sc = lax.dot_general(qh, kwin, (((1,), (1,)), ((), ())), preferred_element_type=f32)
            sc = jnp.where(ok, sc, NEG_INF)
            m = jnp.max(sc, axis=-1, keepdims=True)
            p = jnp.exp(sc - m)
            den = jnp.sum(p, axis=-1, keepdims=True)
            pv = jnp.dot(p.astype(bf16), vwin, preferred_element_type=f32)
            o_acc = jnp.where(hm, pv / den, o_acc)
            l_acc = jnp.where(hm, m + jnp.log(den), l_acc)
        o_ref[0, s * Q_SUB:(s + 1) * Q_SUB, :] = o_acc
        lse_ref[0, s * Q_SUB:(s + 1) * Q_SUB, :] = l_acc


def _attention_group(q, k, v, batch, seq, g):
    d = DILATIONS[g]
    n_rows = seq // d
    tq = min(512, n_rows)
    nblk = n_rows // tq
    hb = tq // HALF_WIN
    n_halo_blocks = n_rows // HALF_WIN
    view = lambda t: t.reshape(batch, n_rows, d * GROUP_W)
    cur = pl.BlockSpec((1, tq, GROUP_W), lambda b, r, i: (b, i, r))
    prev = pl.BlockSpec((1, HALF_WIN, GROUP_W), lambda b, r, i: (b, jnp.maximum(i * hb - 1, 0), r))
    nxt = pl.BlockSpec((1, HALF_WIN, GROUP_W),
                       lambda b, r, i: (b, jnp.minimum((i + 1) * hb, n_halo_blocks - 1), r))
    o, lse = pl.pallas_call(
        functools.partial(_attn_kernel, tq=tq, n_rows=n_rows),
        grid=(batch, d, nblk),
        in_specs=[cur, prev, cur, nxt, prev, cur, nxt],
        out_specs=[cur, cur],
        out_shape=[jax.ShapeDtypeStruct((batch, n_rows, d * GROUP_W), f32)] * 2,
        scratch_shapes=[pltpu.VMEM((tq + 2 * HALF_WIN, GROUP_W), bf16)] * 2,
        compiler_params=_cparams(("parallel", "parallel", "parallel")),
        name=f"attn_g{g}",
    )(view(q), view(k), view(k), view(k), view(v), view(v), view(v))
    return o.reshape(batch * seq, GROUP_W), lse.reshape(batch * seq, GROUP_W)


def _merge_kernel(x_ref, up_ref, uc_ref, un_ref, gate_ref,
                  o0_ref, o1_ref, o2_ref, l0_ref, l1_ref, l2_ref,
                  dww_ref, dwb_ref, lng_ref, lnb_ref, pww_ref, pwb_ref, wup_ref, wout_ref, n2g_ref, wq_ref,
                  x1_ref, h2_ref, qp_ref, ue_ref, cv_ref, *, nsb):
    i = pl.program_id(0)
    t = T_PROJ
    first = (i % nsb) == 0
    last = (i % nsb) == nsb - 1
    ue_ref[0:CONV_HALO] = jnp.where(first, 0.0, up_ref[...])
    ue_ref[CONV_HALO:CONV_HALO + t] = uc_ref[...]
    ue_ref[CONV_HALO + t:2 * CONV_HALO + t] = jnp.where(last, 0.0, un_ref[...])

    rc = 32
    off = CONV_HALO - CONV_K // 2
    for r0 in range(0, t, rc):
        acc = jnp.zeros((rc, CONV_W), f32)
        for j in range(CONV_K):
            acc = acc + ue_ref[r0 + off + j:r0 + off + j + rc, :] * dww_ref[j:j + 1, :]
        cv_ref[r0:r0 + rc, :] = acc
    c = cv_ref[...] + dwb_ref[...]
    mu = jnp.mean(c, axis=-1, keepdims=True)
    cc = c - mu
    var = jnp.mean(cc * cc, axis=-1, keepdims=True)
    un = cc * lax.rsqrt(var + EPS) * lng_ref[...] + lnb_ref[...]
    sw = un * jax.nn.sigmoid(un)
    conv = jnp.dot(sw.astype(bf16), pww_ref[...], preferred_element_type=f32) + pwb_ref[...]

    l0 = l0_ref[...]
    l1 = l1_ref[...]
    l2 = l2_ref[...]
    lm = jnp.maximum(jnp.maximum(l0, l1), l2)
    e0 = jnp.exp(l0 - lm)
    e1 = jnp.exp(l1 - lm)
    e2 = jnp.exp(l2 - lm)
    comb = (e0 * o0_ref[...] + e1 * o1_ref[...] + e2 * o2_ref[...]) / (e0 + e1 + e2)
    attn = jnp.dot(comb.astype(bf16), wup_ref[...], preferred_element_type=f32)

    g_attn = gate_ref[:, 0:D_MODEL].astype(f32)
    g_conv = gate_ref[:, D_MODEL:2 * D_MODEL].astype(f32)
    mixed = g_attn * attn + g_conv * conv
    x1 = x_ref[...] + jnp.dot(mixed.astype(bf16), wout_ref[...], preferred_element_type=f32)
    x1_ref[...] = x1
    ms = jnp.mean(x1 * x1, axis=-1, keepdims=True)
    h2 = x1 * lax.rsqrt(ms + EPS) * n2g_ref[...]
    for c in range(D_MODEL // LANES):
        h2_ref[pl.ds(c, t, stride=SUBLANES), :] = h2[:, c * LANES:(c + 1) * LANES]
    qp_ref[...] = jnp.dot(h2.astype(bf16), wq_ref[...], preferred_element_type=f32).astype(bf16)


def _merge(x2, seq, u, gates, os_, ls_, weights):
    n = x2.shape[0]
    t = T_PROJ
    nsb = seq // t
    hb = t // CONV_HALO
    nhalo = n // CONV_HALO
    row = lambda w: pl.BlockSpec((t, w), lambda i: (i, 0))
    prev = pl.BlockSpec((CONV_HALO, CONV_W), lambda i: (jnp.maximum(i * hb - 1, 0), 0))
    nxt = pl.BlockSpec((CONV_HALO, CONV_W), lambda i: (jnp.minimum((i + 1) * hb, nhalo - 1), 0))
    wspecs = [_const_spec(w.shape) for w in weights]
    qw = weights[-1].shape[1]
    return pl.pallas_call(
        functools.partial(_merge_kernel, nsb=nsb),
        grid=(n // t,),
        in_specs=[row(D_MODEL), prev, row(CONV_W), nxt, row(2 * D_MODEL)] + [row(GROUP_W)] * 6 + wspecs,
        out_specs=[row(D_MODEL), pl.BlockSpec((t * SUBLANES, LANES), lambda i: (i, 0)), row(qw)],
        out_shape=[jax.ShapeDtypeStruct((n, D_MODEL), f32), jax.ShapeDtypeStruct((n * SUBLANES, LANES), f32),
                   jax.ShapeDtypeStruct((n, qw), bf16)],
        scratch_shapes=[pltpu.VMEM((t + 2 * CONV_HALO, CONV_W), f32), pltpu.VMEM((t, CONV_W), f32)],
        compiler_params=_cparams(("parallel",)),
        name="merge",
    )(x2, u, u, u, gates, *os_, *ls_, *weights)


def _candidate_slabs():
    slabs = [("row", 0, 0, 16), ("row", 1, 0, 8)]
    for j in range(PEER_TOPK):
        hi = PEER_TOPK // (j + 1)
        if hi > 2:
            slabs.append(("col", j, 2, hi))
    return slabs


def _route_kernel(qp_ref, keys_ref, ids_ref, gate_ref, val_ref, idx_ref, best_ref, idt_ref, gt_ref):
    t = T_PROJ
    k_iota = lax.broadcasted_iota(i32, (N_SUBKEYS, t), 0)
    for hc in range(2 * PEER_HEADS):
        q = qp_ref[:, hc * N_SUBKEYS:(hc + 1) * N_SUBKEYS]
        s = lax.dot_general(keys_ref[hc], q, (((1,), (1,)), ((), ())), preferred_element_type=f32)
        for r in range(PEER_TOPK):
            m = jnp.max(s, axis=0, keepdims=True)
            am = jnp.min(jnp.where(s == m, k_iota, N_SUBKEYS), axis=0, keepdims=True)
            s = jnp.where(k_iota == am, -jnp.inf, s)
            val_ref[hc, pl.ds(r, 1), :] = m
            idx_ref[hc, pl.ds(r, 1), :] = am

    r_iota = lax.broadcasted_iota(i32, (PEER_TOPK, t), 0)
    slabs = _candidate_slabs()
    for h in range(PEER_HEADS):
        v0 = val_ref[2 * h]
        v1 = val_ref[2 * h + 1]
        i0 = idx_ref[2 * h]
        i1 = idx_ref[2 * h + 1]
        cands, flats, eids = [], [], []
        for kind, fixed, lo, hi in slabs:
            if kind == "row":
                c = v0[fixed:fixed + 1, :] + v1
                fl = fixed * PEER_TOPK + r_iota
                ei = i0[fixed:fixed + 1, :] * N_SUBKEYS + i1
            else:
                c = v0 + v1[fixed:fixed + 1, :]
                fl = r_iota * PEER_TOPK + fixed
                ei = i0 * N_SUBKEYS + i1[fixed:fixed + 1, :]
            valid = (r_iota >= lo) & (r_iota < hi)
            cands.append(jnp.where(valid, c, -jnp.inf))
            flats.append(fl)
            eids.append(ei)
        big = PEER_TOPK * PEER_TOPK
        for r in range(PEER_TOPK):
            m = functools.reduce(jnp.maximum, cands)
            m = jnp.max(m, axis=0, keepdims=True)
            fsel = functools.reduce(jnp.minimum, [jnp.where(c == m, fl, big) for c, fl in zip(cands, flats)])
            fsel = jnp.min(fsel, axis=0, keepdims=True)
            hit = [fl == fsel for fl in flats]
            eid = functools.reduce(jnp.maximum, [jnp.where(hh, ei, -1) for hh, ei in zip(hit, eids)])
            eid = jnp.max(eid, axis=0, keepdims=True)
            cands = [jnp.where(hh, -jnp.inf, c) for hh, c in zip(hit, cands)]
            best_ref[pl.ds(r, 1), :] = m
            idt_ref[pl.ds(h * PEER_TOPK + r, 1), :] = eid
        b = best_ref[...]
        e = jnp.exp(b - jnp.max(b, axis=0, keepdims=True))
        gt_ref[h * PEER_TOPK:(h + 1) * PEER_TOPK, :] = e / jnp.sum(e, axis=0, keepdims=True)
    ids_ref[...] = idt_ref[...].T
    gate_ref[...] = gt_ref[...].T


def _route(qp, keys_bf):
    n, qw = qp.shape
    t = T_PROJ
    return pl.pallas_call(
        _route_kernel,
        grid=(n // t,),
        in_specs=[pl.BlockSpec((t, qw), lambda i: (i, 0)), _const_spec(keys_bf.shape)],
        out_specs=[pl.BlockSpec((t, PEER_SLOTS), lambda i: (i, 0))] * 2,
        out_shape=[jax.ShapeDtypeStruct((n, PEER_SLOTS), i32), jax.ShapeDtypeStruct((n, PEER_SLOTS), f32)],
        scratch_shapes=[pltpu.VMEM((2 * PEER_HEADS, PEER_TOPK, t), f32), pltpu.VMEM((2 * PEER_HEADS, PEER_TOPK, t), i32),
                        pltpu.VMEM((PEER_TOPK, t), f32), pltpu.VMEM((PEER_SLOTS, t), i32),
                        pltpu.VMEM((PEER_SLOTS, t), f32)],
        compiler_params=_cparams(("parallel",)),
        name="route",
    )(qp, keys_bf)


def _pack_table(tbl):
    bits = lax.bitcast_convert_type(tbl.astype(bf16), jnp.uint16).astype(jnp.uint32)
    half = D_MODEL // 2
    words = bits[:, :half] | (bits[:, half:] << 16)
    return lax.bitcast_convert_type(words, i32).reshape(tbl.shape[0], ROW_WORDS, LANES)


def _unpack(words):
    lo = pltpu.bitcast(words << 16, f32)
    hi = pltpu.bitcast(words & jnp.int32(-65536), f32)
    return lo, hi


def _split2(x):
    hi = x.astype(bf16)
    lo = (x - hi.astype(f32)).astype(bf16)
    return hi, lo


def _gather_chunk(ids_ref, tbl_ref, tile_ref, c, base):
    e = None
    for k in range(GATHER_CHUNK):
        e = ids_ref[base + k]
        tile_ref[c, k * ROW_WORDS:(k + 1) * ROW_WORDS, :] = tbl_ref[e]
    return e


def _two_token_pipeline(ids_ref, tbl_ref, tile_a, tile_b, consume, init):
    tile_b[...] = jnp.zeros(tile_b.shape, i32)
    last = T_PEER - 1

    def half_step(t_gather, gather_tile, t_consume, consume_tile, dep):
        base = jnp.minimum(t_gather, last) * PEER_SLOTS
        tc = jnp.clip(t_consume, 0, last)
        acc = init
        for c in range(N_CHUNKS):
            dep = _gather_chunk(ids_ref, tbl_ref, gather_tile, c, base + c * GATHER_CHUNK + (dep >> 31))
            acc = consume(consume_tile, c, tc, acc)
        return dep

    def pairs_step(i, dep):
        for p in range(PAIRS_PER_STEP):
            even = 2 * (PAIRS_PER_STEP * i + p)
            dep = half_step(even, tile_a, even - 1, tile_b, dep)
            dep = half_step(even + 1, tile_b, even, tile_a, dep)
        return dep

    lax.fori_loop(0, T_PEER // (2 * PAIRS_PER_STEP) + 1, pairs_step, jnp.int32(0))


def _tile_scratch():
    return pltpu.VMEM((N_CHUNKS, GATHER_CHUNK * ROW_WORDS, LANES), i32)


def _peer_u_kernel(ids_ref, xr_ref, g_ref, tbl_ref, w_ref, tile_a, tile_b):
    crow = GATHER_CHUNK * ROW_WORDS
    half = crow // 2
    kk = lax.broadcasted_iota(i32, (2 * LANES, LANES), 0)
    nn = lax.broadcasted_iota(i32, (2 * LANES, LANES), 1)
    summer = ((kk < LANES) == (nn % GATHER_CHUNK < GATHER_CHUNK // 2)).astype(bf16)
    rr = lax.broadcasted_iota(i32, (half, LANES), 0)
    ll = lax.broadcasted_iota(i32, (half, LANES), 1)
    own_row = ll % (GATHER_CHUNK // 2) == rr // ROW_WORDS
    lane_chunk = ll // GATHER_CHUNK

    def consume(tile, c, t, acc):
        xt = xr_ref[pl.ds(pl.multiple_of(t * SUBLANES, SUBLANES), SUBLANES), :]
        xlo = xt[0:ROW_WORDS]
        xhi = xt[ROW_WORDS:2 * ROW_WORDS]
        x2lo = pltpu.repeat(jnp.concatenate([xlo, xlo], axis=0), crow // SUBLANES, axis=0)
        x2hi = pltpu.repeat(jnp.concatenate([xhi, xhi], axis=0), crow // SUBLANES, axis=0)
        lo, hi = _unpack(tile[c])
        p = lo * x2lo + hi * x2hi
        p_hi, p_lo = _split2(jnp.concatenate([p[0:half], p[half:crow]], axis=1))
        rs = jnp.dot(p_hi, summer, preferred_element_type=f32) + jnp.dot(p_lo, summer, preferred_element_type=f32)
        acc = acc + jnp.sum(jnp.where(own_row & (lane_chunk == c), rs, 0.0), axis=0, keepdims=True)
        if c == N_CHUNKS - 1:
            gelu = 0.5 * acc * (1.0 + lax.erf(acc * (2.0 ** -0.5)))
            w_ref[t] = g_ref[t] * gelu
        return acc

    _two_token_pipeline(ids_ref, tbl_ref, tile_a, tile_b, consume, jnp.zeros((1, PEER_SLOTS), f32))


def _peer_u(ids_flat, h2r, gate3, tbl, tok_off, n_tok):
    t = T_PEER
    b0 = tok_off // t
    return pl.pallas_call(
        _peer_u_kernel,
        grid=(n_tok // t,),
        in_specs=[pl.BlockSpec((t * PEER_SLOTS,), lambda i: (i + b0,), memory_space=pltpu.SMEM),
                  pl.BlockSpec((t * SUBLANES, LANES), lambda i: (i + b0, 0)),
                  pl.BlockSpec((t, 1, PEER_SLOTS), lambda i: (i + b0, 0, 0)),
                  _const_spec(tbl.shape)],
        out_specs=pl.BlockSpec((t, 1, PEER_SLOTS), lambda i: (i, 0, 0)),
        out_shape=jax.ShapeDtypeStruct((n_tok, 1, PEER_SLOTS), f32),
        scratch_shapes=[_tile_scratch(), _tile_scratch()],
        compiler_params=_cparams(("parallel",)),
        name="peer_u",
    )(ids_flat, h2r, gate3, tbl)


def _peer_v_sc(ids, w, tbl_words, tok_off):
    n_tok = w.shape[0]
    tpw = n_tok // SC_WORKERS
    lanes = SC_LANES
    mesh = plsc.VectorSubcoreMesh(core_axis_name="c", subcore_axis_name="s")

    @functools.partial(
        pl.kernel, mesh=mesh,
        out_type=jax.ShapeDtypeStruct((n_tok, D_MODEL), f32),
        scratch_types=[
            pltpu.VMEM((SC_HALF,), i32), pltpu.VMEM((SC_HALF,), i32),
            pltpu.VMEM((SC_HALF, ROW_WORDS_FLAT), i32), pltpu.VMEM((SC_HALF, ROW_WORDS_FLAT), i32),
            pltpu.VMEM((PEER_SLOTS,), f32),
            pltpu.VMEM((D_MODEL,), f32),
            pltpu.SemaphoreType.DMA, pltpu.SemaphoreType.DMA,
        ],
        compiler_params=dataclasses.replace(pltpu.CompilerParams(), needs_layout_passes=False),
        name="peer_v_sc",
    )
    def run(ids_hbm, w_hbm, tbl_hbm, out_hbm, idx_a, idx_b, rows_a, rows_b, w_v, out_v, sem_a, sem_b):
        wid = lax.axis_index("s") * SC_CORES + lax.axis_index("c")
        base = wid * tpw

        def accumulate(rows, w_off):
            for db in range(ROW_WORDS_FLAT // SC_DIM_BLOCK):
                def group_body(g, carry):
                    accs = []
                    for wc in range(SC_DIM_BLOCK // lanes):
                        accs.append(out_v[pl.ds(db * SC_DIM_BLOCK + wc * lanes, lanes)])
                        accs.append(out_v[pl.ds(ROW_WORDS_FLAT + db * SC_DIM_BLOCK + wc * lanes, lanes)])
                    j0 = g * SC_ROW_GROUP
                    wchunk = w_v[pl.ds(pl.multiple_of((w_off + j0) // lanes * lanes, lanes), lanes)]
                    sub = (w_off + j0) % lanes
                    for r in range(SC_ROW_GROUP):
                        wj = jnp.take(wchunk, jnp.full((lanes,), sub + r, i32))
                        for wc in range(SC_DIM_BLOCK // lanes):
                            word = rows[j0 + r, pl.ds(db * SC_DIM_BLOCK + wc * lanes, lanes)]
                            lo = lax.bitcast_convert_type(word << 16, f32)
                            hi = lax.bitcast_convert_type(word & jnp.int32(-65536), f32)
                            accs[2 * wc] = accs[2 * wc] + wj * lo
                            accs[2 * wc + 1] = accs[2 * wc + 1] + wj * hi
                    for wc in range(SC_DIM_BLOCK // lanes):
                        out_v[pl.ds(db * SC_DIM_BLOCK + wc * lanes, lanes)] = accs[2 * wc]
                        out_v[pl.ds(ROW_WORDS_FLAT + db * SC_DIM_BLOCK + wc * lanes, lanes)] = accs[2 * wc + 1]
                    return carry
                lax.fori_loop(0, SC_HALF // SC_ROW_GROUP, group_body, 0)

        def token(i, carry):
            t = base + i
            pltpu.sync_copy(ids_hbm.at[tok_off + t, pl.ds(0, SC_HALF)], idx_a)
            pltpu.sync_copy(ids_hbm.at[tok_off + t, pl.ds(SC_HALF, SC_HALF)], idx_b)
            copy_a = pltpu.async_copy(tbl_hbm.at[idx_a], rows_a, sem_a)
            copy_b = pltpu.async_copy(tbl_hbm.at[idx_b], rows_b, sem_b)
            pltpu.sync_copy(w_hbm.at[t], w_v)
            for q in range(D_MODEL // lanes):
                out_v[pl.ds(q * lanes, lanes)] = jnp.zeros((lanes,), f32)
            copy_a.wait()
            accumulate(rows_a, 0)
            copy_b.wait()
            accumulate(rows_b, SC_HALF)
            pltpu.sync_copy(out_v, out_hbm.at[t])
            return carry

        lax.fori_loop(0, tpw, token, 0)

    return run(ids, w, tbl_words)


def _final_kernel(x1_ref, p_ref, g_ref, y_ref):
    x = x1_ref[...] + p_ref[...]
    ms = jnp.mean(x * x, axis=-1, keepdims=True)
    y_ref[...] = x * lax.rsqrt(ms + EPS) * g_ref[...]


def _final(x1, peer_out, final_g, tok_off):
    n = peer_out.shape[0]
    t = T_PROJ
    b0 = tok_off // t
    row = pl.BlockSpec((t, D_MODEL), lambda i: (i, 0))
    return pl.pallas_call(
        _final_kernel,
        grid=(n // t,),
        in_specs=[pl.BlockSpec((t, D_MODEL), lambda i: (i + b0, 0)), row, _const_spec((1, D_MODEL))],
        out_specs=row,
        out_shape=jax.ShapeDtypeStruct((n, D_MODEL), f32),
        compiler_params=_cparams(("parallel",)),
        name="final_norm",
    )(x1, peer_out, final_g)


def _rotary_tables(seq):
    half = ROT_DIM // 2
    inv = ROPE_THETA ** (-jnp.arange(half, dtype=f32) * 2.0 / ROT_DIM)
    ang = jnp.arange(seq, dtype=jnp.int32).astype(f32)[:, None] * inv[None, :]
    cos = jnp.cos(ang)
    sin = jnp.sin(ang)
    pad = HEAD_DIM - ROT_DIM
    one = jnp.ones((seq, pad), f32)
    zero = jnp.zeros((seq, pad), f32)
    zh = jnp.zeros((seq, half), f32)
    cos_h = jnp.concatenate([cos, cos, one], axis=1)
    sa_h = jnp.concatenate([-sin, zh, zero], axis=1)
    sb_h = jnp.concatenate([zh, sin, zero], axis=1)
    rep = LANES // HEAD_DIM
    return tuple(jnp.tile(t, (1, rep)) for t in (cos_h, sa_h, sb_h))


def _front_ops(x, params, tables, st):
    (norm1_g, w_in_bf, b_gate, merge_w, keys_bf, _, _, _) = params
    batch, seq, _ = x.shape
    n = batch * seq
    x2 = x.reshape(n, D_MODEL)

    def inproj():
        st["q"], st["k"], st["v"], st["u"], st["gates"] = _inproj(x2, seq, norm1_g, w_in_bf, b_gate, *tables)
        st["o"], st["lse"] = [], []

    def attention(g):
        o, lse = _attention_group(st["q"][g], st["k"][g], st["v"][g], batch, seq, g)
        st["o"].append(o)
        st["lse"].append(lse)

    def merge():
        st["x1"], st["h2r"], st["qp"] = _merge(x2, seq, st["u"], st["gates"], st["o"], st["lse"], merge_w)

    def route():
        ids, gate = _route(st["qp"], keys_bf)
        st["ids"], st["gate3"], st["n"] = ids, gate.reshape(n, 1, PEER_SLOTS), n

    return [inproj] + [functools.partial(attention, g) for g in range(N_GROUPS)] + [merge, route]


def _peer_ops(params, st, pending):
    u_tbl, v_tbl = params[5], params[6]

    def chunk(off, n_chunk):
        ids = st["ids"]
        w3 = _peer_u(ids.reshape(st["n"] * PEER_SLOTS), st["h2r"], st["gate3"], u_tbl, off, n_chunk)
        pv = _peer_v_sc(ids, w3.reshape(n_chunk, PEER_SLOTS), v_tbl, off)
        pending.append((st["x1"], pv, off))

    ops, off = [], 0
    for share in PEER_CHUNK_SHARES:
        n_chunk = st["tokens"] * share // sum(PEER_CHUNK_SHARES)
        ops.append(functools.partial(chunk, off, n_chunk))
        off += n_chunk
    return ops


def kernel(x_prompt, x_sample, norm1_g, w_in, b_gate, w_attn_up, conv_dw_w, conv_dw_b, conv_ln_g, conv_ln_b,
           conv_pw_w, conv_pw_b, w_out, norm2_g, peer_wq, peer_keys, peer_u, peer_v, final_g):
    assert w_in.shape[0] == 1, "one encoder layer followed by the final norm"
    row = lambda a: a.reshape(1, -1)
    tables = _rotary_tables(max(x_prompt.shape[1], x_sample.shape[1]))
    l = 0
    merge_w = (conv_dw_w[l], row(conv_dw_b[l]), row(conv_ln_g[l]), row(conv_ln_b[l]),
               conv_pw_w[l].astype(bf16), row(conv_pw_b[l]), w_attn_up[l].astype(bf16),
               w_out[l].astype(bf16), row(norm2_g[l]), peer_wq[l].astype(bf16))
    keys_bf = peer_keys[l].astype(bf16).reshape(2 * PEER_HEADS, N_SUBKEYS, -1)
    params = (row(norm1_g[l]), w_in[l].astype(bf16), row(b_gate[l]), merge_w, keys_bf,
              _pack_table(peer_u[l]), _pack_table(peer_v[l]).reshape(N_EXPERTS, ROW_WORDS_FLAT), row(final_g))

    subs = []
    for x in (x_prompt, x_sample):
        step = x.shape[0] // BATCH_SUBGROUPS
        subs += [x[b:b + step] for b in range(0, x.shape[0], step)]
    states = [dict(tokens=sub.shape[0] * sub.shape[1]) for sub in subs]
    pendings = [[] for _ in subs]
    for op in _front_ops(subs[0], params, tables, states[0]):
        op()
    for i in range(len(subs)):
        peer = _peer_ops(params, states[i], pendings[i])
        front = _front_ops(subs[i + 1], params, tables, states[i + 1]) if i + 1 < len(subs) else []
        for j in range(max(len(peer), len(front))):
            if j < len(peer):
                peer[j]()
            if j < len(front):
                front[j]()

    def finish(x, pending):
        ys = [_final(x1, pv, params[-1], off) for x1, pv, off in pending]
        return jnp.concatenate(ys, axis=0).reshape(x.shape)

    half = BATCH_SUBGROUPS
    return (finish(x_prompt, sum(pendings[:half], [])), finish(x_sample, sum(pendings[half:], [])))
```

```python
import dataclasses
import functools
import math

import numpy as np
import jax
import jax.numpy as jnp
from jax import lax
from jax.experimental import pallas as pl
from jax.experimental.pallas import tpu as pltpu
from jax.experimental.pallas import tpu_sc as plsc

f32 = jnp.float32
bf16 = jnp.bfloat16
i32 = jnp.int32

D_MODEL = 1024
HEAD_DIM = 64
N_GROUPS = 3
HEADS_PER_GROUP = 4
GROUP_W = HEADS_PER_GROUP * HEAD_DIM
ATTN_W = N_GROUPS * GROUP_W
WINDOWS = (128, 512, 2048)
DILATIONS = (1, 4, 16)
HALF_WIN = 64
ROT_DIM = HEAD_DIM // 4
ROPE_THETA = 500000.0
NEG_INF = -1e30
CONV_W = 512
CONV_K = 31
CONV_HALO = 16
PEER_HEADS = 8
N_SUBKEYS = 128
N_EXPERTS = N_SUBKEYS * N_SUBKEYS
PEER_TOPK = 16
PEER_SLOTS = PEER_HEADS * PEER_TOPK
EPS = 1e-6

LANES = 128
SUBLANES = 8
ROW_WORDS = D_MODEL // 2 // LANES
VMEM_LIMIT = 56 * 1024 * 1024

T_PROJ = 256
T_PEER = 256
PAIRS_PER_STEP = 8
GATHER_CHUNK = 32
N_CHUNKS = (8 * 16) // GATHER_CHUNK
BATCH_SHARES = ((2, 2, 2, 2), (2, 2, 2, 1, 1))
PEER_CHUNK_SHARES = (4, 4, 4, 2, 1, 1)

SC_CORES = 2
SC_SUBCORES = 16
SC_LANES = 16
SC_WORKERS = SC_CORES * SC_SUBCORES
ROW_WORDS_FLAT = D_MODEL // 2
SC_HALF = PEER_SLOTS // 2
SC_DIM_BLOCK = 128
SC_ROW_GROUP = 8
Q_SUB = 128


def _cparams(sem):
    return pltpu.CompilerParams(dimension_semantics=sem, vmem_limit_bytes=VMEM_LIMIT)


def _const_spec(shape):
    nd = len(shape)
    return pl.BlockSpec(shape, lambda *_: (0,) * nd, pipeline_mode=pl.Buffered(1))


def _inproj_kernel(x_ref, g_ref, w_ref, bg_ref, cos_ref, sa_ref, sb_ref, *refs):
    qkv_refs = (refs[0:N_GROUPS], refs[N_GROUPS:2 * N_GROUPS], refs[2 * N_GROUPS:3 * N_GROUPS])
    u_ref, gate_ref, stage_ref = refs[3 * N_GROUPS:]
    x = x_ref[...]
    ms = jnp.mean(x * x, axis=-1, keepdims=True)
    h = (x * lax.rsqrt(ms + EPS) * g_ref[...]).astype(bf16)

    def proj(lo, hi):
        return jnp.dot(h, w_ref[:, lo:hi], preferred_element_type=f32)

    cos = cos_ref[...]
    sa = sa_ref[...]
    sb = sb_ref[...]

    def rotary(tc, scale):
        r = tc * cos + pltpu.roll(tc, LANES - ROT_DIM // 2, 1) * sa + pltpu.roll(tc, ROT_DIM // 2, 1) * sb
        return r * scale

    def emit(t, out_refs, fn):
        for c in range(ATTN_W // LANES):
            g, half = divmod(c, GROUP_W // LANES)
            d = DILATIONS[g]
            chunk = fn(t[:, c * LANES:(c + 1) * LANES])
            if d == 1:
                out_refs[g][:, half * LANES:(half + 1) * LANES] = chunk.astype(bf16)
                continue
            stage_ref[...] = chunk
            for r in range(d):
                col = r * GROUP_W + half * LANES
                out_refs[g][:, col:col + LANES] = stage_ref[pl.ds(r, T_PROJ // d, stride=d), :].astype(bf16)

    emit(proj(0, ATTN_W), qkv_refs[0], lambda tc: rotary(tc, HEAD_DIM ** -0.5))
    emit(proj(ATTN_W, 2 * ATTN_W), qkv_refs[1], lambda tc: rotary(tc, 1.0))
    emit(proj(2 * ATTN_W, 3 * ATTN_W), qkv_refs[2], lambda tc: tc)
    c0 = 3 * ATTN_W
    a = proj(c0, c0 + CONV_W)
    b = proj(c0 + CONV_W, c0 + 2 * CONV_W)
    u_ref[...] = a * jax.nn.sigmoid(b)
    gates = proj(c0 + 2 * CONV_W, c0 + 2 * CONV_W + 2 * D_MODEL) + bg_ref[...]
    gate_ref[...] = jax.nn.sigmoid(gates).astype(bf16)


def _inproj(x2, seq, norm1_g, w_in_bf, b_gate, cos_t, sa_t, sb_t):
    n = x2.shape[0]
    nsb = seq // T_PROJ
    in_cols = w_in_bf.shape[1]
    row = lambda w: pl.BlockSpec((T_PROJ, w), lambda i: (i, 0))
    pos = pl.BlockSpec((T_PROJ, LANES), lambda i: (i % nsb, 0))
    grp_specs = [pl.BlockSpec((T_PROJ // d, d * GROUP_W), lambda i: (i, 0)) for d in DILATIONS]
    grp_shapes = [jax.ShapeDtypeStruct((n // d, d * GROUP_W), bf16) for d in DILATIONS]
    outs = pl.pallas_call(
        _inproj_kernel,
        grid=(n // T_PROJ,),
        in_specs=[row(D_MODEL), _const_spec((1, D_MODEL)), _const_spec((D_MODEL, in_cols)),
                  _const_spec((1, 2 * D_MODEL)), pos, pos, pos],
        out_specs=grp_specs * 3 + [row(CONV_W), row(2 * D_MODEL)],
        out_shape=grp_shapes * 3
        + [jax.ShapeDtypeStruct((n, CONV_W), f32), jax.ShapeDtypeStruct((n, 2 * D_MODEL), bf16)],
        scratch_shapes=[pltpu.VMEM((T_PROJ, LANES), f32)],
        compiler_params=_cparams(("parallel",)),
        name="inproj",
    )(x2, norm1_g, w_in_bf, b_gate, cos_t, sa_t, sb_t)
    q, k, v = outs[0:N_GROUPS], outs[N_GROUPS:2 * N_GROUPS], outs[2 * N_GROUPS:3 * N_GROUPS]
    return q, k, v, outs[3 * N_GROUPS], outs[3 * N_GROUPS + 1]


def _attn_kernel(q_ref, kp_ref, kc_ref, kn_ref, vp_ref, vc_ref, vn_ref, o_ref, lse_ref,
                 kw_ref, vw_ref, *, tq, n_rows):
    i = pl.program_id(2)
    kw_ref[0:HALF_WIN] = kp_ref[0]
    kw_ref[HALF_WIN:HALF_WIN + tq] = kc_ref[0]
    kw_ref[HALF_WIN + tq:2 * HALF_WIN + tq] = kn_ref[0]
    vw_ref[0:HALF_WIN] = vp_ref[0]
    vw_ref[HALF_WIN:HALF_WIN + tq] = vc_ref[0]
    vw_ref[HALF_WIN + tq:2 * HALF_WIN + tq] = vn_ref[0]

    win = Q_SUB + 2 * HALF_WIN
    qi = lax.broadcasted_iota(i32, (Q_SUB, win), 0)
    kj = lax.broadcasted_iota(i32, (Q_SUB, win), 1)
    band = (kj - qi >= 0) & (kj - qi <= 2 * HALF_WIN)
    head_of_lane = lax.broadcasted_iota(i32, (1, GROUP_W), 1) // HEAD_DIM

    for s in range(tq // Q_SUB):
        qs = q_ref[0, s * Q_SUB:(s + 1) * Q_SUB, :]
        kwin = kw_ref[s * Q_SUB:s * Q_SUB + win, :]
        vwin = vw_ref[s * Q_SUB:s * Q_SUB + win, :]
        key_row = i * tq + (s * Q_SUB - HALF_WIN) + kj
        ok = band & (key_row >= 0) & (key_row < n_rows)
        o_acc = jnp.zeros((Q_SUB, GROUP_W), f32)
        l_acc = jnp.zeros((Q_SUB, GROUP_W), f32)
        for h in range(HEADS_PER_GROUP):
            hm = head_of_lane == h
            qh = jnp.where(hm, qs, jnp.zeros_like(qs))
            sc = lax.dot_general(qh, kwin, (((1,), (1,)), ((), ())), preferred_element_type=f32)
            sc = jnp.where(ok, sc, NEG_INF)
            m = jnp.max(sc, axis=-1, keepdims=True)
            p = jnp.exp(sc - m)
            den = jnp.sum(p, axis=-1, keepdims=True)
            pv = jnp.dot(p.astype(bf16), vwin, preferred_element_type=f32)
            o_acc = jnp.where(hm, pv / den, o_acc)
            l_acc = jnp.where(hm, m + jnp.log(den), l_acc)
        o_ref[0, s * Q_SUB:(s + 1) * Q_SUB, :] = o_acc
        lse_ref[0, s * Q_SUB:(s + 1) * Q_SUB, :] = l_acc


def _attention_group(q, k, v, batch, seq, g):
    d = DILATIONS[g]
    n_rows = seq // d
    tq = min(512, n_rows)
    nblk = n_rows // tq
    hb = tq // HALF_WIN
    n_halo_blocks = n_rows // HALF_WIN
    view = lambda t: t.reshape(batch, n_rows, d * GROUP_W)
    cur = pl.BlockSpec((1, tq, GROUP_W), lambda b, r, i: (b, i, r))
    prev = pl.BlockSpec((1, HALF_WIN, GROUP_W), lambda b, r, i: (b, jnp.maximum(i * hb - 1, 0), r))
    nxt = pl.BlockSpec((1, HALF_WIN, GROUP_W),
                       lambda b, r, i: (b, jnp.minimum((i + 1) * hb, n_halo_blocks - 1), r))
    o, lse = pl.pallas_call(
        functools.partial(_attn_kernel, tq=tq, n_rows=n_rows),
        grid=(batch, d, nblk),
        in_specs=[cur, prev, cur, nxt, prev, cur, nxt],
        out_specs=[cur, cur],
        out_shape=[jax.ShapeDtypeStruct((batch, n_rows, d * GROUP_W), f32)] * 2,
        scratch_shapes=[pltpu.VMEM((tq + 2 * HALF_WIN, GROUP_W), bf16)] * 2,
        compiler_params=_cparams(("parallel", "parallel", "parallel")),
        name=f"attn_g{g}",
    )(view(q), view(k), view(k), view(k), view(v), view(v), view(v))
    return o.reshape(batch * n_rows, d * GROUP_W), lse.reshape(batch * n_rows, d * GROUP_W)


def _merge_kernel(x_ref, up_ref, uc_ref, un_ref, gate_ref,
                  o0_ref, o1_ref, o2_ref, l0_ref, l1_ref, l2_ref,
                  dww_ref, dwb_ref, lng_ref, lnb_ref, pww_ref, pwb_ref, wup_ref, wout_ref, n2g_ref, wq_ref,
                  x1_ref, h2_ref, qp_ref, ue_ref, cv_ref, lo_ref, hi_ref, *, nsb):
    i = pl.program_id(0)
    t = T_PROJ

    def token_major(ref, g):
        d = DILATIONS[g]
        if d == 1:
            return ref[...]
        for r in range(d):
            lo_ref[pl.ds(r, t // d, stride=d), :] = ref[:, r * GROUP_W:r * GROUP_W + LANES]
            hi_ref[pl.ds(r, t // d, stride=d), :] = ref[:, r * GROUP_W + LANES:(r + 1) * GROUP_W]
        return jnp.concatenate([lo_ref[...], hi_ref[...]], axis=1)

    first = (i % nsb) == 0
    last = (i % nsb) == nsb - 1
    ue_ref[0:CONV_HALO] = jnp.where(first, 0.0, up_ref[...])
    ue_ref[CONV_HALO:CONV_HALO + t] = uc_ref[...]
    ue_ref[CONV_HALO + t:2 * CONV_HALO + t] = jnp.where(last, 0.0, un_ref[...])

    rc = 32
    off = CONV_HALO - CONV_K // 2
    for r0 in range(0, t, rc):
        acc = jnp.zeros((rc, CONV_W), f32)
        for j in range(CONV_K):
            acc = acc + ue_ref[r0 + off + j:r0 + off + j + rc, :] * dww_ref[j:j + 1, :]
        cv_ref[r0:r0 + rc, :] = acc
    c = cv_ref[...] + dwb_ref[...]
    mu = jnp.mean(c, axis=-1, keepdims=True)
    cc = c - mu
    var = jnp.mean(cc * cc, axis=-1, keepdims=True)
    un = cc * lax.rsqrt(var + EPS) * lng_ref[...] + lnb_ref[...]
    sw = un * jax.nn.sigmoid(un)
    conv = jnp.dot(sw.astype(bf16), pww_ref[...], preferred_element_type=f32) + pwb_ref[...]

    l0 = token_major(l0_ref, 0)
    l1 = token_major(l1_ref, 1)
    l2 = token_major(l2_ref, 2)
    lm = jnp.maximum(jnp.maximum(l0, l1), l2)
    e0 = jnp.exp(l0 - lm)
    e1 = jnp.exp(l1 - lm)
    e2 = jnp.exp(l2 - lm)
    comb = (e0 * token_major(o0_ref, 0) + e1 * token_major(o1_ref, 1) + e2 * token_major(o2_ref, 2)) / (e0 + e1 + e2)
    attn = jnp.dot(comb.astype(bf16), wup_ref[...], preferred_element_type=f32)

    g_attn = gate_ref[:, 0:D_MODEL].astype(f32)
    g_conv = gate_ref[:, D_MODEL:2 * D_MODEL].astype(f32)
    mixed = g_attn * attn + g_conv * conv
    x1 = x_ref[...] + jnp.dot(mixed.astype(bf16), wout_ref[...], preferred_element_type=f32)
    x1_ref[...] = x1
    ms = jnp.mean(x1 * x1, axis=-1, keepdims=True)
    h2 = x1 * lax.rsqrt(ms + EPS) * n2g_ref[...]
    for c in range(D_MODEL // LANES):
        h2_ref[pl.ds(c, t, stride=SUBLANES), :] = h2[:, c * LANES:(c + 1) * LANES]
    qp_ref[...] = jnp.dot(h2.astype(bf16), wq_ref[...], preferred_element_type=f32).astype(bf16)


def _merge(x2, seq, u, gates, os_, ls_, weights):
    n = x2.shape[0]
    t = T_PROJ
    nsb = seq // t
    hb = t // CONV_HALO
    nhalo = n // CONV_HALO
    row = lambda w: pl.BlockSpec((t, w), lambda i: (i, 0))
    prev = pl.BlockSpec((CONV_HALO, CONV_W), lambda i: (jnp.maximum(i * hb - 1, 0), 0))
    nxt = pl.BlockSpec((CONV_HALO, CONV_W), lambda i: (jnp.minimum((i + 1) * hb, nhalo - 1), 0))
    wspecs = [_const_spec(w.shape) for w in weights]
    grp = [pl.BlockSpec((t // d, d * GROUP_W), lambda i: (i, 0)) for d in DILATIONS]
    qw = weights[-1].shape[1]
    return pl.pallas_call(
        functools.partial(_merge_kernel, nsb=nsb),
        grid=(n // t,),
        in_specs=[row(D_MODEL), prev, row(CONV_W), nxt, row(2 * D_MODEL)] + grp * 2 + wspecs,
        out_specs=[row(D_MODEL), pl.BlockSpec((t * SUBLANES, LANES), lambda i: (i, 0)), row(qw)],
        out_shape=[jax.ShapeDtypeStruct((n, D_MODEL), f32), jax.ShapeDtypeStruct((n * SUBLANES, LANES), f32),
                   jax.ShapeDtypeStruct((n, qw), bf16)],
        scratch_shapes=[pltpu.VMEM((t + 2 * CONV_HALO, CONV_W), f32), pltpu.VMEM((t, CONV_W), f32),
                        pltpu.VMEM((t, LANES), f32), pltpu.VMEM((t, LANES), f32)],
        compiler_params=_cparams(("parallel",)),
        name="merge",
    )(x2, u, u, u, gates, *os_, *ls_, *weights)


def _candidate_slabs():
    slabs = [("row", 0, 0, 16), ("row", 1, 0, 8)]
    for j in range(PEER_TOPK):
        hi = PEER_TOPK // (j + 1)
        if hi > 2:
            slabs.append(("col", j, 2, hi))
    return slabs


def _route_kernel(qp_ref, keys_ref, ids_ref, gate_ref, val_ref, idx_ref, best_ref, idt_ref, gt_ref):
    t = T_PROJ
    k_iota = lax.broadcasted_iota(i32, (N_SUBKEYS, t), 0)
    for hc in range(2 * PEER_HEADS):
        q = qp_ref[:, hc * N_SUBKEYS:(hc + 1) * N_SUBKEYS]
        s = lax.dot_general(keys_ref[hc], q, (((1,), (1,)), ((), ())), preferred_element_type=f32)
        for r in range(PEER_TOPK):
            m = jnp.max(s, axis=0, keepdims=True)
            am = jnp.min(jnp.where(s == m, k_iota, N_SUBKEYS), axis=0, keepdims=True)
            s = jnp.where(k_iota == am, -jnp.inf, s)
            val_ref[hc, pl.ds(r, 1), :] = m
            idx_ref[hc, pl.ds(r, 1), :] = am

    r_iota = lax.broadcasted_iota(i32, (PEER_TOPK, t), 0)
    slabs = _candidate_slabs()
    for h in range(PEER_HEADS):
        v0 = val_ref[2 * h]
        v1 = val_ref[2 * h + 1]
        i0 = idx_ref[2 * h]
        i1 = idx_ref[2 * h + 1]
        cands, flats, eids = [], [], []
        for kind, fixed, lo, hi in slabs:
            if kind == "row":
                c = v0[fixed:fixed + 1, :] + v1
                fl = fixed * PEER_TOPK + r_iota
                ei = i0[fixed:fixed + 1, :] * N_SUBKEYS + i1
            else:
                c = v0 + v1[fixed:fixed + 1, :]
                fl = r_iota * PEER_TOPK + fixed
                ei = i0 * N_SUBKEYS + i1[fixed:fixed + 1, :]
            valid = (r_iota >= lo) & (r_iota < hi)
            cands.append(jnp.where(valid, c, -jnp.inf))
            flats.append(fl)
            eids.append(ei)
        big = PEER_TOPK * PEER_TOPK
        for r in range(PEER_TOPK):
            m = functools.reduce(jnp.maximum, cands)
            m = jnp.max(m, axis=0, keepdims=True)
            fsel = functools.reduce(jnp.minimum, [jnp.where(c == m, fl, big) for c, fl in zip(cands, flats)])
            fsel = jnp.min(fsel, axis=0, keepdims=True)
            hit = [fl == fsel for fl in flats]
            eid = functools.reduce(jnp.maximum, [jnp.where(hh, ei, -1) for hh, ei in zip(hit, eids)])
            eid = jnp.max(eid, axis=0, keepdims=True)
            cands = [jnp.where(hh, -jnp.inf, c) for hh, c in zip(hit, cands)]
            best_ref[pl.ds(r, 1), :] = m
            idt_ref[pl.ds(h * PEER_TOPK + r, 1), :] = eid
        b = best_ref[...]
        e = jnp.exp(b - jnp.max(b, axis=0, keepdims=True))
        gt_ref[h * PEER_TOPK:(h + 1) * PEER_TOPK, :] = e / jnp.sum(e, axis=0, keepdims=True)
    ids_ref[...] = idt_ref[...].T
    gate_ref[...] = gt_ref[...].T


def _route(qp, keys_bf):
    n, qw = qp.shape
    t = T_PROJ
    return pl.pallas_call(
        _route_kernel,
        grid=(n // t,),
        in_specs=[pl.BlockSpec((t, qw), lambda i: (i, 0)), _const_spec(keys_bf.shape)],
        out_specs=[pl.BlockSpec((t, PEER_SLOTS), lambda i: (i, 0))] * 2,
        out_shape=[jax.ShapeDtypeStruct((n, PEER_SLOTS), i32), jax.ShapeDtypeStruct((n, PEER_SLOTS), f32)],
        scratch_shapes=[pltpu.VMEM((2 * PEER_HEADS, PEER_TOPK, t), f32), pltpu.VMEM((2 * PEER_HEADS, PEER_TOPK, t), i32),
                        pltpu.VMEM((PEER_TOPK, t), f32), pltpu.VMEM((PEER_SLOTS, t), i32),
                        pltpu.VMEM((PEER_SLOTS, t), f32)],
        compiler_params=_cparams(("parallel",)),
        name="route",
    )(qp, keys_bf)


def _pack_table(tbl):
    bits = lax.bitcast_convert_type(tbl.astype(bf16), jnp.uint16).astype(jnp.uint32)
    half = D_MODEL // 2
    words = bits[:, :half] | (bits[:, half:] << 16)
    return lax.bitcast_convert_type(words, i32).reshape(tbl.shape[0], ROW_WORDS, LANES)


def _unpack(words):
    lo = pltpu.bitcast(words << 16, f32)
    hi = pltpu.bitcast(words & jnp.int32(-65536), f32)
    return lo, hi


def _split2(x):
    hi = x.astype(bf16)
    lo = (x - hi.astype(f32)).astype(bf16)
    return hi, lo


def _gather_chunk(ids_ref, tbl_ref, tile_ref, c, base):
    e = None
    for k in range(GATHER_CHUNK):
        e = ids_ref[base + k]
        tile_ref[c, k * ROW_WORDS:(k + 1) * ROW_WORDS, :] = tbl_ref[e]
    return e


def _two_token_pipeline(ids_ref, tbl_ref, tile_a, tile_b, consume, init):
    tile_b[...] = jnp.zeros(tile_b.shape, i32)
    last = T_PEER - 1

    def half_step(t_gather, gather_tile, t_consume, consume_tile, dep):
        base = jnp.minimum(t_gather, last) * PEER_SLOTS
        tc = jnp.clip(t_consume, 0, last)
        acc = init
        for c in range(N_CHUNKS):
            dep = _gather_chunk(ids_ref, tbl_ref, gather_tile, c, base + c * GATHER_CHUNK + (dep >> 31))
            acc = consume(consume_tile, c, tc, acc)
        return dep

    def pairs_step(i, dep):
        for p in range(PAIRS_PER_STEP):
            even = 2 * (PAIRS_PER_STEP * i + p)
            dep = half_step(even, tile_a, even - 1, tile_b, dep)
            dep = half_step(even + 1, tile_b, even, tile_a, dep)
        return dep

    lax.fori_loop(0, T_PEER // (2 * PAIRS_PER_STEP) + 1, pairs_step, jnp.int32(0))


def _tile_scratch():
    return pltpu.VMEM((N_CHUNKS, GATHER_CHUNK * ROW_WORDS, LANES), i32)


def _peer_u_kernel(ids_ref, xr_ref, g_ref, tbl_ref, w_ref, tile_a, tile_b):
    crow = GATHER_CHUNK * ROW_WORDS
    half = crow // 2
    kk = lax.broadcasted_iota(i32, (2 * LANES, LANES), 0)
    nn = lax.broadcasted_iota(i32, (2 * LANES, LANES), 1)
    summer = ((kk < LANES) == (nn % GATHER_CHUNK < GATHER_CHUNK // 2)).astype(bf16)
    rr = lax.broadcasted_iota(i32, (half, LANES), 0)
    ll = lax.broadcasted_iota(i32, (half, LANES), 1)
    own_row = ll % (GATHER_CHUNK // 2) == rr // ROW_WORDS
    lane_chunk = ll // GATHER_CHUNK

    def consume(tile, c, t, acc):
        xt = xr_ref[pl.ds(pl.multiple_of(t * SUBLANES, SUBLANES), SUBLANES), :]
        xlo = xt[0:ROW_WORDS]
        xhi = xt[ROW_WORDS:2 * ROW_WORDS]
        x2lo = pltpu.repeat(jnp.concatenate([xlo, xlo], axis=0), crow // SUBLANES, axis=0)
        x2hi = pltpu.repeat(jnp.concatenate([xhi, xhi], axis=0), crow // SUBLANES, axis=0)
        lo, hi = _unpack(tile[c])
        p = lo * x2lo + hi * x2hi
        p_hi, p_lo = _split2(jnp.concatenate([p[0:half], p[half:crow]], axis=1))
        rs = jnp.dot(p_hi, summer, preferred_element_type=f32) + jnp.dot(p_lo, summer, preferred_element_type=f32)
        acc = acc + jnp.sum(jnp.where(own_row & (lane_chunk == c), rs, 0.0), axis=0, keepdims=True)
        if c == N_CHUNKS - 1:
            gelu = 0.5 * acc * (1.0 + lax.erf(acc * (2.0 ** -0.5)))
            w_ref[t] = g_ref[t] * gelu
        return acc

    _two_token_pipeline(ids_ref, tbl_ref, tile_a, tile_b, consume, jnp.zeros((1, PEER_SLOTS), f32))


def _peer_u(ids_flat, h2r, gate3, tbl, tok_off, n_tok):
    t = T_PEER
    b0 = tok_off // t
    return pl.pallas_call(
        _peer_u_kernel,
        grid=(n_tok // t,),
        in_specs=[pl.BlockSpec((t * PEER_SLOTS,), lambda i: (i + b0,), memory_space=pltpu.SMEM),
                  pl.BlockSpec((t * SUBLANES, LANES), lambda i: (i + b0, 0)),
                  pl.BlockSpec((t, 1, PEER_SLOTS), lambda i: (i + b0, 0, 0)),
                  _const_spec(tbl.shape)],
        out_specs=pl.BlockSpec((t, 1, PEER_SLOTS), lambda i: (i, 0, 0)),
        out_shape=jax.ShapeDtypeStruct((n_tok, 1, PEER_SLOTS), f32),
        scratch_shapes=[_tile_scratch(), _tile_scratch()],
        compiler_params=_cparams(("parallel",)),
        name="peer_u",
    )(ids_flat, h2r, gate3, tbl)


def _peer_v_sc(ids, w, tbl_words, tok_off):
    n_tok = w.shape[0]
    tpw = n_tok // SC_WORKERS
    lanes = SC_LANES
    mesh = plsc.VectorSubcoreMesh(core_axis_name="c", subcore_axis_name="s")

    @functools.partial(
        pl.kernel, mesh=mesh,
        out_type=jax.ShapeDtypeStruct((n_tok, D_MODEL), f32),
        scratch_types=[
            pltpu.VMEM((SC_HALF,), i32), pltpu.VMEM((SC_HALF,), i32),
            pltpu.VMEM((SC_HALF, ROW_WORDS_FLAT), i32), pltpu.VMEM((SC_HALF, ROW_WORDS_FLAT), i32),
            pltpu.VMEM((PEER_SLOTS,), f32),
            pltpu.VMEM((D_MODEL,), f32),
            pltpu.SemaphoreType.DMA, pltpu.SemaphoreType.DMA,
        ],
        compiler_params=dataclasses.replace(pltpu.CompilerParams(), needs_layout_passes=False),
        name="peer_v_sc",
    )
    def run(ids_hbm, w_hbm, tbl_hbm, out_hbm, idx_a, idx_b, rows_a, rows_b, w_v, out_v, sem_a, sem_b):
        wid = lax.axis_index("s") * SC_CORES + lax.axis_index("c")
        base = wid * tpw

        def accumulate(rows, w_off):
            for db in range(ROW_WORDS_FLAT // SC_DIM_BLOCK):
                def group_body(g, carry):
                    accs = []
                    for wc in range(SC_DIM_BLOCK // lanes):
                        accs.append(out_v[pl.ds(db * SC_DIM_BLOCK + wc * lanes, lanes)])
                        accs.append(out_v[pl.ds(ROW_WORDS_FLAT + db * SC_DIM_BLOCK + wc * lanes, lanes)])
                    j0 = g * SC_ROW_GROUP
                    wchunk = w_v[pl.ds(pl.multiple_of((w_off + j0) // lanes * lanes, lanes), lanes)]
                    sub = (w_off + j0) % lanes
                    for r in range(SC_ROW_GROUP):
                        wj = jnp.take(wchunk, jnp.full((lanes,), sub + r, i32))
                        for wc in range(SC_DIM_BLOCK // lanes):
                            word = rows[j0 + r, pl.ds(db * SC_DIM_BLOCK + wc * lanes, lanes)]
                            lo = lax.bitcast_convert_type(word << 16, f32)
                            hi = lax.bitcast_convert_type(word & jnp.int32(-65536), f32)
                            accs[2 * wc] = accs[2 * wc] + wj * lo
                            accs[2 * wc + 1] = accs[2 * wc + 1] + wj * hi
                    for wc in range(SC_DIM_BLOCK // lanes):
                        out_v[pl.ds(db * SC_DIM_BLOCK + wc * lanes, lanes)] = accs[2 * wc]
                        out_v[pl.ds(ROW_WORDS_FLAT + db * SC_DIM_BLOCK + wc * lanes, lanes)] = accs[2 * wc + 1]
                    return carry
                lax.fori_loop(0, SC_HALF // SC_ROW_GROUP, group_body, 0)

        def token(i, carry):
            t = base + i
            pltpu.sync_copy(ids_hbm.at[tok_off + t, pl.ds(0, SC_HALF)], idx_a)
            pltpu.sync_copy(ids_hbm.at[tok_off + t, pl.ds(SC_HALF, SC_HALF)], idx_b)
            copy_a = pltpu.async_copy(tbl_hbm.at[idx_a], rows_a, sem_a)
            copy_b = pltpu.async_copy(tbl_hbm.at[idx_b], rows_b, sem_b)
            pltpu.sync_copy(w_hbm.at[t], w_v)
            for q in range(D_MODEL // lanes):
                out_v[pl.ds(q * lanes, lanes)] = jnp.zeros((lanes,), f32)
            copy_a.wait()
            accumulate(rows_a, 0)
            copy_b.wait()
            accumulate(rows_b, SC_HALF)
            pltpu.sync_copy(out_v, out_hbm.at[t])
            return carry

        lax.fori_loop(0, tpw, token, 0)

    return run(ids, w, tbl_words)


def _final_kernel(x1_ref, p_ref, g_ref, y_ref):
    x = x1_ref[...] + p_ref[...]
    ms = jnp.mean(x * x, axis=-1, keepdims=True)
    y_ref[...] = x * lax.rsqrt(ms + EPS) * g_ref[...]


def _final(x1, peer_out, final_g, tok_off):
    n = peer_out.shape[0]
    t = T_PROJ
    b0 = tok_off // t
    row = pl.BlockSpec((t, D_MODEL), lambda i: (i, 0))
    return pl.pallas_call(
        _final_kernel,
        grid=(n // t,),
        in_specs=[pl.BlockSpec((t, D_MODEL), lambda i: (i + b0, 0)), row, _const_spec((1, D_MODEL))],
        out_specs=row,
        out_shape=jax.ShapeDtypeStruct((n, D_MODEL), f32),
        compiler_params=_cparams(("parallel",)),
        name="final_norm",
    )(x1, peer_out, final_g)


def _rotary_tables(seq):
    half = ROT_DIM // 2
    inv = ROPE_THETA ** (-jnp.arange(half, dtype=f32) * 2.0 / ROT_DIM)
    ang = jnp.arange(seq, dtype=jnp.int32).astype(f32)[:, None] * inv[None, :]
    cos = jnp.cos(ang)
    sin = jnp.sin(ang)
    pad = HEAD_DIM - ROT_DIM
    one = jnp.ones((seq, pad), f32)
    zero = jnp.zeros((seq, pad), f32)
    zh = jnp.zeros((seq, half), f32)
    cos_h = jnp.concatenate([cos, cos, one], axis=1)
    sa_h = jnp.concatenate([-sin, zh, zero], axis=1)
    sb_h = jnp.concatenate([zh, sin, zero], axis=1)
    rep = LANES // HEAD_DIM
    return tuple(jnp.tile(t, (1, rep)) for t in (cos_h, sa_h, sb_h))


def _front_ops(x, params, tables, st):
    (norm1_g, w_in_bf, b_gate, merge_w, keys_bf, _, _, _) = params
    batch, seq, _ = x.shape
    n = batch * seq
    x2 = x.reshape(n, D_MODEL)

    def inproj():
        st["q"], st["k"], st["v"], st["u"], st["gates"] = _inproj(x2, seq, norm1_g, w_in_bf, b_gate, *tables)
        st["o"], st["lse"] = [], []

    def attention(g):
        o, lse = _attention_group(st["q"][g], st["k"][g], st["v"][g], batch, seq, g)
        st["o"].append(o)
        st["lse"].append(lse)

    def merge():
        st["x1"], st["h2r"], st["qp"] = _merge(x2, seq, st["u"], st["gates"], st["o"], st["lse"], merge_w)

    def route():
        ids, gate = _route(st["qp"], keys_bf)
        st["ids"], st["gate3"], st["n"] = ids, gate.reshape(n, 1, PEER_SLOTS), n

    return [inproj] + [functools.partial(attention, g) for g in range(N_GROUPS)] + [merge, route]


def _peer_ops(params, st, pending):
    u_tbl, v_tbl = params[5], params[6]

    def chunk(off, n_chunk):
        ids = st["ids"]
        w3 = _peer_u(ids.reshape(st["n"] * PEER_SLOTS), st["h2r"], st["gate3"], u_tbl, off, n_chunk)
        pv = _peer_v_sc(ids, w3.reshape(n_chunk, PEER_SLOTS), v_tbl, off)
        pending.append((st["x1"], pv, off))

    ops, off = [], 0
    for share in PEER_CHUNK_SHARES:
        n_chunk = st["tokens"] * share // sum(PEER_CHUNK_SHARES)
        ops.append(functools.partial(chunk, off, n_chunk))
        off += n_chunk
    return ops


def kernel(x_prompt, x_sample, norm1_g, w_in, b_gate, w_attn_up, conv_dw_w, conv_dw_b, conv_ln_g, conv_ln_b,
           conv_pw_w, conv_pw_b, w_out, norm2_g, peer_wq, peer_keys, peer_u, peer_v, final_g):
    assert w_in.shape[0] == 1, "one encoder layer followed by the final norm"
    row = lambda a: a.reshape(1, -1)
    tables = _rotary_tables(max(x_prompt.shape[1], x_sample.shape[1]))
    l = 0
    merge_w = (conv_dw_w[l], row(conv_dw_b[l]), row(conv_ln_g[l]), row(conv_ln_b[l]),
               conv_pw_w[l].astype(bf16), row(conv_pw_b[l]), w_attn_up[l].astype(bf16),
               w_out[l].astype(bf16), row(norm2_g[l]), peer_wq[l].astype(bf16))
    keys_bf = peer_keys[l].astype(bf16).reshape(2 * PEER_HEADS, N_SUBKEYS, -1)
    params = (row(norm1_g[l]), w_in[l].astype(bf16), row(b_gate[l]), merge_w, keys_bf,
              _pack_table(peer_u[l]), _pack_table(peer_v[l]).reshape(N_EXPERTS, ROW_WORDS_FLAT), row(final_g))

    subs = []
    for x, shares in zip((x_prompt, x_sample), BATCH_SHARES):
        b = 0
        for share in shares:
            step = x.shape[0] * share // sum(shares)
            subs.append(x[b:b + step])
            b += step
    states = [dict(tokens=sub.shape[0] * sub.shape[1]) for sub in subs]
    pendings = [[] for _ in subs]
    for op in _front_ops(subs[0], params, tables, states[0]):
        op()
    for i in range(len(subs)):
        peer = _peer_ops(params, states[i], pendings[i])
        front = _front_ops(subs[i + 1], params, tables, states[i + 1]) if i + 1 < len(subs) else []
        for j in range(max(len(peer), len(front))):
            if j < len(peer):
                peer[j]()
            if j < len(front):
                front[j]()

    def finish(x, pending):
        ys = [_final(x1, pv, params[-1], off) for x1, pv, off in pending]
        return jnp.concatenate(ys, axis=0).reshape(x.shape)

    half = len(BATCH_SHARES[0])
    return (finish(x_prompt, sum(pendings[:half], [])), finish(x_sample, sum(pendings[half:], [])))
```

```python
import dataclasses
import functools
import math

import numpy as np
import jax
import jax.numpy as jnp
from jax import lax
from jax.experimental import pallas as pl
from jax.experimental.pallas import tpu as pltpu
from jax.experimental.pallas import tpu_sc as plsc

f32 = jnp.float32
bf16 = jnp.bfloat16
i32 = jnp.int32

D_MODEL = 1024
HEAD_DIM = 64
N_GROUPS = 3
HEADS_PER_GROUP = 4
GROUP_W = HEADS_PER_GROUP * HEAD_DIM
ATTN_W = N_GROUPS * GROUP_W
WINDOWS = (128, 512, 2048)
DILATIONS = (1, 4, 16)
HALF_WIN = 64
ROT_DIM = HEAD_DIM // 4
ROPE_THETA = 500000.0
NEG_INF = -1e30
CONV_W = 512
CONV_K = 31
CONV_HALO = 16
PEER_HEADS = 8
N_SUBKEYS = 128
N_EXPERTS = N_SUBKEYS * N_SUBKEYS
PEER_TOPK = 16
PEER_SLOTS = PEER_HEADS * PEER_TOPK
EPS = 1e-6

LANES = 128
SUBLANES = 8
ROW_WORDS = D_MODEL // 2 // LANES
VMEM_LIMIT = 56 * 1024 * 1024

T_PROJ = 256
T_PEER = 512
PAIRS_PER_STEP = 16
GATHER_CHUNK = 32
N_CHUNKS = (8 * 16) // GATHER_CHUNK
BATCH_SHARES = ((2, 2, 2, 2), (2, 2, 2, 1, 1))
PEER_CHUNK_SHARES = (4, 4, 4, 2, 1, 1)

SC_CORES = 2
SC_SUBCORES = 16
SC_LANES = 16
SC_WORKERS = SC_CORES * SC_SUBCORES
ROW_WORDS_FLAT = D_MODEL // 2
SC_HALF = PEER_SLOTS // 2
SC_DIM_BLOCK = 128
SC_ROW_GROUP = 8
Q_SUB = 128


def _cparams(sem):
    return pltpu.CompilerParams(dimension_semantics=sem, vmem_limit_bytes=VMEM_LIMIT)


def _const_spec(shape):
    nd = len(shape)
    return pl.BlockSpec(shape, lambda *_: (0,) * nd, pipeline_mode=pl.Buffered(1))


def _inproj_kernel(x_ref, g_ref, w_ref, bg_ref, cos_ref, sa_ref, sb_ref, *refs):
    qkv_refs = (refs[0:N_GROUPS], refs[N_GROUPS:2 * N_GROUPS], refs[2 * N_GROUPS:3 * N_GROUPS])
    u_ref, gate_ref, stage_ref = refs[3 * N_GROUPS:]
    x = x_ref[...]
    ms = jnp.mean(x * x, axis=-1, keepdims=True)
    h = (x * lax.rsqrt(ms + EPS) * g_ref[...]).astype(bf16)

    def proj(lo, hi):
        return jnp.dot(h, w_ref[:, lo:hi], preferred_element_type=f32)

    cos = cos_ref[...]
    sa = sa_ref[...]
    sb = sb_ref[...]

    def rotary(tc, scale):
        r = tc * cos + pltpu.roll(tc, LANES - ROT_DIM // 2, 1) * sa + pltpu.roll(tc, ROT_DIM // 2, 1) * sb
        return r * scale

    def emit(t, out_refs, fn):
        for c in range(ATTN_W // LANES):
            g, half = divmod(c, GROUP_W // LANES)
            d = DILATIONS[g]
            chunk = fn(t[:, c * LANES:(c + 1) * LANES])
            if d == 1:
                out_refs[g][:, half * LANES:(half + 1) * LANES] = chunk.astype(bf16)
                continue
            stage_ref[...] = chunk
            for r in range(d):
                col = r * GROUP_W + half * LANES
                out_refs[g][:, col:col + LANES] = stage_ref[pl.ds(r, T_PROJ // d, stride=d), :].astype(bf16)

    emit(proj(0, ATTN_W), qkv_refs[0], lambda tc: rotary(tc, HEAD_DIM ** -0.5))
    emit(proj(ATTN_W, 2 * ATTN_W), qkv_refs[1], lambda tc: rotary(tc, 1.0))
    emit(proj(2 * ATTN_W, 3 * ATTN_W), qkv_refs[2], lambda tc: tc)
    c0 = 3 * ATTN_W
    a = proj(c0, c0 + CONV_W)
    b = proj(c0 + CONV_W, c0 + 2 * CONV_W)
    u_ref[...] = a * jax.nn.sigmoid(b)
    gates = proj(c0 + 2 * CONV_W, c0 + 2 * CONV_W + 2 * D_MODEL) + bg_ref[...]
    gate_ref[...] = jax.nn.sigmoid(gates).astype(bf16)


def _inproj(x2, seq, norm1_g, w_in_bf, b_gate, cos_t, sa_t, sb_t):
    n = x2.shape[0]
    nsb = seq // T_PROJ
    in_cols = w_in_bf.shape[1]
    row = lambda w: pl.BlockSpec((T_PROJ, w), lambda i: (i, 0))
    pos = pl.BlockSpec((T_PROJ, LANES), lambda i: (i % nsb, 0))
    grp_specs = [pl.BlockSpec((T_PROJ // d, d * GROUP_W), lambda i: (i, 0)) for d in DILATIONS]
    grp_shapes = [jax.ShapeDtypeStruct((n // d, d * GROUP_W), bf16) for d in DILATIONS]
    outs = pl.pallas_call(
        _inproj_kernel,
        grid=(n // T_PROJ,),
        in_specs=[row(D_MODEL), _const_spec((1, D_MODEL)), _const_spec((D_MODEL, in_cols)),
                  _const_spec((1, 2 * D_MODEL)), pos, pos, pos],
        out_specs=grp_specs * 3 + [row(CONV_W), row(2 * D_MODEL)],
        out_shape=grp_shapes * 3
        + [jax.ShapeDtypeStruct((n, CONV_W), f32), jax.ShapeDtypeStruct((n, 2 * D_MODEL), bf16)],
        scratch_shapes=[pltpu.VMEM((T_PROJ, LANES), f32)],
        compiler_params=_cparams(("parallel",)),
        name="inproj",
    )(x2, norm1_g, w_in_bf, b_gate, cos_t, sa_t, sb_t)
    q, k, v = outs[0:N_GROUPS], outs[N_GROUPS:2 * N_GROUPS], outs[2 * N_GROUPS:3 * N_GROUPS]
    return q, k, v, outs[3 * N_GROUPS], outs[3 * N_GROUPS + 1]


def _attn_kernel(q_ref, kp_ref, kc_ref, kn_ref, vp_ref, vc_ref, vn_ref, o_ref, lse_ref,
                 kw_ref, vw_ref, *, tq, n_rows):
    i = pl.program_id(2)
    kw_ref[0:HALF_WIN] = kp_ref[0]
    kw_ref[HALF_WIN:HALF_WIN + tq] = kc_ref[0]
    kw_ref[HALF_WIN + tq:2 * HALF_WIN + tq] = kn_ref[0]
    vw_ref[0:HALF_WIN] = vp_ref[0]
    vw_ref[HALF_WIN:HALF_WIN + tq] = vc_ref[0]
    vw_ref[HALF_WIN + tq:2 * HALF_WIN + tq] = vn_ref[0]

    win = Q_SUB + 2 * HALF_WIN
    qi = lax.broadcasted_iota(i32, (Q_SUB, win), 0)
    kj = lax.broadcasted_iota(i32, (Q_SUB, win), 1)
    band = (kj - qi >= 0) & (kj - qi <= 2 * HALF_WIN)
    head_of_lane = lax.broadcasted_iota(i32, (1, GROUP_W), 1) // HEAD_DIM

    for s in range(tq // Q_SUB):
        qs = q_ref[0, s * Q_SUB:(s + 1) * Q_SUB, :]
        kwin = kw_ref[s * Q_SUB:s * Q_SUB + win, :]
        vwin = vw_ref[s * Q_SUB:s * Q_SUB + win, :]
        key_row = i * tq + (s * Q_SUB - HALF_WIN) + kj
        ok = band & (key_row >= 0) & (key_row < n_rows)
        o_acc = jnp.zeros((Q_SUB, GROUP_W), f32)
        l_acc = jnp.zeros((Q_SUB, GROUP_W), f32)
        for h in range(HEADS_PER_GROUP):
            hm = head_of_lane == h
            qh = jnp.where(hm, qs, jnp.zeros_like(qs))
            sc = lax.dot_general(qh, kwin, (((1,), (1,)), ((), ())), preferred_element_type=f32)
            sc = jnp.where(ok, sc, NEG_INF)
            m = jnp.max(sc, axis=-1, keepdims=True)
            p = jnp.exp(sc - m)
            den = jnp.sum(p, axis=-1, keepdims=True)
            pv = jnp.dot(p.astype(bf16), vwin, preferred_element_type=f32)
            o_acc = jnp.where(hm, pv / den, o_acc)
            l_acc = jnp.where(hm, m + jnp.log(den), l_acc)
        o_ref[0, s * Q_SUB:(s + 1) * Q_SUB, :] = o_acc
        lse_ref[0, s * Q_SUB:(s + 1) * Q_SUB, :] = l_acc


def _attention_group(q, k, v, batch, seq, g):
    d = DILATIONS[g]
    n_rows = seq // d
    tq = min(512, n_rows)
    nblk = n_rows // tq
    hb = tq // HALF_WIN
    n_halo_blocks = n_rows // HALF_WIN
    view = lambda t: t.reshape(batch, n_rows, d * GROUP_W)
    cur = pl.BlockSpec((1, tq, GROUP_W), lambda b, r, i: (b, i, r))
    prev = pl.BlockSpec((1, HALF_WIN, GROUP_W), lambda b, r, i: (b, jnp.maximum(i * hb - 1, 0), r))
    nxt = pl.BlockSpec((1, HALF_WIN, GROUP_W),
                       lambda b, r, i: (b, jnp.minimum((i + 1) * hb, n_halo_blocks - 1), r))
    o, lse = pl.pallas_call(
        functools.partial(_attn_kernel, tq=tq, n_rows=n_rows),
        grid=(batch, d, nblk),
        in_specs=[cur, prev, cur, nxt, prev, cur, nxt],
        out_specs=[cur, cur],
        out_shape=[jax.ShapeDtypeStruct((batch, n_rows, d * GROUP_W), f32)] * 2,
        scratch_shapes=[pltpu.VMEM((tq + 2 * HALF_WIN, GROUP_W), bf16)] * 2,
        compiler_params=_cparams(("parallel", "parallel", "parallel")),
        name=f"attn_g{g}",
    )(view(q), view(k), view(k), view(k), view(v), view(v), view(v))
    return o.reshape(batch * n_rows, d * GROUP_W), lse.reshape(batch * n_rows, d * GROUP_W)


def _merge_kernel(x_ref, up_ref, uc_ref, un_ref, gate_ref,
                  o0_ref, o1_ref, o2_ref, l0_ref, l1_ref, l2_ref,
                  dww_ref, dwb_ref, lng_ref, lnb_ref, pww_ref, pwb_ref, wup_ref, wout_ref, n2g_ref, wq_ref,
                  x1_ref, h2_ref, qp_ref, ue_ref, cv_ref, lo_ref, hi_ref, *, nsb):
    i = pl.program_id(0)
    t = T_PROJ

    def token_major(ref, g):
        d = DILATIONS[g]
        if d == 1:
            return ref[...]
        for r in range(d):
            lo_ref[pl.ds(r, t // d, stride=d), :] = ref[:, r * GROUP_W:r * GROUP_W + LANES]
            hi_ref[pl.ds(r, t // d, stride=d), :] = ref[:, r * GROUP_W + LANES:(r + 1) * GROUP_W]
        return jnp.concatenate([lo_ref[...], hi_ref[...]], axis=1)

    first = (i % nsb) == 0
    last = (i % nsb) == nsb - 1
    ue_ref[0:CONV_HALO] = jnp.where(first, 0.0, up_ref[...])
    ue_ref[CONV_HALO:CONV_HALO + t] = uc_ref[...]
    ue_ref[CONV_HALO + t:2 * CONV_HALO + t] = jnp.where(last, 0.0, un_ref[...])

    rc = 32
    off = CONV_HALO - CONV_K // 2
    for r0 in range(0, t, rc):
        acc = jnp.zeros((rc, CONV_W), f32)
        for j in range(CONV_K):
            acc = acc + ue_ref[r0 + off + j:r0 + off + j + rc, :] * dww_ref[j:j + 1, :]
        cv_ref[r0:r0 + rc, :] = acc
    c = cv_ref[...] + dwb_ref[...]
    mu = jnp.mean(c, axis=-1, keepdims=True)
    cc = c - mu
    var = jnp.mean(cc * cc, axis=-1, keepdims=True)
    un = cc * lax.rsqrt(var + EPS) * lng_ref[...] + lnb_ref[...]
    sw = un * jax.nn.sigmoid(un)
    conv = jnp.dot(sw.astype(bf16), pww_ref[...], preferred_element_type=f32) + pwb_ref[...]

    l0 = token_major(l0_ref, 0)
    l1 = token_major(l1_ref, 1)
    l2 = token_major(l2_ref, 2)
    lm = jnp.maximum(jnp.maximum(l0, l1), l2)
    e0 = jnp.exp(l0 - lm)
    e1 = jnp.exp(l1 - lm)
    e2 = jnp.exp(l2 - lm)
    comb = (e0 * token_major(o0_ref, 0) + e1 * token_major(o1_ref, 1) + e2 * token_major(o2_ref, 2)) / (e0 + e1 + e2)
    attn = jnp.dot(comb.astype(bf16), wup_ref[...], preferred_element_type=f32)

    g_attn = gate_ref[:, 0:D_MODEL].astype(f32)
    g_conv = gate_ref[:, D_MODEL:2 * D_MODEL].astype(f32)
    mixed = g_attn * attn + g_conv * conv
    x1 = x_ref[...] + jnp.dot(mixed.astype(bf16), wout_ref[...], preferred_element_type=f32)
    x1_ref[...] = x1
    ms = jnp.mean(x1 * x1, axis=-1, keepdims=True)
    h2 = x1 * lax.rsqrt(ms + EPS) * n2g_ref[...]
    for c in range(D_MODEL // LANES):
        h2_ref[pl.ds(c, t, stride=SUBLANES), :] = h2[:, c * LANES:(c + 1) * LANES]
    qp_ref[...] = jnp.dot(h2.astype(bf16), wq_ref[...], preferred_element_type=f32).astype(bf16)


def _merge(x2, seq, u, gates, os_, ls_, weights):
    n = x2.shape[0]
    t = T_PROJ
    nsb = seq // t
    hb = t // CONV_HALO
    nhalo = n // CONV_HALO
    row = lambda w: pl.BlockSpec((t, w), lambda i: (i, 0))
    prev = pl.BlockSpec((CONV_HALO, CONV_W), lambda i: (jnp.maximum(i * hb - 1, 0), 0))
    nxt = pl.BlockSpec((CONV_HALO, CONV_W), lambda i: (jnp.minimum((i + 1) * hb, nhalo - 1), 0))
    wspecs = [_const_spec(w.shape) for w in weights]
    grp = [pl.BlockSpec((t // d, d * GROUP_W), lambda i: (i, 0)) for d in DILATIONS]
    qw = weights[-1].shape[1]
    return pl.pallas_call(
        functools.partial(_merge_kernel, nsb=nsb),
        grid=(n // t,),
        in_specs=[row(D_MODEL), prev, row(CONV_W), nxt, row(2 * D_MODEL)] + grp * 2 + wspecs,
        out_specs=[row(D_MODEL), pl.BlockSpec((t * SUBLANES, LANES), lambda i: (i, 0)), row(qw)],
        out_shape=[jax.ShapeDtypeStruct((n, D_MODEL), f32), jax.ShapeDtypeStruct((n * SUBLANES, LANES), f32),
                   jax.ShapeDtypeStruct((n, qw), bf16)],
        scratch_shapes=[pltpu.VMEM((t + 2 * CONV_HALO, CONV_W), f32), pltpu.VMEM((t, CONV_W), f32),
                        pltpu.VMEM((t, LANES), f32), pltpu.VMEM((t, LANES), f32)],
        compiler_params=_cparams(("parallel",)),
        name="merge",
    )(x2, u, u, u, gates, *os_, *ls_, *weights)


def _candidate_slabs():
    slabs = [("row", 0, 0, 16), ("row", 1, 0, 8)]
    for j in range(PEER_TOPK):
        hi = PEER_TOPK // (j + 1)
        if hi > 2:
            slabs.append(("col", j, 2, hi))
    return slabs


def _route_kernel(qp_ref, keys_ref, ids_ref, gate_ref, val_ref, idx_ref, best_ref, idt_ref, gt_ref):
    t = T_PROJ
    k_iota = lax.broadcasted_iota(i32, (N_SUBKEYS, t), 0)
    for hc in range(2 * PEER_HEADS):
        q = qp_ref[:, hc * N_SUBKEYS:(hc + 1) * N_SUBKEYS]
        s = lax.dot_general(keys_ref[hc], q, (((1,), (1,)), ((), ())), preferred_element_type=f32)
        for r in range(PEER_TOPK):
            m = jnp.max(s, axis=0, keepdims=True)
            am = jnp.min(jnp.where(s == m, k_iota, N_SUBKEYS), axis=0, keepdims=True)
            s = jnp.where(k_iota == am, -jnp.inf, s)
            val_ref[hc, pl.ds(r, 1), :] = m
            idx_ref[hc, pl.ds(r, 1), :] = am

    r_iota = lax.broadcasted_iota(i32, (PEER_TOPK, t), 0)
    slabs = _candidate_slabs()
    for h in range(PEER_HEADS):
        v0 = val_ref[2 * h]
        v1 = val_ref[2 * h + 1]
        i0 = idx_ref[2 * h]
        i1 = idx_ref[2 * h + 1]
        cands, flats, eids = [], [], []
        for kind, fixed, lo, hi in slabs:
            if kind == "row":
                c = v0[fixed:fixed + 1, :] + v1
                fl = fixed * PEER_TOPK + r_iota
                ei = i0[fixed:fixed + 1, :] * N_SUBKEYS + i1
            else:
                c = v0 + v1[fixed:fixed + 1, :]
                fl = r_iota * PEER_TOPK + fixed
                ei = i0 * N_SUBKEYS + i1[fixed:fixed + 1, :]
            valid = (r_iota >= lo) & (r_iota < hi)
            cands.append(jnp.where(valid, c, -jnp.inf))
            flats.append(fl)
            eids.append(ei)
        big = PEER_TOPK * PEER_TOPK
        for r in range(PEER_TOPK):
            m = functools.reduce(jnp.maximum, cands)
            m = jnp.max(m, axis=0, keepdims=True)
            fsel = functools.reduce(jnp.minimum, [jnp.where(c == m, fl, big) for c, fl in zip(cands, flats)])
            fsel = jnp.min(fsel, axis=0, keepdims=True)
            hit = [fl == fsel for fl in flats]
            eid = functools.reduce(jnp.maximum, [jnp.where(hh, ei, -1) for hh, ei in zip(hit, eids)])
            eid = jnp.max(eid, axis=0, keepdims=True)
            cands = [jnp.where(hh, -jnp.inf, c) for hh, c in zip(hit, cands)]
            best_ref[pl.ds(r, 1), :] = m
            idt_ref[pl.ds(h * PEER_TOPK + r, 1), :] = eid
        b = best_ref[...]
        e = jnp.exp(b - jnp.max(b, axis=0, keepdims=True))
        gt_ref[h * PEER_TOPK:(h + 1) * PEER_TOPK, :] = e / jnp.sum(e, axis=0, keepdims=True)
    ids_ref[...] = idt_ref[...].T
    gate_ref[...] = gt_ref[...].T


def _route(qp, keys_bf):
    n, qw = qp.shape
    t = T_PROJ
    return pl.pallas_call(
        _route_kernel,
        grid=(n // t,),
        in_specs=[pl.BlockSpec((t, qw), lambda i: (i, 0)), _const_spec(keys_bf.shape)],
        out_specs=[pl.BlockSpec((t, PEER_SLOTS), lambda i: (i, 0))] * 2,
        out_shape=[jax.ShapeDtypeStruct((n, PEER_SLOTS), i32), jax.ShapeDtypeStruct((n, PEER_SLOTS), f32)],
        scratch_shapes=[pltpu.VMEM((2 * PEER_HEADS, PEER_TOPK, t), f32), pltpu.VMEM((2 * PEER_HEADS, PEER_TOPK, t), i32),
                        pltpu.VMEM((PEER_TOPK, t), f32), pltpu.VMEM((PEER_SLOTS, t), i32),
                        pltpu.VMEM((PEER_SLOTS, t), f32)],
        compiler_params=_cparams(("parallel",)),
        name="route",
    )(qp, keys_bf)


def _pack_table(tbl):
    bits = lax.bitcast_convert_type(tbl.astype(bf16), jnp.uint16).astype(jnp.uint32)
    half = D_MODEL // 2
    words = bits[:, :half] | (bits[:, half:] << 16)
    return lax.bitcast_convert_type(words, i32).reshape(tbl.shape[0], ROW_WORDS, LANES)


def _unpack(words):
    lo = pltpu.bitcast(words << 16, f32)
    hi = pltpu.bitcast(words & jnp.int32(-65536), f32)
    return lo, hi


def _split2(x):
    hi = x.astype(bf16)
    lo = (x - hi.astype(f32)).astype(bf16)
    return hi, lo


def _gather_chunk(ids_ref, tbl_ref, tile_ref, c, base):
    e = None
    for k in range(GATHER_CHUNK):
        e = ids_ref[base + k]
        tile_ref[c, k * ROW_WORDS:(k + 1) * ROW_WORDS, :] = tbl_ref[e]
    return e


def _two_token_pipeline(ids_ref, tbl_ref, tile_a, tile_b, consume, init):
    tile_b[...] = jnp.zeros(tile_b.shape, i32)
    last = T_PEER - 1

    def half_step(t_gather, gather_tile, t_consume, consume_tile, dep):
        base = jnp.minimum(t_gather, last) * PEER_SLOTS
        tc = jnp.clip(t_consume, 0, last)
        acc = init
        for c in range(N_CHUNKS):
            dep = _gather_chunk(ids_ref, tbl_ref, gather_tile, c, base + c * GATHER_CHUNK + (dep >> 31))
            acc = consume(consume_tile, c, tc, acc)
        return dep

    def pairs_step(i, dep):
        for p in range(PAIRS_PER_STEP):
            even = 2 * (PAIRS_PER_STEP * i + p)
            dep = half_step(even, tile_a, even - 1, tile_b, dep)
            dep = half_step(even + 1, tile_b, even, tile_a, dep)
        return dep

    lax.fori_loop(0, T_PEER // (2 * PAIRS_PER_STEP) + 1, pairs_step, jnp.int32(0))


def _tile_scratch():
    return pltpu.VMEM((N_CHUNKS, GATHER_CHUNK * ROW_WORDS, LANES), i32)


def _peer_u_kernel(ids_ref, xr_ref, g_ref, tbl_ref, w_ref, tile_a, tile_b):
    crow = GATHER_CHUNK * ROW_WORDS
    half = crow // 2
    kk = lax.broadcasted_iota(i32, (2 * LANES, LANES), 0)
    nn = lax.broadcasted_iota(i32, (2 * LANES, LANES), 1)
    summer = ((kk < LANES) == (nn % GATHER_CHUNK < GATHER_CHUNK // 2)).astype(bf16)
    rr = lax.broadcasted_iota(i32, (half, LANES), 0)
    ll = lax.broadcasted_iota(i32, (half, LANES), 1)
    own_row = ll % (GATHER_CHUNK // 2) == rr // ROW_WORDS
    lane_chunk = ll // GATHER_CHUNK

    def consume(tile, c, t, acc):
        xt = xr_ref[pl.ds(pl.multiple_of(t * SUBLANES, SUBLANES), SUBLANES), :]
        xlo = xt[0:ROW_WORDS]
        xhi = xt[ROW_WORDS:2 * ROW_WORDS]
        x2lo = pltpu.repeat(jnp.concatenate([xlo, xlo], axis=0), crow // SUBLANES, axis=0)
        x2hi = pltpu.repeat(jnp.concatenate([xhi, xhi], axis=0), crow // SUBLANES, axis=0)
        lo, hi = _unpack(tile[c])
        p = lo * x2lo + hi * x2hi
        p_hi, p_lo = _split2(jnp.concatenate([p[0:half], p[half:crow]], axis=1))
        rs = jnp.dot(p_hi, summer, preferred_element_type=f32) + jnp.dot(p_lo, summer, preferred_element_type=f32)
        acc = acc + jnp.sum(jnp.where(own_row & (lane_chunk == c), rs, 0.0), axis=0, keepdims=True)
        if c == N_CHUNKS - 1:
            gelu = 0.5 * acc * (1.0 + lax.erf(acc * (2.0 ** -0.5)))
            w_ref[t] = g_ref[t] * gelu
        return acc

    _two_token_pipeline(ids_ref, tbl_ref, tile_a, tile_b, consume, jnp.zeros((1, PEER_SLOTS), f32))


def _peer_u(ids_flat, h2r, gate3, tbl, tok_off, n_tok):
    t = T_PEER
    b0 = tok_off // t
    return pl.pallas_call(
        _peer_u_kernel,
        grid=(n_tok // t,),
        in_specs=[pl.BlockSpec((t * PEER_SLOTS,), lambda i: (i + b0,), memory_space=pltpu.SMEM),
                  pl.BlockSpec((t * SUBLANES, LANES), lambda i: (i + b0, 0)),
                  pl.BlockSpec((t, 1, PEER_SLOTS), lambda i: (i + b0, 0, 0)),
                  _const_spec(tbl.shape)],
        out_specs=pl.BlockSpec((t, 1, PEER_SLOTS), lambda i: (i, 0, 0)),
        out_shape=jax.ShapeDtypeStruct((n_tok, 1, PEER_SLOTS), f32),
        scratch_shapes=[_tile_scratch(), _tile_scratch()],
        compiler_params=_cparams(("parallel",)),
        name="peer_u",
    )(ids_flat, h2r, gate3, tbl)


def _peer_v_sc(ids, w, tbl_words, tok_off):
    n_tok = w.shape[0]
    tpw = n_tok // SC_WORKERS
    lanes = SC_LANES
    mesh = plsc.VectorSubcoreMesh(core_axis_name="c", subcore_axis_name="s")

    @functools.partial(
        pl.kernel, mesh=mesh,
        out_type=jax.ShapeDtypeStruct((n_tok, D_MODEL), f32),
        scratch_types=[
            pltpu.VMEM((SC_HALF,), i32), pltpu.VMEM((SC_HALF,), i32),
            pltpu.VMEM((SC_HALF, ROW_WORDS_FLAT), i32), pltpu.VMEM((SC_HALF, ROW_WORDS_FLAT), i32),
            pltpu.VMEM((PEER_SLOTS,), f32),
            pltpu.VMEM((D_MODEL,), f32),
            pltpu.SemaphoreType.DMA, pltpu.SemaphoreType.DMA,
        ],
        compiler_params=dataclasses.replace(pltpu.CompilerParams(), needs_layout_passes=False),
        name="peer_v_sc",
    )
    def run(ids_hbm, w_hbm, tbl_hbm, out_hbm, idx_a, idx_b, rows_a, rows_b, w_v, out_v, sem_a, sem_b):
        wid = lax.axis_index("s") * SC_CORES + lax.axis_index("c")
        base = wid * tpw

        def accumulate(rows, w_off):
            for db in range(ROW_WORDS_FLAT // SC_DIM_BLOCK):
                def group_body(g, carry):
                    accs = []
                    for wc in range(SC_DIM_BLOCK // lanes):
                        accs.append(out_v[pl.ds(db * SC_DIM_BLOCK + wc * lanes, lanes)])
                        accs.append(out_v[pl.ds(ROW_WORDS_FLAT + db * SC_DIM_BLOCK + wc * lanes, lanes)])
                    j0 = g * SC_ROW_GROUP
                    wchunk = w_v[pl.ds(pl.multiple_of((w_off + j0) // lanes * lanes, lanes), lanes)]
                    sub = (w_off + j0) % lanes
                    for r in range(SC_ROW_GROUP):
                        wj = jnp.take(wchunk, jnp.full((lanes,), sub + r, i32))
                        for wc in range(SC_DIM_BLOCK // lanes):
                            word = rows[j0 + r, pl.ds(db * SC_DIM_BLOCK + wc * lanes, lanes)]
                            lo = lax.bitcast_convert_type(word << 16, f32)
                            hi = lax.bitcast_convert_type(word & jnp.int32(-65536), f32)
                            accs[2 * wc] = accs[2 * wc] + wj * lo
                            accs[2 * wc + 1] = accs[2 * wc + 1] + wj * hi
                    for wc in range(SC_DIM_BLOCK // lanes):
                        out_v[pl.ds(db * SC_DIM_BLOCK + wc * lanes, lanes)] = accs[2 * wc]
                        out_v[pl.ds(ROW_WORDS_FLAT + db * SC_DIM_BLOCK + wc * lanes, lanes)] = accs[2 * wc + 1]
                    return carry
                lax.fori_loop(0, SC_HALF // SC_ROW_GROUP, group_body, 0)

        def token(i, carry):
            t = base + i
            pltpu.sync_copy(ids_hbm.at[tok_off + t, pl.ds(0, SC_HALF)], idx_a)
            pltpu.sync_copy(ids_hbm.at[tok_off + t, pl.ds(SC_HALF, SC_HALF)], idx_b)
            copy_a = pltpu.async_copy(tbl_hbm.at[idx_a], rows_a, sem_a)
            copy_b = pltpu.async_copy(tbl_hbm.at[idx_b], rows_b, sem_b)
            pltpu.sync_copy(w_hbm.at[t], w_v)
            for q in range(D_MODEL // lanes):
                out_v[pl.ds(q * lanes, lanes)] = jnp.zeros((lanes,), f32)
            copy_a.wait()
            accumulate(rows_a, 0)
            copy_b.wait()
            accumulate(rows_b, SC_HALF)
            pltpu.sync_copy(out_v, out_hbm.at[t])
            return carry

        lax.fori_loop(0, tpw, token, 0)

    return run(ids, w, tbl_words)


def _final_kernel(x1_ref, p_ref, g_ref, y_ref):
    x = x1_ref[...] + p_ref[...]
    ms = jnp.mean(x * x, axis=-1, keepdims=True)
    y_ref[...] = x * lax.rsqrt(ms + EPS) * g_ref[...]


def _final(x1, peer_out, final_g, tok_off):
    n = peer_out.shape[0]
    t = T_PROJ
    b0 = tok_off // t
    row = pl.BlockSpec((t, D_MODEL), lambda i: (i, 0))
    return pl.pallas_call(
        _final_kernel,
        grid=(n // t,),
        in_specs=[pl.BlockSpec((t, D_MODEL), lambda i: (i + b0, 0)), row, _const_spec((1, D_MODEL))],
        out_specs=row,
        out_shape=jax.ShapeDtypeStruct((n, D_MODEL), f32),
        compiler_params=_cparams(("parallel",)),
        name="final_norm",
    )(x1, peer_out, final_g)


def _rotary_tables(seq):
    half = ROT_DIM // 2
    inv = ROPE_THETA ** (-jnp.arange(half, dtype=f32) * 2.0 / ROT_DIM)
    ang = jnp.arange(seq, dtype=jnp.int32).astype(f32)[:, None] * inv[None, :]
    cos = jnp.cos(ang)
    sin = jnp.sin(ang)
    pad = HEAD_DIM - ROT_DIM
    one = jnp.ones((seq, pad), f32)
    zero = jnp.zeros((seq, pad), f32)
    zh = jnp.zeros((seq, half), f32)
    cos_h = jnp.concatenate([cos, cos, one], axis=1)
    sa_h = jnp.concatenate([-sin, zh, zero], axis=1)
    sb_h = jnp.concatenate([zh, sin, zero], axis=1)
    rep = LANES // HEAD_DIM
    return tuple(jnp.tile(t, (1, rep)) for t in (cos_h, sa_h, sb_h))


def _front_ops(x, params, tables, st):
    (norm1_g, w_in_bf, b_gate, merge_w, keys_bf, _, _, _) = params
    batch, seq, _ = x.shape
    n = batch * seq
    x2 = x.reshape(n, D_MODEL)

    def inproj():
        st["q"], st["k"], st["v"], st["u"], st["gates"] = _inproj(x2, seq, norm1_g, w_in_bf, b_gate, *tables)
        st["o"], st["lse"] = [], []

    def attention(g):
        o, lse = _attention_group(st["q"][g], st["k"][g], st["v"][g], batch, seq, g)
        st["o"].append(o)
        st["lse"].append(lse)

    def merge():
        st["x1"], st["h2r"], st["qp"] = _merge(x2, seq, st["u"], st["gates"], st["o"], st["lse"], merge_w)

    def route():
        ids, gate = _route(st["qp"], keys_bf)
        st["ids"], st["gate3"], st["n"] = ids, gate.reshape(n, 1, PEER_SLOTS), n

    return [inproj] + [functools.partial(attention, g) for g in range(N_GROUPS)] + [merge, route]


def _peer_ops(params, st, pending):
    u_tbl, v_tbl = params[5], params[6]

    def chunk(off, n_chunk):
        ids = st["ids"]
        w3 = _peer_u(ids.reshape(st["n"] * PEER_SLOTS), st["h2r"], st["gate3"], u_tbl, off, n_chunk)
        pv = _peer_v_sc(ids, w3.reshape(n_chunk, PEER_SLOTS), v_tbl, off)
        pending.append((st["x1"], pv, off))

    ops, off = [], 0
    for share in PEER_CHUNK_SHARES:
        n_chunk = st["tokens"] * share // sum(PEER_CHUNK_SHARES)
        ops.append(functools.partial(chunk, off, n_chunk))
        off += n_chunk
    return ops


def kernel(x_prompt, x_sample, norm1_g, w_in, b_gate, w_attn_up, conv_dw_w, conv_dw_b, conv_ln_g, conv_ln_b,
           conv_pw_w, conv_pw_b, w_out, norm2_g, peer_wq, peer_keys, peer_u, peer_v, final_g):
    assert w_in.shape[0] == 1, "one encoder layer followed by the final norm"
    row = lambda a: a.reshape(1, -1)
    tables = _rotary_tables(max(x_prompt.shape[1], x_sample.shape[1]))
    l = 0
    merge_w = (conv_dw_w[l], row(conv_dw_b[l]), row(conv_ln_g[l]), row(conv_ln_b[l]),
               conv_pw_w[l].astype(bf16), row(conv_pw_b[l]), w_attn_up[l].astype(bf16),
               w_out[l].astype(bf16), row(norm2_g[l]), peer_wq[l].astype(bf16))
    keys_bf = peer_keys[l].astype(bf16).reshape(2 * PEER_HEADS, N_SUBKEYS, -1)
    params = (row(norm1_g[l]), w_in[l].astype(bf16), row(b_gate[l]), merge_w, keys_bf,
              _pack_table(peer_u[l]), _pack_table(peer_v[l]).reshape(N_EXPERTS, ROW_WORDS_FLAT), row(final_g))

    subs = []
    for x, shares in zip((x_prompt, x_sample), BATCH_SHARES):
        b = 0
        for share in shares:
            step = x.shape[0] * share // sum(shares)
            subs.append(x[b:b + step])
            b += step
    states = [dict(tokens=sub.shape[0] * sub.shape[1]) for sub in subs]
    pendings = [[] for _ in subs]
    for op in _front_ops(subs[0], params, tables, states[0]):
        op()
    for i in range(len(subs)):
        peer = _peer_ops(params, states[i], pendings[i])
        front = _front_ops(subs[i + 1], params, tables, states[i + 1]) if i + 1 < len(subs) else []
        for j in range(max(len(peer), len(front))):
            if j < len(peer):
                peer[j]()
            if j < len(front):
                front[j]()

    def finish(x, pending):
        ys = [_final(x1, pv, params[-1], off) for x1, pv, off in pending]
        return jnp.concatenate(ys, axis=0).reshape(x.shape)

    half = len(BATCH_SHARES[0])
    return (finish(x_prompt, sum(pendings[:half], [])), finish(x_sample, sum(pendings[half:], [])))
```

```python
import dataclasses
import functools
import math

import numpy as np
import jax
import jax.numpy as jnp
from jax import lax
from jax.experimental import pallas as pl
from jax.experimental.pallas import tpu as pltpu
from jax.experimental.pallas import tpu_sc as plsc

f32 = jnp.float32
bf16 = jnp.bfloat16
i32 = jnp.int32

D_MODEL = 1024
HEAD_DIM = 64
N_GROUPS = 3
HEADS_PER_GROUP = 4
GROUP_W = HEADS_PER_GROUP * HEAD_DIM
ATTN_W = N_GROUPS * GROUP_W
WINDOWS = (128, 512, 2048)
DILATIONS = (1, 4, 16)
HALF_WIN = 64
ROT_DIM = HEAD_DIM // 4
ROPE_THETA = 500000.0
NEG_INF = -1e30
CONV_W = 512
CONV_K = 31
CONV_HALO = 16
PEER_HEADS = 8
N_SUBKEYS = 128
N_EXPERTS = N_SUBKEYS * N_SUBKEYS
PEER_TOPK = 16
PEER_SLOTS = PEER_HEADS * PEER_TOPK
EPS = 1e-6

LANES = 128
SUBLANES = 8
ROW_WORDS = D_MODEL // 2 // LANES
VMEM_LIMIT = 56 * 1024 * 1024

T_PROJ = 256
T_PEER = 512
PAIRS_PER_STEP = 16
GATHER_CHUNK = 32
N_CHUNKS = (8 * 16) // GATHER_CHUNK
BATCH_SHARES = ((2, 2, 2, 2), (2, 2, 2, 1, 1))
PEER_CHUNK_SHARES = (2,) * 8

SC_CORES = 2
SC_SUBCORES = 16
SC_LANES = 16
SC_WORKERS = SC_CORES * SC_SUBCORES
ROW_WORDS_FLAT = D_MODEL // 2
SC_HALF = PEER_SLOTS // 2
SC_DIM_BLOCK = 128
SC_ROW_GROUP = 8
Q_SUB = 128


def _cparams(sem):
    return pltpu.CompilerParams(dimension_semantics=sem, vmem_limit_bytes=VMEM_LIMIT)


def _const_spec(shape):
    nd = len(shape)
    return pl.BlockSpec(shape, lambda *_: (0,) * nd, pipeline_mode=pl.Buffered(1))


def _inproj_kernel(x_ref, g_ref, w_ref, bg_ref, cos_ref, sa_ref, sb_ref, *refs):
    qkv_refs = (refs[0:N_GROUPS], refs[N_GROUPS:2 * N_GROUPS], refs[2 * N_GROUPS:3 * N_GROUPS])
    u_ref, gate_ref, stage_ref = refs[3 * N_GROUPS:]
    x = x_ref[...]
    ms = jnp.mean(x * x, axis=-1, keepdims=True)
    h = (x * lax.rsqrt(ms + EPS) * g_ref[...]).astype(bf16)

    def proj(lo, hi):
        return jnp.dot(h, w_ref[:, lo:hi], preferred_element_type=f32)

    cos = cos_ref[...]
    sa = sa_ref[...]
    sb = sb_ref[...]

    def rotary(tc, scale):
        r = tc * cos + pltpu.roll(tc, LANES - ROT_DIM // 2, 1) * sa + pltpu.roll(tc, ROT_DIM // 2, 1) * sb
        return r * scale

    def emit(t, out_refs, fn):
        for c in range(ATTN_W // LANES):
            g, half = divmod(c, GROUP_W // LANES)
            d = DILATIONS[g]
            chunk = fn(t[:, c * LANES:(c + 1) * LANES])
            if d == 1:
                out_refs[g][:, half * LANES:(half + 1) * LANES] = chunk.astype(bf16)
                continue
            stage_ref[...] = chunk
            for r in range(d):
                col = r * GROUP_W + half * LANES
                out_refs[g][:, col:col + LANES] = stage_ref[pl.ds(r, T_PROJ // d, stride=d), :].astype(bf16)

    emit(proj(0, ATTN_W), qkv_refs[0], lambda tc: rotary(tc, HEAD_DIM ** -0.5))
    emit(proj(ATTN_W, 2 * ATTN_W), qkv_refs[1], lambda tc: rotary(tc, 1.0))
    emit(proj(2 * ATTN_W, 3 * ATTN_W), qkv_refs[2], lambda tc: tc)
    c0 = 3 * ATTN_W
    a = proj(c0, c0 + CONV_W)
    b = proj(c0 + CONV_W, c0 + 2 * CONV_W)
    u_ref[...] = a * jax.nn.sigmoid(b)
    gates = proj(c0 + 2 * CONV_W, c0 + 2 * CONV_W + 2 * D_MODEL) + bg_ref[...]
    gate_ref[...] = jax.nn.sigmoid(gates).astype(bf16)


def _inproj(x2, seq, norm1_g, w_in_bf, b_gate, cos_t, sa_t, sb_t):
    n = x2.shape[0]
    nsb = seq // T_PROJ
    in_cols = w_in_bf.shape[1]
    row = lambda w: pl.BlockSpec((T_PROJ, w), lambda i: (i, 0))
    pos = pl.BlockSpec((T_PROJ, LANES), lambda i: (i % nsb, 0))
    grp_specs = [pl.BlockSpec((T_PROJ // d, d * GROUP_W), lambda i: (i, 0)) for d in DILATIONS]
    grp_shapes = [jax.ShapeDtypeStruct((n // d, d * GROUP_W), bf16) for d in DILATIONS]
    outs = pl.pallas_call(
        _inproj_kernel,
        grid=(n // T_PROJ,),
        in_specs=[row(D_MODEL), _const_spec((1, D_MODEL)), _const_spec((D_MODEL, in_cols)),
                  _const_spec((1, 2 * D_MODEL)), pos, pos, pos],
        out_specs=grp_specs * 3 + [row(CONV_W), row(2 * D_MODEL)],
        out_shape=grp_shapes * 3
        + [jax.ShapeDtypeStruct((n, CONV_W), f32), jax.ShapeDtypeStruct((n, 2 * D_MODEL), bf16)],
        scratch_shapes=[pltpu.VMEM((T_PROJ, LANES), f32)],
        compiler_params=_cparams(("parallel",)),
        name="inproj",
    )(x2, norm1_g, w_in_bf, b_gate, cos_t, sa_t, sb_t)
    q, k, v = outs[0:N_GROUPS], outs[N_GROUPS:2 * N_GROUPS], outs[2 * N_GROUPS:3 * N_GROUPS]
    return q, k, v, outs[3 * N_GROUPS], outs[3 * N_GROUPS + 1]


def _attn_kernel(q_ref, kp_ref, kc_ref, kn_ref, vp_ref, vc_ref, vn_ref, o_ref, lse_ref,
                 kw_ref, vw_ref, *, tq, n_rows):
    i = pl.program_id(2)
    kw_ref[0:HALF_WIN] = kp_ref[0]
    kw_ref[HALF_WIN:HALF_WIN + tq] = kc_ref[0]
    kw_ref[HALF_WIN + tq:2 * HALF_WIN + tq] = kn_ref[0]
    vw_ref[0:HALF_WIN] = vp_ref[0]
    vw_ref[HALF_WIN:HALF_WIN + tq] = vc_ref[0]
    vw_ref[HALF_WIN + tq:2 * HALF_WIN + tq] = vn_ref[0]

    win = Q_SUB + 2 * HALF_WIN
    qi = lax.broadcasted_iota(i32, (Q_SUB, win), 0)
    kj = lax.broadcasted_iota(i32, (Q_SUB, win), 1)
    band = (kj - qi >= 0) & (kj - qi <= 2 * HALF_WIN)
    head_of_lane = lax.broadcasted_iota(i32, (1, GROUP_W), 1) // HEAD_DIM

    for s in range(tq // Q_SUB):
        qs = q_ref[0, s * Q_SUB:(s + 1) * Q_SUB, :]
        kwin = kw_ref[s * Q_SUB:s * Q_SUB + win, :]
        vwin = vw_ref[s * Q_SUB:s * Q_SUB + win, :]
        key_row = i * tq + (s * Q_SUB - HALF_WIN) + kj
        ok = band & (key_row >= 0) & (key_row < n_rows)
        o_acc = jnp.zeros((Q_SUB, GROUP_W), f32)
        l_acc = jnp.zeros((Q_SUB, GROUP_W), f32)
        for h in range(HEADS_PER_GROUP):
            hm = head_of_lane == h
            qh = jnp.where(hm, qs, jnp.zeros_like(qs))
            sc = lax.dot_general(qh, kwin, (((1,), (1,)), ((), ())), preferred_element_type=f32)
            sc = jnp.where(ok, sc, NEG_INF)
            m = jnp.max(sc, axis=-1, keepdims=True)
            p = jnp.exp(sc - m)
            den = jnp.sum(p, axis=-1, keepdims=True)
            pv = jnp.dot(p.astype(bf16), vwin, preferred_element_type=f32)
            o_acc = jnp.where(hm, pv / den, o_acc)
            l_acc = jnp.where(hm, m + jnp.log(den), l_acc)
        o_ref[0, s * Q_SUB:(s + 1) * Q_SUB, :] = o_acc
        lse_ref[0, s * Q_SUB:(s + 1) * Q_SUB, :] = l_acc


def _attention_group(q, k, v, batch, seq, g):
    d = DILATIONS[g]
    n_rows = seq // d
    tq = min(512, n_rows)
    nblk = n_rows // tq
    hb = tq // HALF_WIN
    n_halo_blocks = n_rows // HALF_WIN
    view = lambda t: t.reshape(batch, n_rows, d * GROUP_W)
    cur = pl.BlockSpec((1, tq, GROUP_W), lambda b, r, i: (b, i, r))
    prev = pl.BlockSpec((1, HALF_WIN, GROUP_W), lambda b, r, i: (b, jnp.maximum(i * hb - 1, 0), r))
    nxt = pl.BlockSpec((1, HALF_WIN, GROUP_W),
                       lambda b, r, i: (b, jnp.minimum((i + 1) * hb, n_halo_blocks - 1), r))
    o, lse = pl.pallas_call(
        functools.partial(_attn_kernel, tq=tq, n_rows=n_rows),
        grid=(batch, d, nblk),
        in_specs=[cur, prev, cur, nxt, prev, cur, nxt],
        out_specs=[cur, cur],
        out_shape=[jax.ShapeDtypeStruct((batch, n_rows, d * GROUP_W), f32)] * 2,
        scratch_shapes=[pltpu.VMEM((tq + 2 * HALF_WIN, GROUP_W), bf16)] * 2,
        compiler_params=_cparams(("parallel", "parallel", "parallel")),
        name=f"attn_g{g}",
    )(view(q), view(k), view(k), view(k), view(v), view(v), view(v))
    return o.reshape(batch * n_rows, d * GROUP_W), lse.reshape(batch * n_rows, d * GROUP_W)


def _merge_kernel(x_ref, up_ref, uc_ref, un_ref, gate_ref,
                  o0_ref, o1_ref, o2_ref, l0_ref, l1_ref, l2_ref,
                  dww_ref, dwb_ref, lng_ref, lnb_ref, pww_ref, pwb_ref, wup_ref, wout_ref, n2g_ref, wq_ref,
                  x1_ref, h2_ref, qp_ref, ue_ref, cv_ref, lo_ref, hi_ref, *, nsb):
    i = pl.program_id(0)
    t = T_PROJ

    def token_major(ref, g):
        d = DILATIONS[g]
        if d == 1:
            return ref[...]
        for r in range(d):
            lo_ref[pl.ds(r, t // d, stride=d), :] = ref[:, r * GROUP_W:r * GROUP_W + LANES]
            hi_ref[pl.ds(r, t // d, stride=d), :] = ref[:, r * GROUP_W + LANES:(r + 1) * GROUP_W]
        return jnp.concatenate([lo_ref[...], hi_ref[...]], axis=1)

    first = (i % nsb) == 0
    last = (i % nsb) == nsb - 1
    ue_ref[0:CONV_HALO] = jnp.where(first, 0.0, up_ref[...])
    ue_ref[CONV_HALO:CONV_HALO + t] = uc_ref[...]
    ue_ref[CONV_HALO + t:2 * CONV_HALO + t] = jnp.where(last, 0.0, un_ref[...])

    rc = 32
    off = CONV_HALO - CONV_K // 2
    for r0 in range(0, t, rc):
        acc = jnp.zeros((rc, CONV_W), f32)
        for j in range(CONV_K):
            acc = acc + ue_ref[r0 + off + j:r0 + off + j + rc, :] * dww_ref[j:j + 1, :]
        cv_ref[r0:r0 + rc, :] = acc
    c = cv_ref[...] + dwb_ref[...]
    mu = jnp.mean(c, axis=-1, keepdims=True)
    cc = c - mu
    var = jnp.mean(cc * cc, axis=-1, keepdims=True)
    un = cc * lax.rsqrt(var + EPS) * lng_ref[...] + lnb_ref[...]
    sw = un * jax.nn.sigmoid(un)
    conv = jnp.dot(sw.astype(bf16), pww_ref[...], preferred_element_type=f32) + pwb_ref[...]

    l0 = token_major(l0_ref, 0)
    l1 = token_major(l1_ref, 1)
    l2 = token_major(l2_ref, 2)
    lm = jnp.maximum(jnp.maximum(l0, l1), l2)
    e0 = jnp.exp(l0 - lm)
    e1 = jnp.exp(l1 - lm)
    e2 = jnp.exp(l2 - lm)
    comb = (e0 * token_major(o0_ref, 0) + e1 * token_major(o1_ref, 1) + e2 * token_major(o2_ref, 2)) / (e0 + e1 + e2)
    attn = jnp.dot(comb.astype(bf16), wup_ref[...], preferred_element_type=f32)

    g_attn = gate_ref[:, 0:D_MODEL].astype(f32)
    g_conv = gate_ref[:, D_MODEL:2 * D_MODEL].astype(f32)
    mixed = g_attn * attn + g_conv * conv
    x1 = x_ref[...] + jnp.dot(mixed.astype(bf16), wout_ref[...], preferred_element_type=f32)
    x1_ref[...] = x1
    ms = jnp.mean(x1 * x1, axis=-1, keepdims=True)
    h2 = x1 * lax.rsqrt(ms + EPS) * n2g_ref[...]
    for c in range(D_MODEL // LANES):
        h2_ref[pl.ds(c, t, stride=SUBLANES), :] = h2[:, c * LANES:(c + 1) * LANES]
    qp_ref[...] = jnp.dot(h2.astype(bf16), wq_ref[...], preferred_element_type=f32).astype(bf16)


def _merge(x2, seq, u, gates, os_, ls_, weights):
    n = x2.shape[0]
    t = T_PROJ
    nsb = seq // t
    hb = t // CONV_HALO
    nhalo = n // CONV_HALO
    row = lambda w: pl.BlockSpec((t, w), lambda i: (i, 0))
    prev = pl.BlockSpec((CONV_HALO, CONV_W), lambda i: (jnp.maximum(i * hb - 1, 0), 0))
    nxt = pl.BlockSpec((CONV_HALO, CONV_W), lambda i: (jnp.minimum((i + 1) * hb, nhalo - 1), 0))
    wspecs = [_const_spec(w.shape) for w in weights]
    grp = [pl.BlockSpec((t // d, d * GROUP_W), lambda i: (i, 0)) for d in DILATIONS]
    qw = weights[-1].shape[1]
    return pl.pallas_call(
        functools.partial(_merge_kernel, nsb=nsb),
        grid=(n // t,),
        in_specs=[row(D_MODEL), prev, row(CONV_W), nxt, row(2 * D_MODEL)] + grp * 2 + wspecs,
        out_specs=[row(D_MODEL), pl.BlockSpec((t * SUBLANES, LANES), lambda i: (i, 0)), row(qw)],
        out_shape=[jax.ShapeDtypeStruct((n, D_MODEL), f32), jax.ShapeDtypeStruct((n * SUBLANES, LANES), f32),
                   jax.ShapeDtypeStruct((n, qw), bf16)],
        scratch_shapes=[pltpu.VMEM((t + 2 * CONV_HALO, CONV_W), f32), pltpu.VMEM((t, CONV_W), f32),
                        pltpu.VMEM((t, LANES), f32), pltpu.VMEM((t, LANES), f32)],
        compiler_params=_cparams(("parallel",)),
        name="merge",
    )(x2, u, u, u, gates, *os_, *ls_, *weights)


def _candidate_slabs():
    slabs = [("row", 0, 0, 16), ("row", 1, 0, 8)]
    for j in range(PEER_TOPK):
        hi = PEER_TOPK // (j + 1)
        if hi > 2:
            slabs.append(("col", j, 2, hi))
    return slabs


def _route_kernel(qp_ref, keys_ref, ids_ref, gate_ref, val_ref, idx_ref, best_ref, idt_ref, gt_ref):
    t = T_PROJ
    k_iota = lax.broadcasted_iota(i32, (N_SUBKEYS, t), 0)
    for hc in range(2 * PEER_HEADS):
        q = qp_ref[:, hc * N_SUBKEYS:(hc + 1) * N_SUBKEYS]
        s = lax.dot_general(keys_ref[hc], q, (((1,), (1,)), ((), ())), preferred_element_type=f32)
        for r in range(PEER_TOPK):
            m = jnp.max(s, axis=0, keepdims=True)
            am = jnp.min(jnp.where(s == m, k_iota, N_SUBKEYS), axis=0, keepdims=True)
            s = jnp.where(k_iota == am, -jnp.inf, s)
            val_ref[hc, pl.ds(r, 1), :] = m
            idx_ref[hc, pl.ds(r, 1), :] = am

    r_iota = lax.broadcasted_iota(i32, (PEER_TOPK, t), 0)
    slabs = _candidate_slabs()
    for h in range(PEER_HEADS):
        v0 = val_ref[2 * h]
        v1 = val_ref[2 * h + 1]
        i0 = idx_ref[2 * h]
        i1 = idx_ref[2 * h + 1]
        cands, flats, eids = [], [], []
        for kind, fixed, lo, hi in slabs:
            if kind == "row":
                c = v0[fixed:fixed + 1, :] + v1
                fl = fixed * PEER_TOPK + r_iota
                ei = i0[fixed:fixed + 1, :] * N_SUBKEYS + i1
            else:
                c = v0 + v1[fixed:fixed + 1, :]
                fl = r_iota * PEER_TOPK + fixed
                ei = i0 * N_SUBKEYS + i1[fixed:fixed + 1, :]
            valid = (r_iota >= lo) & (r_iota < hi)
            cands.append(jnp.where(valid, c, -jnp.inf))
            flats.append(fl)
            eids.append(ei)
        big = PEER_TOPK * PEER_TOPK
        for r in range(PEER_TOPK):
            m = functools.reduce(jnp.maximum, cands)
            m = jnp.max(m, axis=0, keepdims=True)
            fsel = functools.reduce(jnp.minimum, [jnp.where(c == m, fl, big) for c, fl in zip(cands, flats)])
            fsel = jnp.min(fsel, axis=0, keepdims=True)
            hit = [fl == fsel for fl in flats]
            eid = functools.reduce(jnp.maximum, [jnp.where(hh, ei, -1) for hh, ei in zip(hit, eids)])
            eid = jnp.max(eid, axis=0, keepdims=True)
            cands = [jnp.where(hh, -jnp.inf, c) for hh, c in zip(hit, cands)]
            best_ref[pl.ds(r, 1), :] = m
            idt_ref[pl.ds(h * PEER_TOPK + r, 1), :] = eid
        b = best_ref[...]
        e = jnp.exp(b - jnp.max(b, axis=0, keepdims=True))
        gt_ref[h * PEER_TOPK:(h + 1) * PEER_TOPK, :] = e / jnp.sum(e, axis=0, keepdims=True)
    ids_ref[...] = idt_ref[...].T
    gate_ref[...] = gt_ref[...].T


def _route(qp, keys_bf):
    n, qw = qp.shape
    t = T_PROJ
    return pl.pallas_call(
        _route_kernel,
        grid=(n // t,),
        in_specs=[pl.BlockSpec((t, qw), lambda i: (i, 0)), _const_spec(keys_bf.shape)],
        out_specs=[pl.BlockSpec((t, PEER_SLOTS), lambda i: (i, 0))] * 2,
        out_shape=[jax.ShapeDtypeStruct((n, PEER_SLOTS), i32), jax.ShapeDtypeStruct((n, PEER_SLOTS), f32)],
        scratch_shapes=[pltpu.VMEM((2 * PEER_HEADS, PEER_TOPK, t), f32), pltpu.VMEM((2 * PEER_HEADS, PEER_TOPK, t), i32),
                        pltpu.VMEM((PEER_TOPK, t), f32), pltpu.VMEM((PEER_SLOTS, t), i32),
                        pltpu.VMEM((PEER_SLOTS, t), f32)],
        compiler_params=_cparams(("parallel",)),
        name="route",
    )(qp, keys_bf)


def _pack_table(tbl):
    bits = lax.bitcast_convert_type(tbl.astype(bf16), jnp.uint16).astype(jnp.uint32)
    half = D_MODEL // 2
    words = bits[:, :half] | (bits[:, half:] << 16)
    return lax.bitcast_convert_type(words, i32).reshape(tbl.shape[0], ROW_WORDS, LANES)


def _unpack(words):
    lo = pltpu.bitcast(words << 16, f32)
    hi = pltpu.bitcast(words & jnp.int32(-65536), f32)
    return lo, hi


def _split2(x):
    hi = x.astype(bf16)
    lo = (x - hi.astype(f32)).astype(bf16)
    return hi, lo


def _gather_chunk(ids_ref, tbl_ref, tile_ref, c, base):
    e = None
    for k in range(GATHER_CHUNK):
        e = ids_ref[base + k]
        tile_ref[c, k * ROW_WORDS:(k + 1) * ROW_WORDS, :] = tbl_ref[e]
    return e


def _two_token_pipeline(ids_ref, tbl_ref, tile_a, tile_b, consume, init):
    tile_b[...] = jnp.zeros(tile_b.shape, i32)
    last = T_PEER - 1

    def half_step(t_gather, gather_tile, t_consume, consume_tile, dep):
        base = jnp.minimum(t_gather, last) * PEER_SLOTS
        tc = jnp.clip(t_consume, 0, last)
        acc = init
        for c in range(N_CHUNKS):
            dep = _gather_chunk(ids_ref, tbl_ref, gather_tile, c, base + c * GATHER_CHUNK + (dep >> 31))
            acc = consume(consume_tile, c, tc, acc)
        return dep

    def pairs_step(i, dep):
        for p in range(PAIRS_PER_STEP):
            even = 2 * (PAIRS_PER_STEP * i + p)
            dep = half_step(even, tile_a, even - 1, tile_b, dep)
            dep = half_step(even + 1, tile_b, even, tile_a, dep)
        return dep

    lax.fori_loop(0, T_PEER // (2 * PAIRS_PER_STEP) + 1, pairs_step, jnp.int32(0))


def _tile_scratch():
    return pltpu.VMEM((N_CHUNKS, GATHER_CHUNK * ROW_WORDS, LANES), i32)


def _peer_u_kernel(ids_ref, xr_ref, g_ref, tbl_ref, w_ref, tile_a, tile_b):
    crow = GATHER_CHUNK * ROW_WORDS
    half = crow // 2
    kk = lax.broadcasted_iota(i32, (2 * LANES, LANES), 0)
    nn = lax.broadcasted_iota(i32, (2 * LANES, LANES), 1)
    summer = ((kk < LANES) == (nn % GATHER_CHUNK < GATHER_CHUNK // 2)).astype(bf16)
    rr = lax.broadcasted_iota(i32, (half, LANES), 0)
    ll = lax.broadcasted_iota(i32, (half, LANES), 1)
    own_row = ll % (GATHER_CHUNK // 2) == rr // ROW_WORDS
    lane_chunk = ll // GATHER_CHUNK

    def consume(tile, c, t, acc):
        xt = xr_ref[pl.ds(pl.multiple_of(t * SUBLANES, SUBLANES), SUBLANES), :]
        xlo = xt[0:ROW_WORDS]
        xhi = xt[ROW_WORDS:2 * ROW_WORDS]
        x2lo = pltpu.repeat(jnp.concatenate([xlo, xlo], axis=0), crow // SUBLANES, axis=0)
        x2hi = pltpu.repeat(jnp.concatenate([xhi, xhi], axis=0), crow // SUBLANES, axis=0)
        lo, hi = _unpack(tile[c])
        p = lo * x2lo + hi * x2hi
        p_hi, p_lo = _split2(jnp.concatenate([p[0:half], p[half:crow]], axis=1))
        rs = jnp.dot(p_hi, summer, preferred_element_type=f32) + jnp.dot(p_lo, summer, preferred_element_type=f32)
        acc = acc + jnp.sum(jnp.where(own_row & (lane_chunk == c), rs, 0.0), axis=0, keepdims=True)
        if c == N_CHUNKS - 1:
            gelu = 0.5 * acc * (1.0 + lax.erf(acc * (2.0 ** -0.5)))
            w_ref[t] = g_ref[t] * gelu
        return acc

    _two_token_pipeline(ids_ref, tbl_ref, tile_a, tile_b, consume, jnp.zeros((1, PEER_SLOTS), f32))


def _peer_u(ids_flat, h2r, gate3, tbl, tok_off, n_tok):
    t = T_PEER
    b0 = tok_off // t
    return pl.pallas_call(
        _peer_u_kernel,
        grid=(n_tok // t,),
        in_specs=[pl.BlockSpec((t * PEER_SLOTS,), lambda i: (i + b0,), memory_space=pltpu.SMEM),
                  pl.BlockSpec((t * SUBLANES, LANES), lambda i: (i + b0, 0)),
                  pl.BlockSpec((t, 1, PEER_SLOTS), lambda i: (i + b0, 0, 0)),
                  _const_spec(tbl.shape)],
        out_specs=pl.BlockSpec((t, 1, PEER_SLOTS), lambda i: (i, 0, 0)),
        out_shape=jax.ShapeDtypeStruct((n_tok, 1, PEER_SLOTS), f32),
        scratch_shapes=[_tile_scratch(), _tile_scratch()],
        compiler_params=_cparams(("parallel",)),
        name="peer_u",
    )(ids_flat, h2r, gate3, tbl)


def _peer_v_sc(ids, w, tbl_words, tok_off):
    n_tok = w.shape[0]
    tpw = n_tok // SC_WORKERS
    lanes = SC_LANES
    mesh = plsc.VectorSubcoreMesh(core_axis_name="c", subcore_axis_name="s")

    @functools.partial(
        pl.kernel, mesh=mesh,
        out_type=jax.ShapeDtypeStruct((n_tok, D_MODEL), f32),
        scratch_types=[
            pltpu.VMEM((SC_HALF,), i32), pltpu.VMEM((SC_HALF,), i32),
            pltpu.VMEM((SC_HALF, ROW_WORDS_FLAT), i32), pltpu.VMEM((SC_HALF, ROW_WORDS_FLAT), i32),
            pltpu.VMEM((PEER_SLOTS,), f32),
            pltpu.VMEM((D_MODEL,), f32),
            pltpu.SemaphoreType.DMA, pltpu.SemaphoreType.DMA,
        ],
        compiler_params=dataclasses.replace(pltpu.CompilerParams(), needs_layout_passes=False),
        name="peer_v_sc",
    )
    def run(ids_hbm, w_hbm, tbl_hbm, out_hbm, idx_a, idx_b, rows_a, rows_b, w_v, out_v, sem_a, sem_b):
        wid = lax.axis_index("s") * SC_CORES + lax.axis_index("c")
        base = wid * tpw

        def accumulate(rows, w_off):
            for db in range(ROW_WORDS_FLAT // SC_DIM_BLOCK):
                def group_body(g, carry):
                    accs = []
                    for wc in range(SC_DIM_BLOCK // lanes):
                        accs.append(out_v[pl.ds(db * SC_DIM_BLOCK + wc * lanes, lanes)])
                        accs.append(out_v[pl.ds(ROW_WORDS_FLAT + db * SC_DIM_BLOCK + wc * lanes, lanes)])
                    j0 = g * SC_ROW_GROUP
                    wchunk = w_v[pl.ds(pl.multiple_of((w_off + j0) // lanes * lanes, lanes), lanes)]
                    sub = (w_off + j0) % lanes
                    for r in range(SC_ROW_GROUP):
                        wj = jnp.take(wchunk, jnp.full((lanes,), sub + r, i32))
                        for wc in range(SC_DIM_BLOCK // lanes):
                            word = rows[j0 + r, pl.ds(db * SC_DIM_BLOCK + wc * lanes, lanes)]
                            lo = lax.bitcast_convert_type(word << 16, f32)
                            hi = lax.bitcast_convert_type(word & jnp.int32(-65536), f32)
                            accs[2 * wc] = accs[2 * wc] + wj * lo
                            accs[2 * wc + 1] = accs[2 * wc + 1] + wj * hi
                    for wc in range(SC_DIM_BLOCK // lanes):
                        out_v[pl.ds(db * SC_DIM_BLOCK + wc * lanes, lanes)] = accs[2 * wc]
                        out_v[pl.ds(ROW_WORDS_FLAT + db * SC_DIM_BLOCK + wc * lanes, lanes)] = accs[2 * wc + 1]
                    return carry
                lax.fori_loop(0, SC_HALF // SC_ROW_GROUP, group_body, 0)

        def token(i, carry):
            t = base + i
            pltpu.sync_copy(ids_hbm.at[tok_off + t, pl.ds(0, SC_HALF)], idx_a)
            pltpu.sync_copy(ids_hbm.at[tok_off + t, pl.ds(SC_HALF, SC_HALF)], idx_b)
            copy_a = pltpu.async_copy(tbl_hbm.at[idx_a], rows_a, sem_a)
            copy_b = pltpu.async_copy(tbl_hbm.at[idx_b], rows_b, sem_b)
            pltpu.sync_copy(w_hbm.at[t], w_v)
            for q in range(D_MODEL // lanes):
                out_v[pl.ds(q * lanes, lanes)] = jnp.zeros((lanes,), f32)
            copy_a.wait()
            accumulate(rows_a, 0)
            copy_b.wait()
            accumulate(rows_b, SC_HALF)
            pltpu.sync_copy(out_v, out_hbm.at[t])
            return carry

        lax.fori_loop(0, tpw, token, 0)

    return run(ids, w, tbl_words)


def _final_kernel(x1_ref, p_ref, g_ref, y_ref):
    x = x1_ref[...] + p_ref[...]
    ms = jnp.mean(x * x, axis=-1, keepdims=True)
    y_ref[...] = x * lax.rsqrt(ms + EPS) * g_ref[...]


def _final(x1, peer_out, final_g, tok_off):
    n = peer_out.shape[0]
    t = T_PROJ
    b0 = tok_off // t
    row = pl.BlockSpec((t, D_MODEL), lambda i: (i, 0))
    return pl.pallas_call(
        _final_kernel,
        grid=(n // t,),
        in_specs=[pl.BlockSpec((t, D_MODEL), lambda i: (i + b0, 0)), row, _const_spec((1, D_MODEL))],
        out_specs=row,
        out_shape=jax.ShapeDtypeStruct((n, D_MODEL), f32),
        compiler_params=_cparams(("parallel",)),
        name="final_norm",
    )(x1, peer_out, final_g)


def _rotary_tables(seq):
    half = ROT_DIM // 2
    inv = ROPE_THETA ** (-jnp.arange(half, dtype=f32) * 2.0 / ROT_DIM)
    ang = jnp.arange(seq, dtype=jnp.int32).astype(f32)[:, None] * inv[None, :]
    cos = jnp.cos(ang)
    sin = jnp.sin(ang)
    pad = HEAD_DIM - ROT_DIM
    one = jnp.ones((seq, pad), f32)
    zero = jnp.zeros((seq, pad), f32)
    zh = jnp.zeros((seq, half), f32)
    cos_h = jnp.concatenate([cos, cos, one], axis=1)
    sa_h = jnp.concatenate([-sin, zh, zero], axis=1)
    sb_h = jnp.concatenate([zh, sin, zero], axis=1)
    rep = LANES // HEAD_DIM
    return tuple(jnp.tile(t, (1, rep)) for t in (cos_h, sa_h, sb_h))


def _front_ops(x, params, tables, st):
    (norm1_g, w_in_bf, b_gate, merge_w, keys_bf, _, _, _) = params
    batch, seq, _ = x.shape
    n = batch * seq
    x2 = x.reshape(n, D_MODEL)

    def inproj():
        st["q"], st["k"], st["v"], st["u"], st["gates"] = _inproj(x2, seq, norm1_g, w_in_bf, b_gate, *tables)
        st["o"], st["lse"] = [], []

    def attention(g):
        o, lse = _attention_group(st["q"][g], st["k"][g], st["v"][g], batch, seq, g)
        st["o"].append(o)
        st["lse"].append(lse)

    def merge():
        st["x1"], st["h2r"], st["qp"] = _merge(x2, seq, st["u"], st["gates"], st["o"], st["lse"], merge_w)

    def route():
        ids, gate = _route(st["qp"], keys_bf)
        st["ids"], st["gate3"], st["n"] = ids, gate.reshape(n, 1, PEER_SLOTS), n

    return [inproj] + [functools.partial(attention, g) for g in range(N_GROUPS)] + [merge, route]


def _peer_ops(params, st, pending):
    u_tbl, v_tbl = params[5], params[6]

    def chunk(off, n_chunk):
        ids = st["ids"]
        w3 = _peer_u(ids.reshape(st["n"] * PEER_SLOTS), st["h2r"], st["gate3"], u_tbl, off, n_chunk)
        pv = _peer_v_sc(ids, w3.reshape(n_chunk, PEER_SLOTS), v_tbl, off)
        pending.append((st["x1"], pv, off))

    ops, off = [], 0
    for share in PEER_CHUNK_SHARES:
        n_chunk = st["tokens"] * share // sum(PEER_CHUNK_SHARES)
        ops.append(functools.partial(chunk, off, n_chunk))
        off += n_chunk
    return ops


def kernel(x_prompt, x_sample, norm1_g, w_in, b_gate, w_attn_up, conv_dw_w, conv_dw_b, conv_ln_g, conv_ln_b,
           conv_pw_w, conv_pw_b, w_out, norm2_g, peer_wq, peer_keys, peer_u, peer_v, final_g):
    assert w_in.shape[0] == 1, "one encoder layer followed by the final norm"
    row = lambda a: a.reshape(1, -1)
    tables = _rotary_tables(max(x_prompt.shape[1], x_sample.shape[1]))
    l = 0
    merge_w = (conv_dw_w[l], row(conv_dw_b[l]), row(conv_ln_g[l]), row(conv_ln_b[l]),
               conv_pw_w[l].astype(bf16), row(conv_pw_b[l]), w_attn_up[l].astype(bf16),
               w_out[l].astype(bf16), row(norm2_g[l]), peer_wq[l].astype(bf16))
    keys_bf = peer_keys[l].astype(bf16).reshape(2 * PEER_HEADS, N_SUBKEYS, -1)
    params = (row(norm1_g[l]), w_in[l].astype(bf16), row(b_gate[l]), merge_w, keys_bf,
              _pack_table(peer_u[l]), _pack_table(peer_v[l]).reshape(N_EXPERTS, ROW_WORDS_FLAT), row(final_g))

    subs = []
    for x, shares in zip((x_prompt, x_sample), BATCH_SHARES):
        b = 0
        for share in shares:
            step = x.shape[0] * share // sum(shares)
            subs.append(x[b:b + step])
            b += step
    states = [dict(tokens=sub.shape[0] * sub.shape[1]) for sub in subs]
    pendings = [[] for _ in subs]
    for op in _front_ops(subs[0], params, tables, states[0]):
        op()
    for i in range(len(subs)):
        peer = _peer_ops(params, states[i], pendings[i])
        front = _front_ops(subs[i + 1], params, tables, states[i + 1]) if i + 1 < len(subs) else []
        for j in range(max(len(peer), len(front))):
            if j < len(peer):
                peer[j]()
            if j < len(front):
                front[j]()

    def finish(x, pending):
        ys = [_final(x1, pv, params[-1], off) for x1, pv, off in pending]
        return jnp.concatenate(ys, axis=0).reshape(x.shape)

    half = len(BATCH_SHARES[0])
    return (finish(x_prompt, sum(pendings[:half], [])), finish(x_sample, sum(pendings[half:], [])))
```
